```python
import math, functools
import jax, jax.numpy as jnp
from jax import lax
import numpy as np

D_MODEL = 2048
BATCH = 1
SEQ = 8192
DEPTH = 2
DEC_BATCH = 32
DEC_SEQ = 16
PAST_LEN = 4096

CHUNK = 64
WINDOW = 128
WIN_CHUNKS = WINDOW // CHUNK
ATT_HEAD_DIM = 64
ATT_HEADS = D_MODEL // 128
ATT_KV_HEADS = ATT_HEADS // 4
ATT_GROUP = ATT_HEADS // ATT_KV_HEADS
ATT_WIDTH = ATT_HEADS * ATT_HEAD_DIM
KV_WIDTH = ATT_KV_HEADS * ATT_HEAD_DIM
RET_HEAD_DIM = 128
RET_HEADS = D_MODEL // 256
RET_WIDTH = RET_HEADS * RET_HEAD_DIM
MIX_WIDTH = ATT_WIDTH + RET_WIDTH
IN_WIDTH = ATT_WIDTH + 2 * KV_WIDTH + 4 * RET_WIDTH
SPLITS = [ATT_WIDTH, ATT_WIDTH + KV_WIDTH, ATT_WIDTH + 2 * KV_WIDTH,
          ATT_WIDTH + 2 * KV_WIDTH + RET_WIDTH, ATT_WIDTH + 2 * KV_WIDTH + 2 * RET_WIDTH,
          ATT_WIDTH + 2 * KV_WIDTH + 3 * RET_WIDTH]
D_FF = 4 * D_MODEL
PLE_DIM = 256
N_BUCKETS = 32
MAX_DISTANCE = 128
ROPE_BASE = 10000.0
EPS = 1e-6
ATT_SCALE = ATT_HEAD_DIM ** -0.5
RET_SCALE = RET_HEAD_DIM ** -0.5

kernel_name = "hybrid_swa_sink_retention_stream_step"


def rmsnorm(x, g):
    xf = x.astype(jnp.float32)
    y = xf * lax.rsqrt(jnp.mean(xf * xf, axis=-1, keepdims=True) + EPS)
    return (y * g.astype(jnp.float32)).astype(x.dtype)


def t5_bucket(rel):
    nb = N_BUCKETS // 2
    max_exact = nb // 2
    ret = jnp.where(rel > 0, nb, 0)
    n = jnp.abs(rel)
    nf = jnp.maximum(n, max_exact).astype(jnp.float32)
    large = max_exact + (jnp.log(nf / max_exact) / math.log(MAX_DISTANCE / max_exact)
                         * (nb - max_exact)).astype(jnp.int32)
    large = jnp.minimum(large, nb - 1)
    return ret + jnp.where(n < max_exact, n, large)


def rel_bias_heads(table, rel):
    b = table[t5_bucket(rel)].astype(jnp.float32)
    q_len, k_len = rel.shape
    return jnp.transpose(b, (2, 0, 1)).reshape(ATT_KV_HEADS, ATT_GROUP, q_len, k_len)


def rotary(x, pos):
    half = x.shape[-1] // 2
    inv = ROPE_BASE ** (-jnp.arange(half, dtype=jnp.float32) / half)
    ang = pos.astype(jnp.float32)[:, None] * inv[None, :]
    cos = jnp.cos(ang)[None, :, None, :]
    sin = jnp.sin(ang)[None, :, None, :]
    x1, x2 = x[..., :half], x[..., half:]
    return jnp.concatenate([x1 * cos - x2 * sin, x2 * cos + x1 * sin], axis=-1)


def sink_attention(q, k, v, bias, sink, mask):
    s = jnp.einsum('...qkgd,...skd->...kgqs', q, k, preferred_element_type=jnp.float32) * ATT_SCALE
    s = s + bias
    if mask is not None:
        s = jnp.where(mask, s, -jnp.inf)
    sk = sink.astype(jnp.float32).reshape(ATT_KV_HEADS, ATT_GROUP, 1, 1)
    m = jnp.maximum(jnp.max(s, axis=-1, keepdims=True), sk)
    pr = jnp.exp(s - m)
    den = jnp.sum(pr, axis=-1, keepdims=True) + jnp.exp(sk - m)
    pr = (pr / den).astype(v.dtype)
    return jnp.einsum('...kgqs,...skd->...qkgd', pr, v)


def band(x):
    b, s = x.shape[:2]
    nc = s // CHUNK
    xp = jnp.pad(x, ((0, 0), (WINDOW, 0), (0, 0), (0, 0)))
    xp = xp.reshape(b, nc + WIN_CHUNKS, CHUNK, ATT_KV_HEADS, ATT_HEAD_DIM)
    return jnp.concatenate([xp[:, j:j + nc] for j in range(WIN_CHUNKS + 1)], axis=2)


def attend_prompt(q, k, v, sink, *, bias, mask):
    b, s = q.shape[:2]
    nc = s // CHUNK
    qc = q.reshape(b, nc, CHUNK, ATT_KV_HEADS, ATT_GROUP, ATT_HEAD_DIM)
    o = sink_attention(qc, band(k), band(v), bias, sink, mask)
    return o.reshape(b, s, ATT_WIDTH), (k[:, -WINDOW:], v[:, -WINDOW:])


def attend_sample(q, k, v, sink, *, bias, cache_k, cache_v):
    b, l = q.shape[:2]
    kk = jnp.concatenate([cache_k.astype(k.dtype), k], axis=1)
    vv = jnp.concatenate([cache_v.astype(v.dtype), v], axis=1)
    o = sink_attention(q, kk, vv, bias, sink, None)
    return o.reshape(b, l, ATT_WIDTH), (k, v)


def log_gamma():
    return jnp.log(1.0 - 2.0 ** (-5.0 - jnp.arange(RET_HEADS, dtype=jnp.float32)))


def retention(q, k, v, g, gain, pos, state0, chunk):
    b, l = q.shape[:2]
    nc = l // chunk
    f32 = jnp.float32
    shp = (b, l, RET_HEADS, RET_HEAD_DIM)
    q = rotary(q.reshape(shp).astype(f32), pos)
    k = rotary(k.reshape(shp).astype(f32), pos) * RET_SCALE
    v = v.reshape(shp).astype(f32)
    cshp = (b, nc, chunk, RET_HEADS, RET_HEAD_DIM)
    qc, kc, vc = q.reshape(cshp), k.reshape(cshp), v.reshape(cshp)
    lg = log_gamma()
    idx = jnp.arange(chunk, dtype=f32)
    diff = idx[:, None] - idx[None, :]
    decay = jnp.where(diff >= 0, jnp.exp(lg[:, None, None] * jnp.maximum(diff, 0.0)), 0.0)
    zeta = jnp.exp(lg[:, None] * (chunk - 1 - idx))
    xi = jnp.exp(lg[:, None] * (idx + 1))
    g_chunk = jnp.exp(lg * chunk)[:, None, None]
    scores = jnp.einsum('bcihd,bcjhd->bchij', qc, kc) * decay
    o = jnp.einsum('bchij,bcjhe->bcihe', scores, vc)
    upd = jnp.einsum('bcjhd,hj,bcjhe->bchde', kc, zeta, vc)

    def step(s, u):
        return g_chunk * s + u, s

    s_final, s_prev = lax.scan(step, state0.astype(f32), jnp.moveaxis(upd, 1, 0))
    o = o + jnp.einsum('bcihd,cbhde,hi->bcihe', qc, s_prev, xi)
    o = o.reshape(b, l, RET_HEADS, RET_HEAD_DIM)
    o = o * lax.rsqrt(jnp.mean(o * o, axis=-1, keepdims=True) + EPS)
    o = o.reshape(b, l, RET_WIDTH) * gain.astype(f32)
    return (jax.nn.silu(g.astype(f32)) * o).astype(g.dtype), s_final.astype(state0.dtype)


def layer_forward(h, p_l, pos, attend, ret_state, ret_chunk, lw):
    (n_mix, w_in, sink, r_gain, w_out, n_mlp, w_up, w_down, n_ple, w_gate, w_ple) = lw
    b, l, _ = h.shape
    u = rmsnorm(h, n_mix)
    qa, ka, va, qr, kr, vr, gr = jnp.split(u @ w_in, SPLITS, axis=-1)
    qa = qa.reshape(b, l, ATT_KV_HEADS, ATT_GROUP, ATT_HEAD_DIM)
    ka = ka.reshape(b, l, ATT_KV_HEADS, ATT_HEAD_DIM)
    va = va.reshape(b, l, ATT_KV_HEADS, ATT_HEAD_DIM)
    oa, kv_rows = attend(qa, ka, va, sink)
    orr, ret_new = retention(qr, kr, vr, gr, r_gain, pos, ret_state, ret_chunk)
    h = h + jnp.concatenate([oa, orr.astype(oa.dtype)], axis=-1) @ w_out
    h = h + jnp.square(jax.nn.relu(rmsnorm(h, n_mlp) @ w_up)) @ w_down
    gate = jax.nn.sigmoid(rmsnorm(h, n_ple) @ w_gate)
    h = h + gate * (p_l.astype(h.dtype) @ w_ple)
    return h, kv_rows, ret_new


def setup_inputs(seed: int = 0) -> dict:
    key = jax.random.key(seed)
    ks = jax.random.split(key, 24)
    f32 = jnp.float32

    def nrm(k, shape, scale):
        return jax.random.normal(k, shape, f32) * scale

    def gain(k, shape):
        return 1.0 + 0.05 * jax.random.normal(k, shape, f32)

    return {
        "x_prompt": nrm(ks[0], (BATCH, SEQ, D_MODEL), 1.0),
        "x_sample": nrm(ks[1], (DEC_BATCH, DEC_SEQ, D_MODEL), 1.0),
        "cache_k": nrm(ks[2], (DEPTH, DEC_BATCH, WINDOW, ATT_KV_HEADS, ATT_HEAD_DIM), 1.0),
        "cache_v": nrm(ks[3], (DEPTH, DEC_BATCH, WINDOW, ATT_KV_HEADS, ATT_HEAD_DIM), 1.0),
        "state_ret": nrm(ks[4], (DEPTH, DEC_BATCH, RET_HEADS, RET_HEAD_DIM, RET_HEAD_DIM), 0.5),
        "p_prompt": nrm(ks[5], (DEPTH, BATCH, SEQ, PLE_DIM), 1.0),
        "p_sample": nrm(ks[6], (DEPTH, DEC_BATCH, DEC_SEQ, PLE_DIM), 1.0),
        "rel_bias": nrm(ks[7], (N_BUCKETS, ATT_HEADS), 0.5),
        "norm_mix": gain(ks[8], (DEPTH, D_MODEL)),
        "w_in": nrm(ks[9], (DEPTH, D_MODEL, IN_WIDTH), D_MODEL ** -0.5),
        "attn_sink": nrm(ks[10], (DEPTH, ATT_HEADS), 0.5),
        "ret_gain": gain(ks[11], (DEPTH, RET_WIDTH)),
        "w_out": nrm(ks[12], (DEPTH, MIX_WIDTH, D_MODEL), MIX_WIDTH ** -0.5),
        "norm_mlp": gain(ks[13], (DEPTH, D_MODEL)),
        "w_up": nrm(ks[14], (DEPTH, D_MODEL, D_FF), D_MODEL ** -0.5),
        "w_down": nrm(ks[15], (DEPTH, D_FF, D_MODEL), D_FF ** -0.5),
        "norm_ple": gain(ks[16], (DEPTH, D_MODEL)),
        "w_gate": nrm(ks[17], (DEPTH, D_MODEL, D_MODEL), D_MODEL ** -0.5),
        "w_ple": nrm(ks[18], (DEPTH, PLE_DIM, D_MODEL), PLE_DIM ** -0.5),
        "norm_final": gain(ks[19], (D_MODEL,)),
    }


def reference(x_prompt, x_sample, cache_k, cache_v, state_ret, p_prompt, p_sample,
              rel_bias, norm_mix, w_in, attn_sink, ret_gain, w_out, norm_mlp, w_up, w_down,
              norm_ple, w_gate, w_ple, norm_final):
    bsz, seq = x_prompt.shape[:2]
    dec_l = x_sample.shape[1]
    n_chunks = seq // CHUNK
    band_len = (WIN_CHUNKS + 1) * CHUNK
    q_idx = jnp.arange(CHUNK, dtype=jnp.int32)
    s_idx = jnp.arange(band_len, dtype=jnp.int32)
    bias_p = rel_bias_heads(rel_bias, s_idx[None, :] - WINDOW - q_idx[:, None])
    key_pos = jnp.arange(n_chunks, dtype=jnp.int32)[:, None] * CHUNK - WINDOW + s_idx[None, :]
    mask_p = (key_pos >= 0)[:, None, None, None, :]
    bias_s = rel_bias_heads(rel_bias, jnp.arange(WINDOW + dec_l, dtype=jnp.int32)[None, :] - WINDOW
                            - jnp.arange(dec_l, dtype=jnp.int32)[:, None])
    pos_p = jnp.arange(seq, dtype=jnp.int32)
    pos_s = PAST_LEN + jnp.arange(dec_l, dtype=jnp.int32)
    state0 = jnp.zeros((bsz, RET_HEADS, RET_HEAD_DIM, RET_HEAD_DIM), state_ret.dtype)

    hp, hs = x_prompt, x_sample
    kp_l, vp_l, sp_l, ks_l, vs_l, ss_l = [], [], [], [], [], []
    for l in range(DEPTH):
        lw = (norm_mix[l], w_in[l], attn_sink[l], ret_gain[l], w_out[l], norm_mlp[l],
              w_up[l], w_down[l], norm_ple[l], w_gate[l], w_ple[l])
        hp, (k_p, v_p), s_p = layer_forward(
            hp, p_prompt[l], pos_p,
            functools.partial(attend_prompt, bias=bias_p, mask=mask_p),
            state0, CHUNK, lw)
        hs, (k_s, v_s), s_s = layer_forward(
            hs, p_sample[l], pos_s,
            functools.partial(attend_sample, bias=bias_s, cache_k=cache_k[l], cache_v=cache_v[l]),
            state_ret[l], dec_l, lw)
        kp_l.append(k_p); vp_l.append(v_p); sp_l.append(s_p)
        ks_l.append(k_s); vs_l.append(v_s); ss_l.append(s_s)

    y_prompt = rmsnorm(hp, norm_final)
    y_sample = rmsnorm(hs, norm_final)
    new_k_prompt = jnp.stack(kp_l)
    new_v_prompt = jnp.stack(vp_l)
    new_state_prompt = jnp.stack(sp_l)
    new_k_sample = jnp.stack(ks_l)
    new_v_sample = jnp.stack(vs_l)
    new_state_sample = jnp.stack(ss_l)
    return (y_prompt, y_sample, new_k_prompt, new_v_prompt, new_state_prompt,
            new_k_sample, new_v_sample, new_state_sample)
```

```python
import functools
import math

import jax
import jax.numpy as jnp
from jax import lax
from jax.experimental import pallas as pl
from jax.experimental.pallas import tpu as pltpu

F32 = jnp.float32
BF16 = jnp.bfloat16

D_MODEL = 2048
SEQ = 8192
DEPTH = 2
DEC_BATCH = 32
DEC_SEQ = 16
PAST_LEN = 4096
N_SAMPLE = DEC_BATCH * DEC_SEQ
N_TOK = SEQ + N_SAMPLE

CHUNK = 64
WINDOW = 128
BAND = WINDOW + CHUNK
HEAD_DIM = 64
N_HEADS = 16
N_KV = 4
GROUP = N_HEADS // N_KV
ATT_WIDTH = N_HEADS * HEAD_DIM
KV_WIDTH = N_KV * HEAD_DIM
RET_DIM = 128
RET_HEADS = 8
RET_WIDTH = RET_HEADS * RET_DIM
IN_WIDTH = ATT_WIDTH + 2 * KV_WIDTH + 4 * RET_WIDTH
D_FF = 4 * D_MODEL
PLE_DIM = 256
N_BUCKETS = 32
MAX_DISTANCE = 128
ROPE_BASE = 10000.0
EPS = 1e-6
ATT_SCALE = HEAD_DIM ** -0.5
RET_SCALE = RET_DIM ** -0.5

ROW_TILE = 512
MLP_ROW_TILE = 1088
MLP_FF_TILE = 512
ATT_TILE = 512
RET_TILE = 256
SAMPLE_BLOCK = 8
VMEM_LIMIT = 58 * 1024 * 1024

NT_DIMS = (((1,), (1,)), ((), ()))
TN_DIMS = (((0,), (0,)), ((), ()))


def _params(*sem):
    return pltpu.CompilerParams(dimension_semantics=sem, vmem_limit_bytes=VMEM_LIMIT)


def _resident(shape):
    nd = len(shape)
    return pl.BlockSpec(shape, lambda *_: (0,) * nd, pipeline_mode=pl.Buffered(1))


def _rms(x, g):
    ms = jnp.mean(x * x, axis=-1, keepdims=True)
    return x * lax.rsqrt(ms + EPS) * g


def _bias_kernel(table_ref, bp_ref, bs_ref, op_ref, os_ref):
    bp = bp_ref[...]
    bs = bs_ref[...]

    def per_head(h, c):
        ap = jnp.zeros(bp.shape, F32)
        as_ = jnp.zeros(bs.shape, F32)
        for b in range(N_BUCKETS):
            t = table_ref[b, h]
            ap = jnp.where(bp == b, t, ap)
            as_ = jnp.where(bs == b, t, as_)
        op_ref[h] = ap
        os_ref[h] = as_
        return c

    lax.fori_loop(0, N_HEADS, per_head, 0)


def _rel_bias(table, bucket_p, bucket_s):
    return pl.pallas_call(
        _bias_kernel,
        out_shape=(jax.ShapeDtypeStruct((N_HEADS,) + bucket_p.shape, F32),
                   jax.ShapeDtypeStruct((N_HEADS,) + bucket_s.shape, F32)),
        in_specs=[pl.BlockSpec(memory_space=pltpu.SMEM),
                  pl.BlockSpec(memory_space=pltpu.VMEM),
                  pl.BlockSpec(memory_space=pltpu.VMEM)],
        out_specs=(pl.BlockSpec(memory_space=pltpu.VMEM), pl.BlockSpec(memory_space=pltpu.VMEM)),
        name="rel_bias",
    )(table, bucket_p, bucket_s)


def _in_proj_kernel(h_ref, g_ref, w_ref, cos_ref, sin_ref,
                    qa_ref, k_ref, v_ref, qr_ref, kr_ref, vr_ref, gr_ref):
    xn = _rms(h_ref[...], g_ref[...]).astype(BF16)

    def proj(lo, width):
        return jnp.dot(xn, w_ref[:, lo:lo + width], preferred_element_type=F32)

    qa_ref[...] = (proj(0, ATT_WIDTH) * ATT_SCALE).astype(BF16)
    k_ref[...] = proj(ATT_WIDTH, KV_WIDTH)
    v_ref[...] = proj(ATT_WIDTH + KV_WIDTH, KV_WIDTH)
    cos = cos_ref[...]
    sin = sin_ref[...]
    base = ATT_WIDTH + 2 * KV_WIDTH

    def rot(x):
        return x * cos + pltpu.roll(x, RET_DIM // 2, 1) * sin

    q = proj(base, RET_WIDTH)
    for h in range(RET_HEADS):
        sl = slice(h * RET_DIM, (h + 1) * RET_DIM)
        qr_ref[:, sl] = rot(q[:, sl]).astype(BF16)
    k = proj(base + RET_WIDTH, RET_WIDTH)
    for h in range(RET_HEADS):
        sl = slice(h * RET_DIM, (h + 1) * RET_DIM)
        kr_ref[:, sl] = (rot(k[:, sl]) * RET_SCALE).astype(BF16)
    vr_ref[...] = proj(base + 2 * RET_WIDTH, RET_WIDTH).astype(BF16)
    gr_ref[...] = proj(base + 3 * RET_WIDTH, RET_WIDTH).astype(BF16)


def _in_proj(h, g, w, cos2, sin2):
    tm = ROW_TILE
    row = lambda width: pl.BlockSpec((tm, width), lambda i: (i, 0))
    out = lambda width, dt: jax.ShapeDtypeStruct((N_TOK, width), dt)
    return pl.pallas_call(
        _in_proj_kernel,
        grid=(N_TOK // tm,),
        in_specs=[row(D_MODEL), _resident((1, D_MODEL)), _resident((D_MODEL, IN_WIDTH)),
                  row(RET_DIM), row(RET_DIM)],
        out_specs=(row(ATT_WIDTH), row(KV_WIDTH), row(KV_WIDTH),
                   row(RET_WIDTH), row(RET_WIDTH), row(RET_WIDTH), row(RET_WIDTH)),
        out_shape=(out(ATT_WIDTH, BF16), out(KV_WIDTH, F32), out(KV_WIDTH, F32),
                   out(RET_WIDTH, BF16), out(RET_WIDTH, BF16), out(RET_WIDTH, BF16),
                   out(RET_WIDTH, BF16)),
        compiler_params=_params("parallel"),
        name="in_proj",
    )(h, g, w, cos2, sin2)


def _softmax_sink(pieces, sink):
    m = sink
    for s in pieces:
        m = jnp.maximum(m, jnp.max(s, axis=-1, keepdims=True))
    ps = [jnp.exp(s - m) for s in pieces]
    den = jnp.exp(sink - m)
    for p in ps:
        den = den + jnp.sum(p, axis=-1, keepdims=True)
    inv = 1.0 / den
    return [(p * inv).astype(BF16) for p in ps]


def _attn_prompt_kernel(q_ref, kc_ref, kp_ref, vc_ref, vp_ref, bias_ref, sink_ref, o_ref,
                        kfull, vfull):
    i = pl.program_id(0)
    kfull[0:WINDOW, :] = kp_ref[...].astype(BF16)
    kfull[WINDOW:, :] = kc_ref[...].astype(BF16)
    vfull[0:WINDOW, :] = vp_ref[...].astype(BF16)
    vfull[WINDOW:, :] = vc_ref[...].astype(BF16)
    col = lax.broadcasted_iota(jnp.int32, (GROUP * CHUNK, BAND), 1)

    def chunk(j, c):
        r0 = pl.multiple_of(j * CHUNK, CHUNK)
        qc = q_ref[pl.ds(r0, CHUNK), :]
        valid = col >= WINDOW - (i * ATT_TILE + r0)
        for kv in range(N_KV):
            qs = jnp.concatenate(
                [qc[:, (kv * GROUP + g) * HEAD_DIM:(kv * GROUP + g + 1) * HEAD_DIM]
                 for g in range(GROUP)], axis=0)
            ksl = slice(kv * HEAD_DIM, (kv + 1) * HEAD_DIM)
            kb = kfull[pl.ds(r0, BAND), ksl]
            vb = vfull[pl.ds(r0, BAND), ksl]
            s = lax.dot_general(qs, kb, NT_DIMS, preferred_element_type=F32) + bias_ref[kv]
            s = jnp.where(valid, s, -jnp.inf)
            (p,) = _softmax_sink([s], sink_ref[kv])
            o = jnp.dot(p, vb, preferred_element_type=F32)
            ok = jnp.concatenate([o[g * CHUNK:(g + 1) * CHUNK] for g in range(GROUP)], axis=1)
            o_ref[pl.ds(r0, CHUNK), kv * GROUP * HEAD_DIM:(kv + 1) * GROUP * HEAD_DIM] = (
                ok.astype(BF16))
        return c

    lax.fori_loop(0, ATT_TILE // CHUNK, chunk, 0)


def _attn_prompt(qa, k, v, bias, sink):
    t = ATT_TILE
    per = t // WINDOW
    cur = lambda width: pl.BlockSpec((t, width), lambda i: (i, 0))
    prev = pl.BlockSpec((WINDOW, KV_WIDTH), lambda i: (jnp.maximum(i * per - 1, 0), 0))
    return pl.pallas_call(
        _attn_prompt_kernel,
        grid=(SEQ // t,),
        in_specs=[cur(ATT_WIDTH), cur(KV_WIDTH), prev, cur(KV_WIDTH), prev,
                  _resident(bias.shape), _resident(sink.shape)],
        out_specs=cur(ATT_WIDTH),
        out_shape=jax.ShapeDtypeStruct((SEQ, ATT_WIDTH), BF16),
        scratch_shapes=[pltpu.VMEM((t + WINDOW, KV_WIDTH), BF16),
                        pltpu.VMEM((t + WINDOW, KV_WIDTH), BF16)],
        compiler_params=_params("parallel"),
        name="attn_prompt",
    )(qa, k, k, v, v, bias, sink)


def _attn_sample_kernel(q_ref, kn_ref, vn_ref, ck_ref, cv_ref, bc_ref, bn_ref, sink_ref, o_ref):
    def stream(b, c):
        r0 = pl.multiple_of(b * DEC_SEQ, DEC_SEQ)
        qc = q_ref[pl.ds(r0, DEC_SEQ), :]
        kn = kn_ref[pl.ds(r0, DEC_SEQ), :].astype(BF16)
        vn = vn_ref[pl.ds(r0, DEC_SEQ), :].astype(BF16)
        ck = ck_ref[b].astype(BF16)
        cv = cv_ref[b].astype(BF16)
        for kv in range(N_KV):
            qs = jnp.concatenate(
                [qc[:, (kv * GROUP + g) * HEAD_DIM:(kv * GROUP + g + 1) * HEAD_DIM]
                 for g in range(GROUP)], axis=0)
            ksl = slice(kv * HEAD_DIM, (kv + 1) * HEAD_DIM)
            s1 = lax.dot_general(qs, ck[:, ksl], NT_DIMS, preferred_element_type=F32) + bc_ref[kv]
            s2 = lax.dot_general(qs, kn[:, ksl], NT_DIMS, preferred_element_type=F32) + bn_ref[kv]
            p1, p2 = _softmax_sink([s1, s2], sink_ref[kv])
            o = (jnp.dot(p1, cv[:, ksl], preferred_element_type=F32)
                 + jnp.dot(p2, vn[:, ksl], preferred_element_type=F32))
            ok = jnp.concatenate([o[g * DEC_SEQ:(g + 1) * DEC_SEQ] for g in range(GROUP)], axis=1)
            o_ref[pl.ds(r0, DEC_SEQ), kv * GROUP * HEAD_DIM:(kv + 1) * GROUP * HEAD_DIM] = (
                ok.astype(BF16))
        return c

    lax.fori_loop(0, SAMPLE_BLOCK, stream, 0)


def _attn_sample(qa, k, v, cache_k, cache_v, bias_c, bias_n, sink):
    rows = SAMPLE_BLOCK * DEC_SEQ
    first = SEQ // rows
    new = lambda width: pl.BlockSpec((rows, width), lambda i: (first + i, 0))
    cache = pl.BlockSpec((SAMPLE_BLOCK, WINDOW, KV_WIDTH), lambda i: (i, 0, 0))
    return pl.pallas_call(
        _attn_sample_kernel,
        grid=(DEC_BATCH // SAMPLE_BLOCK,),
        in_specs=[new(ATT_WIDTH), new(KV_WIDTH), new(KV_WIDTH), cache, cache,
                  _resident(bias_c.shape), _resident(bias_n.shape), _resident(sink.shape)],
        out_specs=pl.BlockSpec((rows, ATT_WIDTH), lambda i: (i, 0)),
        out_shape=jax.ShapeDtypeStruct((N_SAMPLE, ATT_WIDTH), BF16),
        compiler_params=_params("parallel"),
        name="attn_sample",
    )(qa, k, v, cache_k, cache_v, bias_c, bias_n, sink)


def _retention_head(q, k, v, g, state, decay, zeta, xi, g_chunk, gain):
    scores = lax.dot_general(q, k, NT_DIMS, preferred_element_type=F32) * decay
    o = jnp.dot(scores.astype(BF16), v, preferred_element_type=F32)
    o = o + jnp.dot(q, state.astype(BF16), preferred_element_type=F32) * xi
    kz = (k.astype(F32) * zeta).astype(BF16)
    upd = lax.dot_general(kz, v, TN_DIMS, preferred_element_type=F32)
    new_state = g_chunk * state + upd
    o = o * lax.rsqrt(jnp.mean(o * o, axis=-1, keepdims=True) + EPS) * gain
    gf = g.astype(F32)
    return gf * jax.nn.sigmoid(gf) * o, new_state


def _ret_prompt_kernel(gc_ref, q_ref, k_ref, v_ref, g_ref, decay_ref, zeta_ref, xi_ref, gain_ref,
                       o_ref, st_ref, state):
    i = pl.program_id(0)

    @pl.when(i == 0)
    def _():
        state[...] = jnp.zeros(state.shape, F32)

    for h in range(RET_HEADS):
        sl = slice(h * RET_DIM, (h + 1) * RET_DIM)
        o, s_new = _retention_head(q_ref[:, sl], k_ref[:, sl], v_ref[:, sl], g_ref[:, sl],
                                   state[h], decay_ref[h], zeta_ref[h], xi_ref[h], gc_ref[h],
                                   gain_ref[:, sl])
        state[h] = s_new
        o_ref[:, sl] = o.astype(BF16)

    @pl.when(i == pl.num_programs(0) - 1)
    def _():
        st_ref[...] = state[...]


def _ret_prompt(qr, kr, vr, gr, consts, gain):
    g_chunk, decay, zeta, xi = consts
    c = RET_TILE
    row = pl.BlockSpec((c, RET_WIDTH), lambda i: (i, 0))
    st_shape = (RET_HEADS, RET_DIM, RET_DIM)
    return pl.pallas_call(
        _ret_prompt_kernel,
        grid=(SEQ // c,),
        in_specs=[pl.BlockSpec(memory_space=pltpu.SMEM), row, row, row, row,
                  _resident(decay.shape), _resident(zeta.shape), _resident(xi.shape),
                  _resident(gain.shape)],
        out_specs=(row, pl.BlockSpec(st_shape, lambda i: (0, 0, 0))),
        out_shape=(jax.ShapeDtypeStruct((SEQ, RET_WIDTH), BF16),
                   jax.ShapeDtypeStruct(st_shape, F32)),
        scratch_shapes=[pltpu.VMEM(st_shape, F32)],
        compiler_params=_params("arbitrary"),
        name="ret_prompt",
    )(g_chunk, qr, kr, vr, gr, decay, zeta, xi, gain)


def _ret_sample_kernel(gc_ref, q_ref, k_ref, v_ref, g_ref, s_ref, decay_ref, zeta_ref, xi_ref,
                       gain_ref, o_ref, st_ref):
    for h in range(RET_HEADS):
        sl = slice(h * RET_DIM, (h + 1) * RET_DIM)
        o, s_new = _retention_head(q_ref[:, sl], k_ref[:, sl], v_ref[:, sl], g_ref[:, sl],
                                   s_ref[0, h], decay_ref[h], zeta_ref[h], xi_ref[h], gc_ref[h],
                                   gain_ref[:, sl])
        st_ref[0, h] = s_new
        o_ref[:, sl] = o.astype(BF16)


def _ret_sample(qr, kr, vr, gr, state, consts, gain):
    g_chunk, decay, zeta, xi = consts
    first = SEQ // DEC_SEQ
    row = pl.BlockSpec((DEC_SEQ, RET_WIDTH), lambda b: (first + b, 0))
    st = pl.BlockSpec((1, RET_HEADS, RET_DIM, RET_DIM), lambda b: (b, 0, 0, 0))
    return pl.pallas_call(
        _ret_sample_kernel,
        grid=(DEC_BATCH,),
        in_specs=[pl.BlockSpec(memory_space=pltpu.SMEM), row, row, row, row, st,
                  _resident(decay.shape), _resident(zeta.shape), _resident(xi.shape),
                  _resident(gain.shape)],
        out_specs=(pl.BlockSpec((DEC_SEQ, RET_WIDTH), lambda b: (b, 0)), st),
        out_shape=(jax.ShapeDtypeStruct((N_SAMPLE, RET_WIDTH), BF16),
                   jax.ShapeDtypeStruct(state.shape, F32)),
        compiler_params=_params("parallel"),
        name="ret_sample",
    )(g_chunk, qr, kr, vr, gr, state, decay, zeta, xi, gain)


def _retention_consts(chunk):
    lg = jnp.log(1.0 - 2.0 ** (-5.0 - jnp.arange(RET_HEADS, dtype=F32)))
    idx = jnp.arange(chunk, dtype=F32)
    diff = idx[:, None] - idx[None, :]
    decay = jnp.where(diff >= 0, jnp.exp(lg[:, None, None] * jnp.maximum(diff, 0.0)), 0.0)
    zeta = jnp.exp(lg[:, None] * (chunk - 1 - idx))[:, :, None]
    xi = jnp.exp(lg[:, None] * (idx + 1))[:, :, None]
    return jnp.exp(lg * chunk), decay, zeta, xi


def _out_proj_kernel(oa_ref, or_ref, h_ref, w_ref, g_ref, h1_ref, xn_ref):
    h1 = (h_ref[...]
          + jnp.dot(oa_ref[...], w_ref[0:ATT_WIDTH, :], preferred_element_type=F32)
          + jnp.dot(or_ref[...], w_ref[ATT_WIDTH:, :], preferred_element_type=F32))
    h1_ref[...] = h1
    xn_ref[...] = _rms(h1, g_ref[...]).astype(BF16)


def _out_proj(oa, orr, h, w, g):
    tm = ROW_TILE
    row = lambda width: pl.BlockSpec((tm, width), lambda i: (i, 0))
    return pl.pallas_call(
        _out_proj_kernel,
        grid=(N_TOK // tm,),
        in_specs=[row(ATT_WIDTH), row(RET_WIDTH), row(D_MODEL),
                  _resident(w.shape), _resident(g.shape)],
        out_specs=(row(D_MODEL), row(D_MODEL)),
        out_shape=(jax.ShapeDtypeStruct((N_TOK, D_MODEL), F32),
                   jax.ShapeDtypeStruct((N_TOK, D_MODEL), BF16)),
        compiler_params=_params("parallel"),
        name="out_proj",
    )(oa, orr, h, w, g)


def _mlp_kernel(xn_ref, wu_ref, wd_ref, o_ref):
    f = pl.program_id(1)
    up = jnp.dot(xn_ref[...], wu_ref[...], preferred_element_type=F32)
    act = jnp.square(jnp.maximum(up, 0.0)).astype(BF16)
    down = jnp.dot(act, wd_ref[...], preferred_element_type=F32)

    @pl.when(f == 0)
    def _():
        o_ref[...] = down

    @pl.when(f > 0)
    def _():
        o_ref[...] += down


def _mlp(xn, w_up, w_down):
    tm, tf = MLP_ROW_TILE, MLP_FF_TILE
    return pl.pallas_call(
        _mlp_kernel,
        grid=(N_TOK // tm, D_FF // tf),
        in_specs=[pl.BlockSpec((tm, D_MODEL), lambda i, f: (i, 0)),
                  pl.BlockSpec((D_MODEL, tf), lambda i, f: (0, f)),
                  pl.BlockSpec((tf, D_MODEL), lambda i, f: (f, 0))],
        out_specs=pl.BlockSpec((tm, D_MODEL), lambda i, f: (i, 0)),
        out_shape=jax.ShapeDtypeStruct((N_TOK, D_MODEL), F32),
        compiler_params=_params("parallel", "arbitrary"),
        name="mlp",
    )(xn, w_up, w_down)


def _gate_kernel(h1_ref, m_ref, p_ref, wg_ref, wp_ref, g_ref, gf_ref, o_ref, *, final):
    h2 = h1_ref[...] + m_ref[...]
    xn = _rms(h2, g_ref[...]).astype(BF16)
    gate = jax.nn.sigmoid(jnp.dot(xn, wg_ref[...], preferred_element_type=F32))
    emb = jnp.dot(p_ref[...].astype(BF16), wp_ref[...], preferred_element_type=F32)
    h3 = h2 + gate * emb
    o_ref[...] = _rms(h3, gf_ref[...]) if final else h3


def _gate(h1, mlp_out, p, w_gate, w_ple, g, g_final, final):
    tm = ROW_TILE
    row = lambda width: pl.BlockSpec((tm, width), lambda i: (i, 0))
    return pl.pallas_call(
        functools.partial(_gate_kernel, final=final),
        grid=(N_TOK // tm,),
        in_specs=[row(D_MODEL), row(D_MODEL), row(PLE_DIM),
                  _resident(w_gate.shape), _resident(w_ple.shape),
                  _resident(g.shape), _resident(g_final.shape)],
        out_specs=row(D_MODEL),
        out_shape=jax.ShapeDtypeStruct((N_TOK, D_MODEL), F32),
        compiler_params=_params("parallel"),
        name="gate",
    )(h1, mlp_out, p, w_gate, w_ple, g, g_final)


def _t5_bucket(rel):
    nb = N_BUCKETS // 2
    max_exact = nb // 2
    ret = jnp.where(rel > 0, nb, 0)
    n = jnp.abs(rel)
    nf = jnp.maximum(n, max_exact).astype(F32)
    large = max_exact + (jnp.log(nf / max_exact) / math.log(MAX_DISTANCE / max_exact)
                         * (nb - max_exact)).astype(jnp.int32)
    large = jnp.minimum(large, nb - 1)
    return ret + jnp.where(n < max_exact, n, large)


def _rotary_tables():
    half = RET_DIM // 2
    pos = jnp.concatenate([jnp.arange(SEQ, dtype=jnp.int32),
                           jnp.tile(PAST_LEN + jnp.arange(DEC_SEQ, dtype=jnp.int32), DEC_BATCH)])
    inv = ROPE_BASE ** (-jnp.arange(half, dtype=F32) / half)
    ang = pos.astype(F32)[:, None] * inv[None, :]
    cos, sin = jnp.cos(ang), jnp.sin(ang)
    return jnp.concatenate([cos, cos], axis=-1), jnp.concatenate([-sin, sin], axis=-1)


def kernel(x_prompt, x_sample, cache_k, cache_v, state_ret, p_prompt, p_sample, rel_bias,
           norm_mix, w_in, attn_sink, ret_gain, w_out, norm_mlp, w_up, w_down, norm_ple,
           w_gate, w_ple, norm_final):
    q_idx = jnp.arange(CHUNK, dtype=jnp.int32)
    s_idx = jnp.arange(BAND, dtype=jnp.int32)
    bucket_p = _t5_bucket(s_idx[None, :] - WINDOW - q_idx[:, None])
    bucket_s = _t5_bucket(jnp.arange(WINDOW + DEC_SEQ, dtype=jnp.int32)[None, :] - WINDOW
                          - jnp.arange(DEC_SEQ, dtype=jnp.int32)[:, None])
    bias_p, bias_s = _rel_bias(rel_bias, bucket_p, bucket_s)
    bias_p = bias_p.reshape(N_KV, GROUP * CHUNK, BAND)
    bias_s = bias_s.reshape(N_KV, GROUP * DEC_SEQ, WINDOW + DEC_SEQ)
    bias_sc, bias_sn = bias_s[:, :, :WINDOW], bias_s[:, :, WINDOW:]

    cos2, sin2 = _rotary_tables()
    consts_p = _retention_consts(RET_TILE)
    consts_s = _retention_consts(DEC_SEQ)

    h = jnp.concatenate([x_prompt.reshape(SEQ, D_MODEL), x_sample.reshape(N_SAMPLE, D_MODEL)])
    k_p, v_p, s_p, k_s, v_s, s_s = [], [], [], [], [], []
    for l in range(DEPTH):
        sink = attn_sink[l].astype(F32).reshape(N_KV, GROUP, 1, 1)
        sink_p = jnp.broadcast_to(sink, (N_KV, GROUP, CHUNK, 1)).reshape(N_KV, GROUP * CHUNK, 1)
        sink_s = jnp.broadcast_to(sink, (N_KV, GROUP, DEC_SEQ, 1)).reshape(N_KV, GROUP * DEC_SEQ, 1)
        gain = ret_gain[l].reshape(1, RET_WIDTH)
        p_l = jnp.concatenate([p_prompt[l].reshape(SEQ, PLE_DIM),
                               p_sample[l].reshape(N_SAMPLE, PLE_DIM)])

        qa, k, v, qr, kr, vr, gr = _in_proj(h, norm_mix[l].reshape(1, D_MODEL),
                                            w_in[l].astype(BF16), cos2, sin2)
        oa_p = _attn_prompt(qa, k, v, bias_p, sink_p)
        oa_s = _attn_sample(qa, k, v, cache_k[l].reshape(DEC_BATCH, WINDOW, KV_WIDTH),
                            cache_v[l].reshape(DEC_BATCH, WINDOW, KV_WIDTH),
                            bias_sc, bias_sn, sink_s)
        or_p, st_p = _ret_prompt(qr, kr, vr, gr, consts_p, gain)
        or_s, st_s = _ret_sample(qr, kr, vr, gr, state_ret[l], consts_s, gain)

        h1, xn = _out_proj(jnp.concatenate([oa_p, oa_s]), jnp.concatenate([or_p, or_s]), h,
                           w_out[l].astype(BF16), norm_mlp[l].reshape(1, D_MODEL))
        mlp_out = _mlp(xn, w_up[l].astype(BF16), w_down[l].astype(BF16))
        h = _gate(h1, mlp_out, p_l, w_gate[l].astype(BF16), w_ple[l].astype(BF16),
                  norm_ple[l].reshape(1, D_MODEL), norm_final.reshape(1, D_MODEL),
                  final=(l == DEPTH - 1))

        k_p.append(k[SEQ - WINDOW:SEQ].reshape(1, WINDOW, N_KV, HEAD_DIM))
        v_p.append(v[SEQ - WINDOW:SEQ].reshape(1, WINDOW, N_KV, HEAD_DIM))
        s_p.append(st_p[None])
        k_s.append(k[SEQ:].reshape(DEC_BATCH, DEC_SEQ, N_KV, HEAD_DIM))
        v_s.append(v[SEQ:].reshape(DEC_BATCH, DEC_SEQ, N_KV, HEAD_DIM))
        s_s.append(st_s)

    return (h[:SEQ].reshape(1, SEQ, D_MODEL), h[SEQ:].reshape(DEC_BATCH, DEC_SEQ, D_MODEL),
            jnp.stack(k_p), jnp.stack(v_p), jnp.stack(s_p),
            jnp.stack(k_s), jnp.stack(v_s), jnp.stack(s_s))
```

```python
import functools
import math

import jax
import jax.numpy as jnp
from jax import lax
from jax.experimental import pallas as pl
from jax.experimental.pallas import tpu as pltpu

F32 = jnp.float32
BF16 = jnp.bfloat16

D_MODEL = 2048
SEQ = 8192
DEPTH = 2
DEC_BATCH = 32
DEC_SEQ = 16
PAST_LEN = 4096
N_SAMPLE = DEC_BATCH * DEC_SEQ
N_TOK = SEQ + N_SAMPLE

CHUNK = 64
WINDOW = 128
BAND = WINDOW + CHUNK
HEAD_DIM = 64
N_HEADS = 16
N_KV = 4
GROUP = N_HEADS // N_KV
ATT_WIDTH = N_HEADS * HEAD_DIM
KV_WIDTH = N_KV * HEAD_DIM
RET_DIM = 128
RET_HEADS = 8
RET_WIDTH = RET_HEADS * RET_DIM
IN_WIDTH = ATT_WIDTH + 2 * KV_WIDTH + 4 * RET_WIDTH
D_FF = 4 * D_MODEL
PLE_DIM = 256
N_BUCKETS = 32
MAX_DISTANCE = 128
ROPE_BASE = 10000.0
EPS = 1e-6
ATT_SCALE = HEAD_DIM ** -0.5
RET_SCALE = RET_DIM ** -0.5

ROW_TILE = 512
MLP_ROW_TILE = 1088
MLP_FF_TILE = 512
ATT_TILE = 512
Q_BLOCK = 2 * CHUNK
K_BLOCK = WINDOW + Q_BLOCK
RET_TILE = 256
SAMPLE_BLOCK = 8
VMEM_LIMIT = 58 * 1024 * 1024

NT_DIMS = (((1,), (1,)), ((), ()))
TN_DIMS = (((0,), (0,)), ((), ()))


def _params(*sem):
    return pltpu.CompilerParams(dimension_semantics=sem, vmem_limit_bytes=VMEM_LIMIT)


def _resident(shape):
    nd = len(shape)
    return pl.BlockSpec(shape, lambda *_: (0,) * nd, pipeline_mode=pl.Buffered(1))


def _rms(x, g):
    ms = jnp.mean(x * x, axis=-1, keepdims=True)
    return x * lax.rsqrt(ms + EPS) * g


def _bias_kernel(table_ref, bp_ref, bs_ref, op_ref, os_ref):
    bp = bp_ref[...]
    bs = bs_ref[...]

    def per_head(h, c):
        ap = jnp.zeros(bp.shape, F32)
        as_ = jnp.zeros(bs.shape, F32)
        for b in range(N_BUCKETS):
            t = table_ref[b, h]
            ap = jnp.where(bp == b, t, ap)
            as_ = jnp.where(bs == b, t, as_)
        op_ref[h] = ap
        os_ref[h] = as_
        return c

    lax.fori_loop(0, N_HEADS, per_head, 0)


def _rel_bias(table, bucket_p, bucket_s):
    return pl.pallas_call(
        _bias_kernel,
        out_shape=(jax.ShapeDtypeStruct((N_HEADS,) + bucket_p.shape, F32),
                   jax.ShapeDtypeStruct((N_HEADS,) + bucket_s.shape, F32)),
        in_specs=[pl.BlockSpec(memory_space=pltpu.SMEM),
                  pl.BlockSpec(memory_space=pltpu.VMEM),
                  pl.BlockSpec(memory_space=pltpu.VMEM)],
        out_specs=(pl.BlockSpec(memory_space=pltpu.VMEM), pl.BlockSpec(memory_space=pltpu.VMEM)),
        name="rel_bias",
    )(table, bucket_p, bucket_s)


def _in_proj_kernel(h_ref, g_ref, w_ref, cos_ref, sin_ref,
                    qa_ref, k_ref, v_ref, qr_ref, kr_ref, vr_ref, gr_ref):
    xn = _rms(h_ref[...], g_ref[...]).astype(BF16)

    def proj(lo, width):
        return jnp.dot(xn, w_ref[:, lo:lo + width], preferred_element_type=F32)

    qa_ref[...] = (proj(0, ATT_WIDTH) * ATT_SCALE).astype(BF16)
    k_ref[...] = proj(ATT_WIDTH, KV_WIDTH)
    v_ref[...] = proj(ATT_WIDTH + KV_WIDTH, KV_WIDTH)
    cos = cos_ref[...]
    sin = sin_ref[...]
    base = ATT_WIDTH + 2 * KV_WIDTH

    def rot(x):
        return x * cos + pltpu.roll(x, RET_DIM // 2, 1) * sin

    q = proj(base, RET_WIDTH)
    for h in range(RET_HEADS):
        sl = slice(h * RET_DIM, (h + 1) * RET_DIM)
        qr_ref[:, sl] = rot(q[:, sl]).astype(BF16)
    k = proj(base + RET_WIDTH, RET_WIDTH)
    for h in range(RET_HEADS):
        sl = slice(h * RET_DIM, (h + 1) * RET_DIM)
        kr_ref[:, sl] = (rot(k[:, sl]) * RET_SCALE).astype(BF16)
    vr_ref[...] = proj(base + 2 * RET_WIDTH, RET_WIDTH).astype(BF16)
    gr_ref[...] = proj(base + 3 * RET_WIDTH, RET_WIDTH).astype(BF16)


def _in_proj(h, g, w, cos2, sin2):
    tm = ROW_TILE
    row = lambda width: pl.BlockSpec((tm, width), lambda i: (i, 0))
    out = lambda width, dt: jax.ShapeDtypeStruct((N_TOK, width), dt)
    return pl.pallas_call(
        _in_proj_kernel,
        grid=(N_TOK // tm,),
        in_specs=[row(D_MODEL), _resident((1, D_MODEL)), _resident((D_MODEL, IN_WIDTH)),
                  row(RET_DIM), row(RET_DIM)],
        out_specs=(row(ATT_WIDTH), row(KV_WIDTH), row(KV_WIDTH),
                   row(RET_WIDTH), row(RET_WIDTH), row(RET_WIDTH), row(RET_WIDTH)),
        out_shape=(out(ATT_WIDTH, BF16), out(KV_WIDTH, F32), out(KV_WIDTH, F32),
                   out(RET_WIDTH, BF16), out(RET_WIDTH, BF16), out(RET_WIDTH, BF16),
                   out(RET_WIDTH, BF16)),
        compiler_params=_params("parallel"),
        name="in_proj",
    )(h, g, w, cos2, sin2)


def _softmax_sink(pieces, sink):
    m = sink
    for s in pieces:
        m = jnp.maximum(m, jnp.max(s, axis=-1, keepdims=True))
    ps = [jnp.exp(s - m) for s in pieces]
    den = jnp.exp(sink - m)
    for p in ps:
        den = den + jnp.sum(p, axis=-1, keepdims=True)
    inv = 1.0 / den
    return [(p * inv).astype(BF16) for p in ps]


def _fill_padded(dst, rows, x):
    lo = lax.broadcasted_iota(jnp.int32, (x.shape[0], 2 * HEAD_DIM), 1) < HEAD_DIM
    for pair in range(N_KV // 2):
        xp = x[:, pair * 2 * HEAD_DIM:(pair + 1) * 2 * HEAD_DIM]
        xr = pltpu.roll(xp, HEAD_DIM, 1)
        zero = jnp.zeros_like(xp)
        dst[2 * pair, 0, rows, :] = jnp.where(lo, xp, zero).astype(BF16)
        dst[2 * pair, 1, rows, :] = jnp.where(lo, zero, xr).astype(BF16)
        dst[2 * pair + 1, 0, rows, :] = jnp.where(lo, xr, zero).astype(BF16)
        dst[2 * pair + 1, 1, rows, :] = jnp.where(lo, zero, xp).astype(BF16)


def _attn_prompt_kernel(q_ref, kc_ref, kp_ref, vc_ref, vp_ref, bias_ref, sink_ref, o_ref, kz, vz):
    i = pl.program_id(0)
    _fill_padded(kz, slice(0, WINDOW), kp_ref[...])
    _fill_padded(kz, slice(WINDOW, WINDOW + ATT_TILE), kc_ref[...])
    _fill_padded(vz, slice(0, WINDOW), vp_ref[...])
    _fill_padded(vz, slice(WINDOW, WINDOW + ATT_TILE), vc_ref[...])
    lo = lax.broadcasted_iota(jnp.int32, (2 * Q_BLOCK, 2 * HEAD_DIM), 1) < HEAD_DIM

    def block(j, c):
        r0 = pl.multiple_of(j * Q_BLOCK, Q_BLOCK)
        rows = pl.ds(r0, Q_BLOCK)
        band = pl.ds(r0, K_BLOCK)
        first = jnp.where(jnp.logical_and(i == 0, j == 0), 1, 0)
        for kv in range(N_KV):
            c0 = kv * GROUP * HEAD_DIM
            qst = jnp.concatenate([q_ref[rows, c0:c0 + 2 * HEAD_DIM],
                                   q_ref[rows, c0 + 2 * HEAD_DIM:c0 + 4 * HEAD_DIM]], axis=0)
            acc = None
            inv = []
            for side in range(2):
                s = lax.dot_general(qst, kz[kv, side, band, :], NT_DIMS,
                                    preferred_element_type=F32) + bias_ref[first, kv, side]
                sink = sink_ref[kv, side]
                m = jnp.maximum(jnp.max(s, axis=-1, keepdims=True), sink)
                p = jnp.exp(s - m)
                den = jnp.sum(p, axis=-1, keepdims=True) + jnp.exp(sink - m)
                inv.append(1.0 / den)
                d = jnp.dot(p.astype(BF16), vz[kv, side, band, :], preferred_element_type=F32)
                acc = d if acc is None else acc + d
            o = acc * jnp.where(lo, inv[0], inv[1])
            o_ref[rows, c0:c0 + 2 * HEAD_DIM] = o[0:Q_BLOCK].astype(BF16)
            o_ref[rows, c0 + 2 * HEAD_DIM:c0 + 4 * HEAD_DIM] = o[Q_BLOCK:].astype(BF16)
        return c

    lax.fori_loop(0, ATT_TILE // Q_BLOCK, block, 0)


def _attn_prompt(qa, k, v, bias, sink):
    t = ATT_TILE
    per = t // WINDOW
    cur = lambda width: pl.BlockSpec((t, width), lambda i: (i, 0))
    prev = pl.BlockSpec((WINDOW, KV_WIDTH), lambda i: (jnp.maximum(i * per - 1, 0), 0))
    padded = pltpu.VMEM((N_KV, 2, t + WINDOW, 2 * HEAD_DIM), BF16)
    return pl.pallas_call(
        _attn_prompt_kernel,
        grid=(SEQ // t,),
        in_specs=[cur(ATT_WIDTH), cur(KV_WIDTH), prev, cur(KV_WIDTH), prev,
                  _resident(bias.shape), _resident(sink.shape)],
        out_specs=cur(ATT_WIDTH),
        out_shape=jax.ShapeDtypeStruct((SEQ, ATT_WIDTH), BF16),
        scratch_shapes=[padded, padded],
        compiler_params=_params("parallel"),
        name="attn_prompt",
    )(qa, k, k, v, v, bias, sink)


def _pair_layout(x):
    lead = x.shape[:-3]
    rows, cols = x.shape[-2:]
    x = x.reshape(lead + (N_KV, 2, 2, rows, cols))
    x = jnp.swapaxes(x, -4, -3)
    return x.reshape(lead + (N_KV, 2, 2 * rows, cols))


def _attn_sample_kernel(q_ref, kn_ref, vn_ref, ck_ref, cv_ref, bc_ref, bn_ref, sink_ref, o_ref):
    def stream(b, c):
        r0 = pl.multiple_of(b * DEC_SEQ, DEC_SEQ)
        qc = q_ref[pl.ds(r0, DEC_SEQ), :]
        kn = kn_ref[pl.ds(r0, DEC_SEQ), :].astype(BF16)
        vn = vn_ref[pl.ds(r0, DEC_SEQ), :].astype(BF16)
        ck = ck_ref[b].astype(BF16)
        cv = cv_ref[b].astype(BF16)
        for kv in range(N_KV):
            qs = jnp.concatenate(
                [qc[:, (kv * GROUP + g) * HEAD_DIM:(kv * GROUP + g + 1) * HEAD_DIM]
                 for g in range(GROUP)], axis=0)
            ksl = slice(kv * HEAD_DIM, (kv + 1) * HEAD_DIM)
            s1 = lax.dot_general(qs, ck[:, ksl], NT_DIMS, preferred_element_type=F32) + bc_ref[kv]
            s2 = lax.dot_general(qs, kn[:, ksl], NT_DIMS, preferred_element_type=F32) + bn_ref[kv]
            p1, p2 = _softmax_sink([s1, s2], sink_ref[kv])
            o = (jnp.dot(p1, cv[:, ksl], preferred_element_type=F32)
                 + jnp.dot(p2, vn[:, ksl], preferred_element_type=F32))
            ok = jnp.concatenate([o[g * DEC_SEQ:(g + 1) * DEC_SEQ] for g in range(GROUP)], axis=1)
            o_ref[pl.ds(r0, DEC_SEQ), kv * GROUP * HEAD_DIM:(kv + 1) * GROUP * HEAD_DIM] = (
                ok.astype(BF16))
        return c

    lax.fori_loop(0, SAMPLE_BLOCK, stream, 0)


def _attn_sample(qa, k, v, cache_k, cache_v, bias_c, bias_n, sink):
    rows = SAMPLE_BLOCK * DEC_SEQ
    first = SEQ // rows
    new = lambda width: pl.BlockSpec((rows, width), lambda i: (first + i, 0))
    cache = pl.BlockSpec((SAMPLE_BLOCK, WINDOW, KV_WIDTH), lambda i: (i, 0, 0))
    return pl.pallas_call(
        _attn_sample_kernel,
        grid=(DEC_BATCH // SAMPLE_BLOCK,),
        in_specs=[new(ATT_WIDTH), new(KV_WIDTH), new(KV_WIDTH), cache, cache,
                  _resident(bias_c.shape), _resident(bias_n.shape), _resident(sink.shape)],
        out_specs=pl.BlockSpec((rows, ATT_WIDTH), lambda i: (i, 0)),
        out_shape=jax.ShapeDtypeStruct((N_SAMPLE, ATT_WIDTH), BF16),
        compiler_params=_params("parallel"),
        name="attn_sample",
    )(qa, k, v, cache_k, cache_v, bias_c, bias_n, sink)


def _retention_head(q, k, v, g, state, decay, zeta, xi, g_chunk, gain):
    scores = lax.dot_general(q, k, NT_DIMS, preferred_element_type=F32) * decay
    o = jnp.dot(scores.astype(BF16), v, preferred_element_type=F32)
    o = o + jnp.dot(q, state.astype(BF16), preferred_element_type=F32) * xi
    kz = (k.astype(F32) * zeta).astype(BF16)
    upd = lax.dot_general(kz, v, TN_DIMS, preferred_element_type=F32)
    new_state = g_chunk * state + upd
    o = o * lax.rsqrt(jnp.mean(o * o, axis=-1, keepdims=True) + EPS) * gain
    gf = g.astype(F32)
    return gf * jax.nn.sigmoid(gf) * o, new_state


def _ret_prompt_kernel(gc_ref, q_ref, k_ref, v_ref, g_ref, decay_ref, zeta_ref, xi_ref, gain_ref,
                       o_ref, st_ref, state):
    i = pl.program_id(0)

    @pl.when(i == 0)
    def _():
        state[...] = jnp.zeros(state.shape, F32)

    for h in range(RET_HEADS):
        sl = slice(h * RET_DIM, (h + 1) * RET_DIM)
        o, s_new = _retention_head(q_ref[:, sl], k_ref[:, sl], v_ref[:, sl], g_ref[:, sl],
                                   state[h], decay_ref[h], zeta_ref[h], xi_ref[h], gc_ref[h],
                                   gain_ref[:, sl])
        state[h] = s_new
        o_ref[:, sl] = o.astype(BF16)

    @pl.when(i == pl.num_programs(0) - 1)
    def _():
        st_ref[...] = state[...]


def _ret_prompt(qr, kr, vr, gr, consts, gain):
    g_chunk, decay, zeta, xi = consts
    c = RET_TILE
    row = pl.BlockSpec((c, RET_WIDTH), lambda i: (i, 0))
    st_shape = (RET_HEADS, RET_DIM, RET_DIM)
    return pl.pallas_call(
        _ret_prompt_kernel,
        grid=(SEQ // c,),
        in_specs=[pl.BlockSpec(memory_space=pltpu.SMEM), row, row, row, row,
                  _resident(decay.shape), _resident(zeta.shape), _resident(xi.shape),
                  _resident(gain.shape)],
        out_specs=(row, pl.BlockSpec(st_shape, lambda i: (0, 0, 0))),
        out_shape=(jax.ShapeDtypeStruct((SEQ, RET_WIDTH), BF16),
                   jax.ShapeDtypeStruct(st_shape, F32)),
        scratch_shapes=[pltpu.VMEM(st_shape, F32)],
        compiler_params=_params("arbitrary"),
        name="ret_prompt",
    )(g_chunk, qr, kr, vr, gr, decay, zeta, xi, gain)


def _ret_sample_kernel(gc_ref, q_ref, k_ref, v_ref, g_ref, s_ref, decay_ref, zeta_ref, xi_ref,
                       gain_ref, o_ref, st_ref):
    for h in range(RET_HEADS):
        sl = slice(h * RET_DIM, (h + 1) * RET_DIM)
        o, s_new = _retention_head(q_ref[:, sl], k_ref[:, sl], v_ref[:, sl], g_ref[:, sl],
                                   s_ref[0, h], decay_ref[h], zeta_ref[h], xi_ref[h], gc_ref[h],
                                   gain_ref[:, sl])
        st_ref[0, h] = s_new
        o_ref[:, sl] = o.astype(BF16)


def _ret_sample(qr, kr, vr, gr, state, consts, gain):
    g_chunk, decay, zeta, xi = consts
    first = SEQ // DEC_SEQ
    row = pl.BlockSpec((DEC_SEQ, RET_WIDTH), lambda b: (first + b, 0))
    st = pl.BlockSpec((1, RET_HEADS, RET_DIM, RET_DIM), lambda b: (b, 0, 0, 0))
    return pl.pallas_call(
        _ret_sample_kernel,
        grid=(DEC_BATCH,),
        in_specs=[pl.BlockSpec(memory_space=pltpu.SMEM), row, row, row, row, st,
                  _resident(decay.shape), _resident(zeta.shape), _resident(xi.shape),
                  _resident(gain.shape)],
        out_specs=(pl.BlockSpec((DEC_SEQ, RET_WIDTH), lambda b: (b, 0)), st),
        out_shape=(jax.ShapeDtypeStruct((N_SAMPLE, RET_WIDTH), BF16),
                   jax.ShapeDtypeStruct(state.shape, F32)),
        compiler_params=_params("parallel"),
        name="ret_sample",
    )(g_chunk, qr, kr, vr, gr, state, decay, zeta, xi, gain)


def _retention_consts(chunk):
    lg = jnp.log(1.0 - 2.0 ** (-5.0 - jnp.arange(RET_HEADS, dtype=F32)))
    idx = jnp.arange(chunk, dtype=F32)
    diff = idx[:, None] - idx[None, :]
    decay = jnp.where(diff >= 0, jnp.exp(lg[:, None, None] * jnp.maximum(diff, 0.0)), 0.0)
    zeta = jnp.exp(lg[:, None] * (chunk - 1 - idx))[:, :, None]
    xi = jnp.exp(lg[:, None] * (idx + 1))[:, :, None]
    return jnp.exp(lg * chunk), decay, zeta, xi


def _out_proj_kernel(oa_ref, or_ref, h_ref, w_ref, g_ref, h1_ref, xn_ref):
    h1 = (h_ref[...]
          + jnp.dot(oa_ref[...], w_ref[0:ATT_WIDTH, :], preferred_element_type=F32)
          + jnp.dot(or_ref[...], w_ref[ATT_WIDTH:, :], preferred_element_type=F32))
    h1_ref[...] = h1
    xn_ref[...] = _rms(h1, g_ref[...]).astype(BF16)


def _out_proj(oa, orr, h, w, g):
    tm = ROW_TILE
    row = lambda width: pl.BlockSpec((tm, width), lambda i: (i, 0))
    return pl.pallas_call(
        _out_proj_kernel,
        grid=(N_TOK // tm,),
        in_specs=[row(ATT_WIDTH), row(RET_WIDTH), row(D_MODEL),
                  _resident(w.shape), _resident(g.shape)],
        out_specs=(row(D_MODEL), row(D_MODEL)),
        out_shape=(jax.ShapeDtypeStruct((N_TOK, D_MODEL), F32),
                   jax.ShapeDtypeStruct((N_TOK, D_MODEL), BF16)),
        compiler_params=_params("parallel"),
        name="out_proj",
    )(oa, orr, h, w, g)


def _mlp_kernel(xn_ref, wu_ref, wd_ref, o_ref):
    f = pl.program_id(1)
    up = jnp.dot(xn_ref[...], wu_ref[...].astype(BF16), preferred_element_type=F32)
    act = jnp.square(jnp.maximum(up, 0.0)).astype(BF16)
    down = jnp.dot(act, wd_ref[...].astype(BF16), preferred_element_type=F32)

    @pl.when(f == 0)
    def _():
        o_ref[...] = down

    @pl.when(f > 0)
    def _():
        o_ref[...] += down


def _mlp(xn, w_up, w_down, layer):
    tm, tf = MLP_ROW_TILE, MLP_FF_TILE
    return pl.pallas_call(
        _mlp_kernel,
        grid=(N_TOK // tm, D_FF // tf),
        in_specs=[pl.BlockSpec((tm, D_MODEL), lambda i, f: (i, 0)),
                  pl.BlockSpec((None, D_MODEL, tf), lambda i, f: (layer, 0, f)),
                  pl.BlockSpec((None, tf, D_MODEL), lambda i, f: (layer, f, 0))],
        out_specs=pl.BlockSpec((tm, D_MODEL), lambda i, f: (i, 0)),
        out_shape=jax.ShapeDtypeStruct((N_TOK, D_MODEL), F32),
        compiler_params=_params("parallel", "arbitrary"),
        name="mlp",
    )(xn, w_up, w_down)


def _gate_kernel(h1_ref, m_ref, p_ref, wg_ref, wp_ref, g_ref, gf_ref, o_ref, *, final):
    h2 = h1_ref[...] + m_ref[...]
    xn = _rms(h2, g_ref[...]).astype(BF16)
    gate = jax.nn.sigmoid(jnp.dot(xn, wg_ref[...], preferred_element_type=F32))
    emb = jnp.dot(p_ref[...].astype(BF16), wp_ref[...], preferred_element_type=F32)
    h3 = h2 + gate * emb
    o_ref[...] = _rms(h3, gf_ref[...]) if final else h3


def _gate(h1, mlp_out, p, w_gate, w_ple, g, g_final, final):
    tm = ROW_TILE
    row = lambda width: pl.BlockSpec((tm, width), lambda i: (i, 0))
    return pl.pallas_call(
        functools.partial(_gate_kernel, final=final),
        grid=(N_TOK // tm,),
        in_specs=[row(D_MODEL), row(D_MODEL), row(PLE_DIM),
                  _resident(w_gate.shape), _resident(w_ple.shape),
                  _resident(g.shape), _resident(g_final.shape)],
        out_specs=row(D_MODEL),
        out_shape=jax.ShapeDtypeStruct((N_TOK, D_MODEL), F32),
        compiler_params=_params("parallel"),
        name="gate",
    )(h1, mlp_out, p, w_gate, w_ple, g, g_final)


def _t5_bucket(rel):
    nb = N_BUCKETS // 2
    max_exact = nb // 2
    ret = jnp.where(rel > 0, nb, 0)
    n = jnp.abs(rel)
    nf = jnp.maximum(n, max_exact).astype(F32)
    large = max_exact + (jnp.log(nf / max_exact) / math.log(MAX_DISTANCE / max_exact)
                         * (nb - max_exact)).astype(jnp.int32)
    large = jnp.minimum(large, nb - 1)
    return ret + jnp.where(n < max_exact, n, large)


def _rotary_tables():
    half = RET_DIM // 2
    pos = jnp.concatenate([jnp.arange(SEQ, dtype=jnp.int32),
                           jnp.tile(PAST_LEN + jnp.arange(DEC_SEQ, dtype=jnp.int32), DEC_BATCH)])
    inv = ROPE_BASE ** (-jnp.arange(half, dtype=F32) / half)
    ang = pos.astype(F32)[:, None] * inv[None, :]
    cos, sin = jnp.cos(ang), jnp.sin(ang)
    return jnp.concatenate([cos, cos], axis=-1), jnp.concatenate([-sin, sin], axis=-1)


def kernel(x_prompt, x_sample, cache_k, cache_v, state_ret, p_prompt, p_sample, rel_bias,
           norm_mix, w_in, attn_sink, ret_gain, w_out, norm_mlp, w_up, w_down, norm_ple,
           w_gate, w_ple, norm_final):
    q_idx = jnp.arange(Q_BLOCK, dtype=jnp.int32)[:, None]
    s_idx = jnp.arange(K_BLOCK, dtype=jnp.int32)[None, :]
    bucket_p = _t5_bucket(s_idx - WINDOW - q_idx)
    bucket_s = _t5_bucket(jnp.arange(WINDOW + DEC_SEQ, dtype=jnp.int32)[None, :] - WINDOW
                          - jnp.arange(DEC_SEQ, dtype=jnp.int32)[:, None])
    bias_p, bias_s = _rel_bias(rel_bias, bucket_p, bucket_s)
    back = q_idx // CHUNK + WINDOW // CHUNK - s_idx // CHUNK
    seen = jnp.logical_and(back >= 0, back <= WINDOW // CHUNK)
    seen_first = jnp.logical_and(seen, s_idx >= WINDOW)
    bias_p = _pair_layout(jnp.stack([jnp.where(seen, bias_p, -jnp.inf),
                                     jnp.where(seen_first, bias_p, -jnp.inf)]))
    bias_s = bias_s.reshape(N_KV, GROUP * DEC_SEQ, WINDOW + DEC_SEQ)
    bias_sc, bias_sn = bias_s[:, :, :WINDOW], bias_s[:, :, WINDOW:]

    cos2, sin2 = _rotary_tables()
    consts_p = _retention_consts(RET_TILE)
    consts_s = _retention_consts(DEC_SEQ)

    h = jnp.concatenate([x_prompt.reshape(SEQ, D_MODEL), x_sample.reshape(N_SAMPLE, D_MODEL)])
    k_p, v_p, s_p, k_s, v_s, s_s = [], [], [], [], [], []
    for l in range(DEPTH):
        sink = attn_sink[l].astype(F32).reshape(N_KV, GROUP, 1, 1)
        sink_p = _pair_layout(jnp.broadcast_to(sink.reshape(N_HEADS, 1, 1), (N_HEADS, Q_BLOCK, 1)))
        sink_s = jnp.broadcast_to(sink, (N_KV, GROUP, DEC_SEQ, 1)).reshape(N_KV, GROUP * DEC_SEQ, 1)
        gain = ret_gain[l].reshape(1, RET_WIDTH)
        p_l = jnp.concatenate([p_prompt[l].reshape(SEQ, PLE_DIM),
                               p_sample[l].reshape(N_SAMPLE, PLE_DIM)])

        qa, k, v, qr, kr, vr, gr = _in_proj(h, norm_mix[l].reshape(1, D_MODEL),
                                            w_in[l].astype(BF16), cos2, sin2)
        oa_p = _attn_prompt(qa, k, v, bias_p, sink_p)
        oa_s = _attn_sample(qa, k, v, cache_k[l].reshape(DEC_BATCH, WINDOW, KV_WIDTH),
                            cache_v[l].reshape(DEC_BATCH, WINDOW, KV_WIDTH),
                            bias_sc, bias_sn, sink_s)
        or_p, st_p = _ret_prompt(qr, kr, vr, gr, consts_p, gain)
        or_s, st_s = _ret_sample(qr, kr, vr, gr, state_ret[l], consts_s, gain)

        h1, xn = _out_proj(jnp.concatenate([oa_p, oa_s]), jnp.concatenate([or_p, or_s]), h,
                           w_out[l].astype(BF16), norm_mlp[l].reshape(1, D_MODEL))
        mlp_out = _mlp(xn, w_up, w_down, l)
        h = _gate(h1, mlp_out, p_l, w_gate[l].astype(BF16), w_ple[l].astype(BF16),
                  norm_ple[l].reshape(1, D_MODEL), norm_final.reshape(1, D_MODEL),
                  final=(l == DEPTH - 1))

        k_p.append(k[SEQ - WINDOW:SEQ].reshape(1, WINDOW, N_KV, HEAD_DIM))
        v_p.append(v[SEQ - WINDOW:SEQ].reshape(1, WINDOW, N_KV, HEAD_DIM))
        s_p.append(st_p[None])
        k_s.append(k[SEQ:].reshape(DEC_BATCH, DEC_SEQ, N_KV, HEAD_DIM))
        v_s.append(v[SEQ:].reshape(DEC_BATCH, DEC_SEQ, N_KV, HEAD_DIM))
        s_s.append(st_s)

    return (h[:SEQ].reshape(1, SEQ, D_MODEL), h[SEQ:].reshape(DEC_BATCH, DEC_SEQ, D_MODEL),
            jnp.stack(k_p), jnp.stack(v_p), jnp.stack(s_p),
            jnp.stack(k_s), jnp.stack(v_s), jnp.stack(s_s))
```

```python
import functools
import math

import jax
import jax.numpy as jnp
from jax import lax
from jax.experimental import pallas as pl
from jax.experimental.pallas import tpu as pltpu

F32 = jnp.float32
BF16 = jnp.bfloat16

D_MODEL = 2048
SEQ = 8192
DEPTH = 2
DEC_BATCH = 32
DEC_SEQ = 16
PAST_LEN = 4096
N_SAMPLE = DEC_BATCH * DEC_SEQ
N_TOK = SEQ + N_SAMPLE

CHUNK = 64
WINDOW = 128
BAND = WINDOW + CHUNK
HEAD_DIM = 64
N_HEADS = 16
N_KV = 4
GROUP = N_HEADS // N_KV
ATT_WIDTH = N_HEADS * HEAD_DIM
KV_WIDTH = N_KV * HEAD_DIM
RET_DIM = 128
RET_HEADS = 8
RET_WIDTH = RET_HEADS * RET_DIM
IN_WIDTH = ATT_WIDTH + 2 * KV_WIDTH + 4 * RET_WIDTH
D_FF = 4 * D_MODEL
PLE_DIM = 256
N_BUCKETS = 32
MAX_DISTANCE = 128
ROPE_BASE = 10000.0
EPS = 1e-6
ATT_SCALE = HEAD_DIM ** -0.5
RET_SCALE = RET_DIM ** -0.5

ROW_TILE = 512
MLP_ROW_TILE = 1088
MLP_FF_TILE = 512
ATT_TILE = 512
Q_BLOCK = 2 * CHUNK
K_BLOCK = WINDOW + Q_BLOCK
RET_TILE = 256
SAMPLE_BLOCK = 8
VMEM_LIMIT = 58 * 1024 * 1024

NT_DIMS = (((1,), (1,)), ((), ()))
TN_DIMS = (((0,), (0,)), ((), ()))


def _params(*sem):
    return pltpu.CompilerParams(dimension_semantics=sem, vmem_limit_bytes=VMEM_LIMIT)


def _resident(shape):
    nd = len(shape)
    return pl.BlockSpec(shape, lambda *_: (0,) * nd, pipeline_mode=pl.Buffered(1))


def _rms(x, g):
    ms = jnp.mean(x * x, axis=-1, keepdims=True)
    return x * lax.rsqrt(ms + EPS) * g


N_PROMPT_TILES = SEQ // ROW_TILE
assert N_SAMPLE == ROW_TILE


def _stacked_rows(width):
    return pl.BlockSpec((ROW_TILE, width), lambda i: (i, 0))


def _prompt_rows(width, *lead):
    none = (None,) * len(lead)
    return pl.BlockSpec(none + (ROW_TILE, width),
                        lambda i: lead + (jnp.minimum(i, N_PROMPT_TILES - 1), 0))


def _sample_rows(width, *lead):
    none = (None,) * len(lead)
    return pl.BlockSpec(none + (ROW_TILE, width), lambda i: lead + (0, 0))


def _pick_rows(prompt_ref, sample_ref):
    return jnp.where(pl.program_id(0) < N_PROMPT_TILES, prompt_ref[...], sample_ref[...])


def _bias_kernel(table_ref, bp_ref, bs_ref, op_ref, os_ref):
    bp = bp_ref[...]
    bs = bs_ref[...]

    def per_head(h, c):
        ap = jnp.zeros(bp.shape, F32)
        as_ = jnp.zeros(bs.shape, F32)
        for b in range(N_BUCKETS):
            t = table_ref[b, h]
            ap = jnp.where(bp == b, t, ap)
            as_ = jnp.where(bs == b, t, as_)
        op_ref[h] = ap
        os_ref[h] = as_
        return c

    lax.fori_loop(0, N_HEADS, per_head, 0)


def _rel_bias(table, bucket_p, bucket_s):
    return pl.pallas_call(
        _bias_kernel,
        out_shape=(jax.ShapeDtypeStruct((N_HEADS,) + bucket_p.shape, F32),
                   jax.ShapeDtypeStruct((N_HEADS,) + bucket_s.shape, F32)),
        in_specs=[pl.BlockSpec(memory_space=pltpu.SMEM),
                  pl.BlockSpec(memory_space=pltpu.VMEM),
                  pl.BlockSpec(memory_space=pltpu.VMEM)],
        out_specs=(pl.BlockSpec(memory_space=pltpu.VMEM), pl.BlockSpec(memory_space=pltpu.VMEM)),
        name="rel_bias",
    )(table, bucket_p, bucket_s)


def _in_proj_kernel(hp_ref, hs_ref, g_ref, w_ref, cos_ref, sin_ref,
                    qa_ref, k_ref, v_ref, qr_ref, kr_ref, vr_ref, gr_ref):
    xn = _rms(_pick_rows(hp_ref, hs_ref), g_ref[...]).astype(BF16)

    def proj(lo, width):
        return jnp.dot(xn, w_ref[:, lo:lo + width], preferred_element_type=F32)

    qa_ref[...] = (proj(0, ATT_WIDTH) * ATT_SCALE).astype(BF16)
    k_ref[...] = proj(ATT_WIDTH, KV_WIDTH)
    v_ref[...] = proj(ATT_WIDTH + KV_WIDTH, KV_WIDTH)
    cos = cos_ref[...]
    sin = sin_ref[...]
    base = ATT_WIDTH + 2 * KV_WIDTH

    def rot(x):
        return x * cos + pltpu.roll(x, RET_DIM // 2, 1) * sin

    q = proj(base, RET_WIDTH)
    for h in range(RET_HEADS):
        sl = slice(h * RET_DIM, (h + 1) * RET_DIM)
        qr_ref[:, sl] = rot(q[:, sl]).astype(BF16)
    k = proj(base + RET_WIDTH, RET_WIDTH)
    for h in range(RET_HEADS):
        sl = slice(h * RET_DIM, (h + 1) * RET_DIM)
        kr_ref[:, sl] = (rot(k[:, sl]) * RET_SCALE).astype(BF16)
    vr_ref[...] = proj(base + 2 * RET_WIDTH, RET_WIDTH).astype(BF16)
    gr_ref[...] = proj(base + 3 * RET_WIDTH, RET_WIDTH).astype(BF16)


def _in_proj(hp, hs, g, w, cos2, sin2):
    tm = ROW_TILE
    row = _stacked_rows
    out = lambda width, dt: jax.ShapeDtypeStruct((N_TOK, width), dt)
    return pl.pallas_call(
        _in_proj_kernel,
        grid=(N_TOK // tm,),
        in_specs=[_prompt_rows(D_MODEL), _sample_rows(D_MODEL),
                  _resident((1, D_MODEL)), _resident((D_MODEL, IN_WIDTH)),
                  row(RET_DIM), row(RET_DIM)],
        out_specs=(row(ATT_WIDTH), row(KV_WIDTH), row(KV_WIDTH),
                   row(RET_WIDTH), row(RET_WIDTH), row(RET_WIDTH), row(RET_WIDTH)),
        out_shape=(out(ATT_WIDTH, BF16), out(KV_WIDTH, F32), out(KV_WIDTH, F32),
                   out(RET_WIDTH, BF16), out(RET_WIDTH, BF16), out(RET_WIDTH, BF16),
                   out(RET_WIDTH, BF16)),
        compiler_params=_params("parallel"),
        name="in_proj",
    )(hp, hs, g, w, cos2, sin2)


def _softmax_sink(pieces, sink):
    m = sink
    for s in pieces:
        m = jnp.maximum(m, jnp.max(s, axis=-1, keepdims=True))
    ps = [jnp.exp(s - m) for s in pieces]
    den = jnp.exp(sink - m)
    for p in ps:
        den = den + jnp.sum(p, axis=-1, keepdims=True)
    inv = 1.0 / den
    return [(p * inv).astype(BF16) for p in ps]


def _fill_padded(dst, rows, x):
    lo = lax.broadcasted_iota(jnp.int32, (x.shape[0], 2 * HEAD_DIM), 1) < HEAD_DIM
    for pair in range(N_KV // 2):
        xp = x[:, pair * 2 * HEAD_DIM:(pair + 1) * 2 * HEAD_DIM]
        xr = pltpu.roll(xp, HEAD_DIM, 1)
        zero = jnp.zeros_like(xp)
        dst[2 * pair, 0, rows, :] = jnp.where(lo, xp, zero).astype(BF16)
        dst[2 * pair, 1, rows, :] = jnp.where(lo, zero, xr).astype(BF16)
        dst[2 * pair + 1, 0, rows, :] = jnp.where(lo, xr, zero).astype(BF16)
        dst[2 * pair + 1, 1, rows, :] = jnp.where(lo, zero, xp).astype(BF16)


def _attn_prompt_kernel(q_ref, kc_ref, kp_ref, vc_ref, vp_ref, bias_ref, sink_ref, o_ref, kz, vz):
    i = pl.program_id(0)
    _fill_padded(kz, slice(0, WINDOW), kp_ref[...])
    _fill_padded(kz, slice(WINDOW, WINDOW + ATT_TILE), kc_ref[...])
    _fill_padded(vz, slice(0, WINDOW), vp_ref[...])
    _fill_padded(vz, slice(WINDOW, WINDOW + ATT_TILE), vc_ref[...])
    lo = lax.broadcasted_iota(jnp.int32, (2 * Q_BLOCK, 2 * HEAD_DIM), 1) < HEAD_DIM

    def block(j, c):
        r0 = pl.multiple_of(j * Q_BLOCK, Q_BLOCK)
        rows = pl.ds(r0, Q_BLOCK)
        band = pl.ds(r0, K_BLOCK)
        first = jnp.where(jnp.logical_and(i == 0, j == 0), 1, 0)
        for kv in range(N_KV):
            c0 = kv * GROUP * HEAD_DIM
            qst = jnp.concatenate([q_ref[rows, c0:c0 + 2 * HEAD_DIM],
                                   q_ref[rows, c0 + 2 * HEAD_DIM:c0 + 4 * HEAD_DIM]], axis=0)
            acc = None
            inv = []
            for side in range(2):
                s = lax.dot_general(qst, kz[kv, side, band, :], NT_DIMS,
                                    preferred_element_type=F32) + bias_ref[first, kv, side]
                sink = sink_ref[kv, side]
                m = jnp.maximum(jnp.max(s, axis=-1, keepdims=True), sink)
                p = jnp.exp(s - m)
                den = jnp.sum(p, axis=-1, keepdims=True) + jnp.exp(sink - m)
                inv.append(1.0 / den)
                d = jnp.dot(p.astype(BF16), vz[kv, side, band, :], preferred_element_type=F32)
                acc = d if acc is None else acc + d
            o = acc * jnp.where(lo, inv[0], inv[1])
            o_ref[rows, c0:c0 + 2 * HEAD_DIM] = o[0:Q_BLOCK].astype(BF16)
            o_ref[rows, c0 + 2 * HEAD_DIM:c0 + 4 * HEAD_DIM] = o[Q_BLOCK:].astype(BF16)
        return c

    lax.fori_loop(0, ATT_TILE // Q_BLOCK, block, 0)


def _attn_prompt(qa, k, v, bias, sink):
    t = ATT_TILE
    per = t // WINDOW
    cur = lambda width: pl.BlockSpec((t, width), lambda i: (i, 0))
    prev = pl.BlockSpec((WINDOW, KV_WIDTH), lambda i: (jnp.maximum(i * per - 1, 0), 0))
    padded = pltpu.VMEM((N_KV, 2, t + WINDOW, 2 * HEAD_DIM), BF16)
    return pl.pallas_call(
        _attn_prompt_kernel,
        grid=(SEQ // t,),
        in_specs=[cur(ATT_WIDTH), cur(KV_WIDTH), prev, cur(KV_WIDTH), prev,
                  _resident(bias.shape), _resident(sink.shape)],
        out_specs=cur(ATT_WIDTH),
        out_shape=jax.ShapeDtypeStruct((SEQ, ATT_WIDTH), BF16),
        scratch_shapes=[padded, padded],
        compiler_params=_params("parallel"),
        name="attn_prompt",
    )(qa, k, k, v, v, bias, sink)


def _pair_layout(x):
    lead = x.shape[:-3]
    rows, cols = x.shape[-2:]
    x = x.reshape(lead + (N_KV, 2, 2, rows, cols))
    x = jnp.swapaxes(x, -4, -3)
    return x.reshape(lead + (N_KV, 2, 2 * rows, cols))


def _attn_sample_kernel(q_ref, kn_ref, vn_ref, ck_ref, cv_ref, bc_ref, bn_ref, sink_ref, o_ref):
    def stream(b, c):
        r0 = pl.multiple_of(b * DEC_SEQ, DEC_SEQ)
        qc = q_ref[pl.ds(r0, DEC_SEQ), :]
        kn = kn_ref[pl.ds(r0, DEC_SEQ), :].astype(BF16)
        vn = vn_ref[pl.ds(r0, DEC_SEQ), :].astype(BF16)
        ck = ck_ref[b].astype(BF16)
        cv = cv_ref[b].astype(BF16)
        for kv in range(N_KV):
            qs = jnp.concatenate(
                [qc[:, (kv * GROUP + g) * HEAD_DIM:(kv * GROUP + g + 1) * HEAD_DIM]
                 for g in range(GROUP)], axis=0)
            ksl = slice(kv * HEAD_DIM, (kv + 1) * HEAD_DIM)
            s1 = lax.dot_general(qs, ck[:, ksl], NT_DIMS, preferred_element_type=F32) + bc_ref[kv]
            s2 = lax.dot_general(qs, kn[:, ksl], NT_DIMS, preferred_element_type=F32) + bn_ref[kv]
            p1, p2 = _softmax_sink([s1, s2], sink_ref[kv])
            o = (jnp.dot(p1, cv[:, ksl], preferred_element_type=F32)
                 + jnp.dot(p2, vn[:, ksl], preferred_element_type=F32))
            ok = jnp.concatenate([o[g * DEC_SEQ:(g + 1) * DEC_SEQ] for g in range(GROUP)], axis=1)
            o_ref[pl.ds(r0, DEC_SEQ), kv * GROUP * HEAD_DIM:(kv + 1) * GROUP * HEAD_DIM] = (
                ok.astype(BF16))
        return c

    lax.fori_loop(0, SAMPLE_BLOCK, stream, 0)


def _attn_sample(qa, k, v, cache_k, cache_v, bias_c, bias_n, sink):
    rows = SAMPLE_BLOCK * DEC_SEQ
    first = SEQ // rows
    new = lambda width: pl.BlockSpec((rows, width), lambda i: (first + i, 0))
    cache = pl.BlockSpec((SAMPLE_BLOCK, WINDOW, KV_WIDTH), lambda i: (i, 0, 0))
    return pl.pallas_call(
        _attn_sample_kernel,
        grid=(DEC_BATCH // SAMPLE_BLOCK,),
        in_specs=[new(ATT_WIDTH), new(KV_WIDTH), new(KV_WIDTH), cache, cache,
                  _resident(bias_c.shape), _resident(bias_n.shape), _resident(sink.shape)],
        out_specs=pl.BlockSpec((rows, ATT_WIDTH), lambda i: (i, 0)),
        out_shape=jax.ShapeDtypeStruct((N_SAMPLE, ATT_WIDTH), BF16),
        compiler_params=_params("parallel"),
        name="attn_sample",
    )(qa, k, v, cache_k, cache_v, bias_c, bias_n, sink)


def _retention_head(q, k, v, g, state, decay, zeta, xi, g_chunk, gain):
    scores = lax.dot_general(q, k, NT_DIMS, preferred_element_type=F32) * decay
    o = jnp.dot(scores.astype(BF16), v, preferred_element_type=F32)
    o = o + jnp.dot(q, state.astype(BF16), preferred_element_type=F32) * xi
    kz = (k.astype(F32) * zeta).astype(BF16)
    upd = lax.dot_general(kz, v, TN_DIMS, preferred_element_type=F32)
    new_state = g_chunk * state + upd
    o = o * lax.rsqrt(jnp.mean(o * o, axis=-1, keepdims=True) + EPS) * gain
    gf = g.astype(F32)
    return gf * jax.nn.sigmoid(gf) * o, new_state


def _ret_prompt_kernel(gc_ref, q_ref, k_ref, v_ref, g_ref, decay_ref, zeta_ref, xi_ref, gain_ref,
                       o_ref, st_ref, state):
    i = pl.program_id(0)

    @pl.when(i == 0)
    def _():
        state[...] = jnp.zeros(state.shape, F32)

    for h in range(RET_HEADS):
        sl = slice(h * RET_DIM, (h + 1) * RET_DIM)
        o, s_new = _retention_head(q_ref[:, sl], k_ref[:, sl], v_ref[:, sl], g_ref[:, sl],
                                   state[h], decay_ref[h], zeta_ref[h], xi_ref[h], gc_ref[h],
                                   gain_ref[:, sl])
        state[h] = s_new
        o_ref[:, sl] = o.astype(BF16)

    @pl.when(i == pl.num_programs(0) - 1)
    def _():
        st_ref[...] = state[...]


def _ret_prompt(qr, kr, vr, gr, consts, gain):
    g_chunk, decay, zeta, xi = consts
    c = RET_TILE
    row = pl.BlockSpec((c, RET_WIDTH), lambda i: (i, 0))
    st_shape = (RET_HEADS, RET_DIM, RET_DIM)
    return pl.pallas_call(
        _ret_prompt_kernel,
        grid=(SEQ // c,),
        in_specs=[pl.BlockSpec(memory_space=pltpu.SMEM), row, row, row, row,
                  _resident(decay.shape), _resident(zeta.shape), _resident(xi.shape),
                  _resident(gain.shape)],
        out_specs=(row, pl.BlockSpec(st_shape, lambda i: (0, 0, 0))),
        out_shape=(jax.ShapeDtypeStruct((SEQ, RET_WIDTH), BF16),
                   jax.ShapeDtypeStruct(st_shape, F32)),
        scratch_shapes=[pltpu.VMEM(st_shape, F32)],
        compiler_params=_params("arbitrary"),
        name="ret_prompt",
    )(g_chunk, qr, kr, vr, gr, decay, zeta, xi, gain)


def _ret_sample_kernel(gc_ref, q_ref, k_ref, v_ref, g_ref, s_ref, decay_ref, zeta_ref, xi_ref,
                       gain_ref, o_ref, st_ref):
    for h in range(RET_HEADS):
        sl = slice(h * RET_DIM, (h + 1) * RET_DIM)
        o, s_new = _retention_head(q_ref[:, sl], k_ref[:, sl], v_ref[:, sl], g_ref[:, sl],
                                   s_ref[0, h], decay_ref[h], zeta_ref[h], xi_ref[h], gc_ref[h],
                                   gain_ref[:, sl])
        st_ref[0, h] = s_new
        o_ref[:, sl] = o.astype(BF16)


def _ret_sample(qr, kr, vr, gr, state, consts, gain):
    g_chunk, decay, zeta, xi = consts
    first = SEQ // DEC_SEQ
    row = pl.BlockSpec((DEC_SEQ, RET_WIDTH), lambda b: (first + b, 0))
    st = pl.BlockSpec((1, RET_HEADS, RET_DIM, RET_DIM), lambda b: (b, 0, 0, 0))
    return pl.pallas_call(
        _ret_sample_kernel,
        grid=(DEC_BATCH,),
        in_specs=[pl.BlockSpec(memory_space=pltpu.SMEM), row, row, row, row, st,
                  _resident(decay.shape), _resident(zeta.shape), _resident(xi.shape),
                  _resident(gain.shape)],
        out_specs=(pl.BlockSpec((DEC_SEQ, RET_WIDTH), lambda b: (b, 0)), st),
        out_shape=(jax.ShapeDtypeStruct((N_SAMPLE, RET_WIDTH), BF16),
                   jax.ShapeDtypeStruct(state.shape, F32)),
        compiler_params=_params("parallel"),
        name="ret_sample",
    )(g_chunk, qr, kr, vr, gr, state, decay, zeta, xi, gain)


def _retention_consts(chunk):
    lg = jnp.log(1.0 - 2.0 ** (-5.0 - jnp.arange(RET_HEADS, dtype=F32)))
    idx = jnp.arange(chunk, dtype=F32)
    diff = idx[:, None] - idx[None, :]
    decay = jnp.where(diff >= 0, jnp.exp(lg[:, None, None] * jnp.maximum(diff, 0.0)), 0.0)
    zeta = jnp.exp(lg[:, None] * (chunk - 1 - idx))[:, :, None]
    xi = jnp.exp(lg[:, None] * (idx + 1))[:, :, None]
    return jnp.exp(lg * chunk), decay, zeta, xi


def _out_proj_kernel(oap_ref, oas_ref, orp_ref, ors_ref, hp_ref, hs_ref, w_ref, g_ref,
                     h1_ref, xn_ref):
    h1 = (_pick_rows(hp_ref, hs_ref)
          + jnp.dot(_pick_rows(oap_ref, oas_ref), w_ref[0:ATT_WIDTH, :],
                    preferred_element_type=F32)
          + jnp.dot(_pick_rows(orp_ref, ors_ref), w_ref[ATT_WIDTH:, :],
                    preferred_element_type=F32))
    h1_ref[...] = h1
    xn_ref[...] = _rms(h1, g_ref[...]).astype(BF16)


def _out_proj(oa_p, oa_s, or_p, or_s, hp, hs, w, g):
    row = _stacked_rows
    return pl.pallas_call(
        _out_proj_kernel,
        grid=(N_TOK // ROW_TILE,),
        in_specs=[_prompt_rows(ATT_WIDTH), _sample_rows(ATT_WIDTH),
                  _prompt_rows(RET_WIDTH), _sample_rows(RET_WIDTH),
                  _prompt_rows(D_MODEL), _sample_rows(D_MODEL),
                  _resident(w.shape), _resident(g.shape)],
        out_specs=(row(D_MODEL), row(D_MODEL)),
        out_shape=(jax.ShapeDtypeStruct((N_TOK, D_MODEL), F32),
                   jax.ShapeDtypeStruct((N_TOK, D_MODEL), BF16)),
        compiler_params=_params("parallel"),
        name="out_proj",
    )(oa_p, oa_s, or_p, or_s, hp, hs, w, g)


def _mlp_kernel(xn_ref, wu_ref, wd_ref, o_ref):
    @pl.when(pl.program_id(1) == 0)
    def _():
        o_ref[...] = jnp.zeros(o_ref.shape, F32)

    up = jnp.dot(xn_ref[...], wu_ref[...].astype(BF16), preferred_element_type=F32)
    act = jnp.square(jnp.maximum(up, 0.0)).astype(BF16)
    o_ref[...] += jnp.dot(act, wd_ref[...].astype(BF16), preferred_element_type=F32)


def _mlp(xn, w_up, w_down, layer):
    tm, tf = MLP_ROW_TILE, MLP_FF_TILE
    return pl.pallas_call(
        _mlp_kernel,
        grid=(N_TOK // tm, D_FF // tf),
        in_specs=[pl.BlockSpec((tm, D_MODEL), lambda i, f: (i, 0)),
                  pl.BlockSpec((None, D_MODEL, tf), lambda i, f: (layer, 0, f)),
                  pl.BlockSpec((None, tf, D_MODEL), lambda i, f: (layer, f, 0))],
        out_specs=pl.BlockSpec((tm, D_MODEL), lambda i, f: (i, 0)),
        out_shape=jax.ShapeDtypeStruct((N_TOK, D_MODEL), F32),
        compiler_params=_params("parallel", "arbitrary"),
        name="mlp",
    )(xn, w_up, w_down)


def _gate_kernel(h1_ref, m_ref, pp_ref, ps_ref, wg_ref, wp_ref, g_ref, gf_ref, op_ref, os_ref,
                 *, final):
    h2 = h1_ref[...] + m_ref[...]
    xn = _rms(h2, g_ref[...]).astype(BF16)
    gate = jax.nn.sigmoid(jnp.dot(xn, wg_ref[...], preferred_element_type=F32))
    emb = jnp.dot(_pick_rows(pp_ref, ps_ref).astype(BF16), wp_ref[...],
                  preferred_element_type=F32)
    h3 = h2 + gate * emb
    out = _rms(h3, gf_ref[...]) if final else h3
    is_prompt = pl.program_id(0) < N_PROMPT_TILES

    @pl.when(is_prompt)
    def _():
        op_ref[...] = out

    @pl.when(jnp.logical_not(is_prompt))
    def _():
        os_ref[...] = out


def _gate(h1, mlp_out, p_prompt, p_sample, layer, w_gate, w_ple, g, g_final, final):
    row = _stacked_rows
    return pl.pallas_call(
        functools.partial(_gate_kernel, final=final),
        grid=(N_TOK // ROW_TILE,),
        in_specs=[row(D_MODEL), row(D_MODEL),
                  _prompt_rows(PLE_DIM, layer, 0), _sample_rows(PLE_DIM, layer),
                  _resident(w_gate.shape), _resident(w_ple.shape),
                  _resident(g.shape), _resident(g_final.shape)],
        out_specs=(_prompt_rows(D_MODEL), _sample_rows(D_MODEL)),
        out_shape=(jax.ShapeDtypeStruct((SEQ, D_MODEL), F32),
                   jax.ShapeDtypeStruct((N_SAMPLE, D_MODEL), F32)),
        compiler_params=_params("arbitrary"),
        name="gate",
    )(h1, mlp_out, p_prompt, p_sample, w_gate, w_ple, g, g_final)


def _t5_bucket(rel):
    nb = N_BUCKETS // 2
    max_exact = nb // 2
    ret = jnp.where(rel > 0, nb, 0)
    n = jnp.abs(rel)
    nf = jnp.maximum(n, max_exact).astype(F32)
    large = max_exact + (jnp.log(nf / max_exact) / math.log(MAX_DISTANCE / max_exact)
                         * (nb - max_exact)).astype(jnp.int32)
    large = jnp.minimum(large, nb - 1)
    return ret + jnp.where(n < max_exact, n, large)


def _rotary_tables():
    half = RET_DIM // 2
    pos = jnp.concatenate([jnp.arange(SEQ, dtype=jnp.int32),
                           jnp.tile(PAST_LEN + jnp.arange(DEC_SEQ, dtype=jnp.int32), DEC_BATCH)])
    inv = ROPE_BASE ** (-jnp.arange(half, dtype=F32) / half)
    ang = pos.astype(F32)[:, None] * inv[None, :]
    cos, sin = jnp.cos(ang), jnp.sin(ang)
    return jnp.concatenate([cos, cos], axis=-1), jnp.concatenate([-sin, sin], axis=-1)


def kernel(x_prompt, x_sample, cache_k, cache_v, state_ret, p_prompt, p_sample, rel_bias,
           norm_mix, w_in, attn_sink, ret_gain, w_out, norm_mlp, w_up, w_down, norm_ple,
           w_gate, w_ple, norm_final):
    q_idx = jnp.arange(Q_BLOCK, dtype=jnp.int32)[:, None]
    s_idx = jnp.arange(K_BLOCK, dtype=jnp.int32)[None, :]
    bucket_p = _t5_bucket(s_idx - WINDOW - q_idx)
    bucket_s = _t5_bucket(jnp.arange(WINDOW + DEC_SEQ, dtype=jnp.int32)[None, :] - WINDOW
                          - jnp.arange(DEC_SEQ, dtype=jnp.int32)[:, None])
    bias_p, bias_s = _rel_bias(rel_bias, bucket_p, bucket_s)
    back = q_idx // CHUNK + WINDOW // CHUNK - s_idx // CHUNK
    seen = jnp.logical_and(back >= 0, back <= WINDOW // CHUNK)
    seen_first = jnp.logical_and(seen, s_idx >= WINDOW)
    bias_p = _pair_layout(jnp.stack([jnp.where(seen, bias_p, -jnp.inf),
                                     jnp.where(seen_first, bias_p, -jnp.inf)]))
    bias_s = bias_s.reshape(N_KV, GROUP * DEC_SEQ, WINDOW + DEC_SEQ)
    bias_sc, bias_sn = bias_s[:, :, :WINDOW], bias_s[:, :, WINDOW:]

    cos2, sin2 = _rotary_tables()
    consts_p = _retention_consts(RET_TILE)
    consts_s = _retention_consts(DEC_SEQ)

    hp = x_prompt.reshape(SEQ, D_MODEL)
    hs = x_sample.reshape(N_SAMPLE, D_MODEL)
    p_sample = p_sample.reshape(DEPTH, N_SAMPLE, PLE_DIM)
    k_p, v_p, s_p, k_s, v_s, s_s = [], [], [], [], [], []
    for l in range(DEPTH):
        sink = attn_sink[l].astype(F32).reshape(N_KV, GROUP, 1, 1)
        sink_p = _pair_layout(jnp.broadcast_to(sink.reshape(N_HEADS, 1, 1), (N_HEADS, Q_BLOCK, 1)))
        sink_s = jnp.broadcast_to(sink, (N_KV, GROUP, DEC_SEQ, 1)).reshape(N_KV, GROUP * DEC_SEQ, 1)
        gain = ret_gain[l].reshape(1, RET_WIDTH)

        qa, k, v, qr, kr, vr, gr = _in_proj(hp, hs, norm_mix[l].reshape(1, D_MODEL),
                                            w_in[l].astype(BF16), cos2, sin2)
        oa_p = _attn_prompt(qa, k, v, bias_p, sink_p)
        oa_s = _attn_sample(qa, k, v, cache_k[l].reshape(DEC_BATCH, WINDOW, KV_WIDTH),
                            cache_v[l].reshape(DEC_BATCH, WINDOW, KV_WIDTH),
                            bias_sc, bias_sn, sink_s)
        or_p, st_p = _ret_prompt(qr, kr, vr, gr, consts_p, gain)
        or_s, st_s = _ret_sample(qr, kr, vr, gr, state_ret[l], consts_s, gain)

        h1, xn = _out_proj(oa_p, oa_s, or_p, or_s, hp, hs,
                           w_out[l].astype(BF16), norm_mlp[l].reshape(1, D_MODEL))
        mlp_out = _mlp(xn, w_up, w_down, l)
        hp, hs = _gate(h1, mlp_out, p_prompt, p_sample, l,
                       w_gate[l].astype(BF16), w_ple[l].astype(BF16),
                       norm_ple[l].reshape(1, D_MODEL), norm_final.reshape(1, D_MODEL),
                       final=(l == DEPTH - 1))

        k_p.append(k[SEQ - WINDOW:SEQ].reshape(1, WINDOW, N_KV, HEAD_DIM))
        v_p.append(v[SEQ - WINDOW:SEQ].reshape(1, WINDOW, N_KV, HEAD_DIM))
        s_p.append(st_p[None])
        k_s.append(k[SEQ:].reshape(DEC_BATCH, DEC_SEQ, N_KV, HEAD_DIM))
        v_s.append(v[SEQ:].reshape(DEC_BATCH, DEC_SEQ, N_KV, HEAD_DIM))
        s_s.append(st_s)

    return (hp.reshape(1, SEQ, D_MODEL), hs.reshape(DEC_BATCH, DEC_SEQ, D_MODEL),
            jnp.stack(k_p), jnp.stack(v_p), jnp.stack(s_p),
            jnp.stack(k_s), jnp.stack(v_s), jnp.stack(s_s))
```

```python
import functools
import math

import jax
import jax.numpy as jnp
from jax import lax
from jax.experimental import pallas as pl
from jax.experimental.pallas import tpu as pltpu

F32 = jnp.float32
BF16 = jnp.bfloat16

D_MODEL = 2048
SEQ = 8192
DEPTH = 2
DEC_BATCH = 32
DEC_SEQ = 16
PAST_LEN = 4096
N_SAMPLE = DEC_BATCH * DEC_SEQ
N_TOK = SEQ + N_SAMPLE

CHUNK = 64
WINDOW = 128
BAND = WINDOW + CHUNK
HEAD_DIM = 64
N_HEADS = 16
N_KV = 4
GROUP = N_HEADS // N_KV
ATT_WIDTH = N_HEADS * HEAD_DIM
KV_WIDTH = N_KV * HEAD_DIM
RET_DIM = 128
RET_HEADS = 8
RET_WIDTH = RET_HEADS * RET_DIM
IN_WIDTH = ATT_WIDTH + 2 * KV_WIDTH + 4 * RET_WIDTH
D_FF = 4 * D_MODEL
PLE_DIM = 256
N_BUCKETS = 32
MAX_DISTANCE = 128
ROPE_BASE = 10000.0
EPS = 1e-6
ATT_SCALE = HEAD_DIM ** -0.5
RET_SCALE = RET_DIM ** -0.5

ROW_TILE = 512
ROW_PARTS = 2
WEIGHT_STAGE_ROWS = 64
MLP_ROW_TILE = 1088
MLP_FF_TILE = 512
ATT_TILE = 512
Q_BLOCK = 2 * CHUNK
K_BLOCK = WINDOW + Q_BLOCK
RET_TILE = 256
SAMPLE_BLOCK = 8
SAMPLE_KEYS = 256
RET_SAMPLE_BLOCK = 4
VMEM_LIMIT = 58 * 1024 * 1024

NT_DIMS = (((1,), (1,)), ((), ()))
TN_DIMS = (((0,), (0,)), ((), ()))


def _params(*sem):
    return pltpu.CompilerParams(dimension_semantics=sem, vmem_limit_bytes=VMEM_LIMIT)


def _resident(shape):
    nd = len(shape)
    return pl.BlockSpec(shape, lambda *_: (0,) * nd, pipeline_mode=pl.Buffered(1))


def _resident_layer(shape, layer):
    nd = len(shape) - 1
    return pl.BlockSpec((None,) + tuple(shape[1:]), lambda *_: (layer,) + (0,) * nd,
                        pipeline_mode=pl.Buffered(1))


def _hbm_layer(w):
    del w
    return pl.BlockSpec(memory_space=pl.ANY)


def _weight_scratch(w):
    _, k, n = w.shape
    return [pltpu.VMEM((k, n), BF16), pltpu.VMEM((2, WEIGHT_STAGE_ROWS, n), F32),
            pltpu.SemaphoreType.DMA((2,))]


def _load_weight_bf16(w_hbm, layer, dst, stage, sem):
    rows = stage.shape[1]
    n_chunks = dst.shape[0] // rows
    assert n_chunks * rows == dst.shape[0]

    def chunk_copy(c, slot):
        return pltpu.make_async_copy(w_hbm.at[layer, c * rows:(c + 1) * rows, :],
                                     stage.at[slot], sem.at[slot])

    chunk_copy(0, 0).start()
    for c in range(n_chunks):
        slot = c % 2
        if c + 1 < n_chunks:
            chunk_copy(c + 1, 1 - slot).start()
        chunk_copy(c, slot).wait()
        dst[c * rows:(c + 1) * rows, :] = stage[slot].astype(BF16)


def _rms(x, g):
    ms = jnp.mean(x * x, axis=-1, keepdims=True)
    return x * lax.rsqrt(ms + EPS) * g


N_PROMPT_TILES = SEQ // ROW_TILE
assert N_SAMPLE == ROW_TILE


def _stacked_rows(width):
    return pl.BlockSpec((ROW_TILE, width),
                        lambda i: ((i + N_PROMPT_TILES) % (N_PROMPT_TILES + 1), 0))


def _prompt_rows(width, *lead):
    none = (None,) * len(lead)
    return pl.BlockSpec(none + (ROW_TILE, width), lambda i: lead + (jnp.maximum(i - 1, 0), 0))


def _sample_rows(width, *lead, single_buffer=True):
    none = (None,) * len(lead)
    mode = dict(pipeline_mode=pl.Buffered(1)) if single_buffer else {}
    return pl.BlockSpec(none + (ROW_TILE, width), lambda i: lead + (0, 0), **mode)


def _row_parts():
    part = ROW_TILE // ROW_PARTS
    return [slice(r * part, (r + 1) * part) for r in range(ROW_PARTS)]


def _pick_rows(prompt_ref, sample_ref, rows=slice(None)):
    return jnp.where(pl.program_id(0) == 0, sample_ref[rows, :], prompt_ref[rows, :])


def _bias_kernel(table_ref, bp_ref, bs_ref, op_ref, os_ref):
    bp = bp_ref[...]
    bs = bs_ref[...]

    def per_head(h, c):
        ap = jnp.zeros(bp.shape, F32)
        as_ = jnp.zeros(bs.shape, F32)
        for b in range(N_BUCKETS):
            t = table_ref[b, h]
            ap = jnp.where(bp == b, t, ap)
            as_ = jnp.where(bs == b, t, as_)
        op_ref[h] = ap
        os_ref[h] = as_
        return c

    lax.fori_loop(0, N_HEADS, per_head, 0)


def _rel_bias(table, bucket_p, bucket_s):
    return pl.pallas_call(
        _bias_kernel,
        out_shape=(jax.ShapeDtypeStruct((N_HEADS,) + bucket_p.shape, F32),
                   jax.ShapeDtypeStruct((N_HEADS,) + bucket_s.shape, F32)),
        in_specs=[pl.BlockSpec(memory_space=pltpu.SMEM),
                  pl.BlockSpec(memory_space=pltpu.VMEM),
                  pl.BlockSpec(memory_space=pltpu.VMEM)],
        out_specs=(pl.BlockSpec(memory_space=pltpu.VMEM), pl.BlockSpec(memory_space=pltpu.VMEM)),
        name="rel_bias",
    )(table, bucket_p, bucket_s)


def _in_proj_kernel(hp_ref, hs_ref, g_ref, w_hbm, cos_ref, sin_ref,
                    qa_ref, k_ref, v_ref, qr_ref, kr_ref, vr_ref, gr_ref,
                    w_ref, stage, sem, *, layer):
    @pl.when(pl.program_id(0) == 0)
    def _():
        _load_weight_bf16(w_hbm, layer, w_ref, stage, sem)

    base = ATT_WIDTH + 2 * KV_WIDTH
    for rows in _row_parts():
        xn = _rms(_pick_rows(hp_ref, hs_ref, rows), g_ref[...]).astype(BF16)

        def proj(lo, width):
            return jnp.dot(xn, w_ref[:, lo:lo + width], preferred_element_type=F32)

        qa_ref[rows, :] = (proj(0, ATT_WIDTH) * ATT_SCALE).astype(BF16)
        k_ref[rows, :] = proj(ATT_WIDTH, KV_WIDTH)
        v_ref[rows, :] = proj(ATT_WIDTH + KV_WIDTH, KV_WIDTH)
        cos = cos_ref[rows, :]
        sin = sin_ref[rows, :]

        def rot(x):
            return x * cos + pltpu.roll(x, RET_DIM // 2, 1) * sin

        q = proj(base, RET_WIDTH)
        for h in range(RET_HEADS):
            sl = slice(h * RET_DIM, (h + 1) * RET_DIM)
            qr_ref[rows, sl] = rot(q[:, sl]).astype(BF16)
        k = proj(base + RET_WIDTH, RET_WIDTH)
        for h in range(RET_HEADS):
            sl = slice(h * RET_DIM, (h + 1) * RET_DIM)
            kr_ref[rows, sl] = (rot(k[:, sl]) * RET_SCALE).astype(BF16)
        vr_ref[rows, :] = proj(base + 2 * RET_WIDTH, RET_WIDTH).astype(BF16)
        gr_ref[rows, :] = proj(base + 3 * RET_WIDTH, RET_WIDTH).astype(BF16)


def _in_proj(hp, hs, g, w, layer, cos2, sin2):
    tm = ROW_TILE
    row = _stacked_rows
    out = lambda width, dt: jax.ShapeDtypeStruct((N_TOK, width), dt)
    return pl.pallas_call(
        functools.partial(_in_proj_kernel, layer=layer),
        grid=(N_TOK // tm,),
        in_specs=[_prompt_rows(D_MODEL), _sample_rows(D_MODEL),
                  _resident_layer(g.shape, layer), _hbm_layer(w),
                  row(RET_DIM), row(RET_DIM)],
        out_specs=(row(ATT_WIDTH), row(KV_WIDTH), row(KV_WIDTH),
                   row(RET_WIDTH), row(RET_WIDTH), row(RET_WIDTH), row(RET_WIDTH)),
        out_shape=(out(ATT_WIDTH, BF16), out(KV_WIDTH, F32), out(KV_WIDTH, F32),
                   out(RET_WIDTH, BF16), out(RET_WIDTH, BF16), out(RET_WIDTH, BF16),
                   out(RET_WIDTH, BF16)),
        scratch_shapes=_weight_scratch(w),
        compiler_params=_params("arbitrary"),
        name="in_proj",
    )(hp, hs, g, w, cos2, sin2)


PAIR = 2 * HEAD_DIM


def _attend_pairs(qst, kcat, vext, bias, sinks):
    half = kcat.shape[0] // 2
    s = lax.dot_general(qst, kcat, NT_DIMS, preferred_element_type=F32) + bias
    ps, es = [], []
    for side in range(2):
        ss = s[:, side * half:(side + 1) * half]
        m = jnp.maximum(jnp.max(ss, axis=-1, keepdims=True), sinks[side])
        ps.append(jnp.exp(ss - jnp.concatenate([m] * (half // PAIR), axis=1)))
        es.append(jnp.exp(sinks[side] - m))
    p = jnp.concatenate(ps, axis=1).astype(BF16)
    r = jnp.dot(p, vext, preferred_element_type=F32)
    lo = lax.broadcasted_iota(jnp.int32, qst.shape, 1) < HEAD_DIM
    den = r[:, PAIR:] + jnp.where(lo, es[0], es[1])
    return r[:, :PAIR] / den


def _fill_padded(dst, rows, x, sum_lanes=False):
    lo = lax.broadcasted_iota(jnp.int32, (x.shape[0], PAIR), 1) < HEAD_DIM
    for pair in range(N_KV // 2):
        xp = x[:, pair * PAIR:(pair + 1) * PAIR]
        xr = pltpu.roll(xp, HEAD_DIM, 1)
        zero = jnp.zeros_like(xp)
        dst[2 * pair, 0, rows, 0:PAIR] = jnp.where(lo, xp, zero).astype(BF16)
        dst[2 * pair, 1, rows, 0:PAIR] = jnp.where(lo, zero, xr).astype(BF16)
        dst[2 * pair + 1, 0, rows, 0:PAIR] = jnp.where(lo, xr, zero).astype(BF16)
        dst[2 * pair + 1, 1, rows, 0:PAIR] = jnp.where(lo, zero, xp).astype(BF16)
    if sum_lanes:
        for kv in range(N_KV):
            dst[kv, 0, rows, PAIR:2 * PAIR] = jnp.where(lo, 1.0, 0.0).astype(BF16)
            dst[kv, 1, rows, PAIR:2 * PAIR] = jnp.where(lo, 0.0, 1.0).astype(BF16)


def _attn_prompt_kernel(q_ref, kc_ref, kp_ref, vc_ref, vp_ref, bias_ref, sink_ref, o_ref, kz, vz):
    i = pl.program_id(0)
    _fill_padded(kz, slice(0, WINDOW), kp_ref[...])
    _fill_padded(kz, slice(WINDOW, WINDOW + ATT_TILE), kc_ref[...])
    _fill_padded(vz, slice(0, WINDOW), vp_ref[...], sum_lanes=True)
    _fill_padded(vz, slice(WINDOW, WINDOW + ATT_TILE), vc_ref[...], sum_lanes=True)

    def block(j, c):
        r0 = pl.multiple_of(j * Q_BLOCK, Q_BLOCK)
        rows = pl.ds(r0, Q_BLOCK)
        band = pl.ds(r0, K_BLOCK)
        first = jnp.where(jnp.logical_and(i == 0, j == 0), 1, 0)
        for kv in range(N_KV):
            c0 = kv * GROUP * HEAD_DIM
            qst = jnp.concatenate([q_ref[rows, c0:c0 + 2 * HEAD_DIM],
                                   q_ref[rows, c0 + 2 * HEAD_DIM:c0 + 4 * HEAD_DIM]], axis=0)
            o = _attend_pairs(
                qst,
                jnp.concatenate([kz[kv, side, band, :] for side in range(2)], axis=0),
                jnp.concatenate([vz[kv, side, band, :] for side in range(2)], axis=0),
                bias_ref[first, kv], [sink_ref[kv, side] for side in range(2)])
            o_ref[rows, c0:c0 + 2 * HEAD_DIM] = o[0:Q_BLOCK].astype(BF16)
            o_ref[rows, c0 + 2 * HEAD_DIM:c0 + 4 * HEAD_DIM] = o[Q_BLOCK:].astype(BF16)
        return c

    lax.fori_loop(0, ATT_TILE // Q_BLOCK, block, 0)


def _attn_prompt(qa, k, v, bias, sink):
    t = ATT_TILE
    per = t // WINDOW
    cur = lambda width: pl.BlockSpec((t, width), lambda i: (i, 0))
    prev = pl.BlockSpec((WINDOW, KV_WIDTH), lambda i: (jnp.maximum(i * per - 1, 0), 0))
    return pl.pallas_call(
        _attn_prompt_kernel,
        grid=(SEQ // t,),
        in_specs=[cur(ATT_WIDTH), cur(KV_WIDTH), prev, cur(KV_WIDTH), prev,
                  _resident(bias.shape), _resident(sink.shape)],
        out_specs=cur(ATT_WIDTH),
        out_shape=jax.ShapeDtypeStruct((SEQ, ATT_WIDTH), BF16),
        scratch_shapes=[pltpu.VMEM((N_KV, 2, t + WINDOW, PAIR), BF16),
                        pltpu.VMEM((N_KV, 2, t + WINDOW, 2 * PAIR), BF16)],
        compiler_params=_params("parallel"),
        name="attn_prompt",
    )(qa, k, k, v, v, bias, sink)


def _pair_layout(x):
    lead = x.shape[:-3]
    rows, cols = x.shape[-2:]
    x = x.reshape(lead + (N_KV, 2, 2, rows, cols))
    x = jnp.swapaxes(x, -4, -3)
    return x.reshape(lead + (N_KV, 2, 2 * rows, cols))


def _merge_sides(x):
    rows, cols = x.shape[-2:]
    return jnp.swapaxes(x, -3, -2).reshape(x.shape[:-3] + (rows, 2 * cols))


def _attn_sample_kernel(q_ref, kn_ref, vn_ref, ck_ref, cv_ref, bias_ref, sink_ref, o_ref, kz, vz):
    keys = WINDOW + DEC_SEQ
    kz[:, :, :, keys:, :] = jnp.zeros((SAMPLE_BLOCK, N_KV, 2, SAMPLE_KEYS - keys, PAIR), BF16)
    vz[:, :, :, keys:, :] = jnp.zeros((SAMPLE_BLOCK, N_KV, 2, SAMPLE_KEYS - keys, 2 * PAIR), BF16)
    for b in range(SAMPLE_BLOCK):
        new = slice(b * DEC_SEQ, (b + 1) * DEC_SEQ)
        _fill_padded(kz.at[b], slice(0, WINDOW), ck_ref[b])
        _fill_padded(kz.at[b], slice(WINDOW, keys), kn_ref[new, :])
        _fill_padded(vz.at[b], slice(0, WINDOW), cv_ref[b], sum_lanes=True)
        _fill_padded(vz.at[b], slice(WINDOW, keys), vn_ref[new, :], sum_lanes=True)
    for b in range(SAMPLE_BLOCK):
        rows = slice(b * DEC_SEQ, (b + 1) * DEC_SEQ)
        for kv in range(N_KV):
            c0 = kv * GROUP * HEAD_DIM
            qst = jnp.concatenate([q_ref[rows, c0:c0 + 2 * HEAD_DIM],
                                   q_ref[rows, c0 + 2 * HEAD_DIM:c0 + 4 * HEAD_DIM]], axis=0)
            o = _attend_pairs(qst, kz[b, kv].reshape(2 * SAMPLE_KEYS, PAIR),
                              vz[b, kv].reshape(2 * SAMPLE_KEYS, 2 * PAIR),
                              bias_ref[kv], [sink_ref[kv, side] for side in range(2)])
            o_ref[rows, c0:c0 + 2 * HEAD_DIM] = o[0:DEC_SEQ].astype(BF16)
            o_ref[rows, c0 + 2 * HEAD_DIM:c0 + 4 * HEAD_DIM] = o[DEC_SEQ:].astype(BF16)


def _attn_sample(qa, k, v, cache_k, cache_v, bias, sink):
    rows = SAMPLE_BLOCK * DEC_SEQ
    first = SEQ // rows
    new = lambda width: pl.BlockSpec((rows, width), lambda i: (first + i, 0))
    cache = pl.BlockSpec((SAMPLE_BLOCK, WINDOW, KV_WIDTH), lambda i: (i, 0, 0))
    return pl.pallas_call(
        _attn_sample_kernel,
        grid=(DEC_BATCH // SAMPLE_BLOCK,),
        in_specs=[new(ATT_WIDTH), new(KV_WIDTH), new(KV_WIDTH), cache, cache,
                  _resident(bias.shape), _resident(sink.shape)],
        out_specs=pl.BlockSpec((rows, ATT_WIDTH), lambda i: (i, 0)),
        out_shape=jax.ShapeDtypeStruct((N_SAMPLE, ATT_WIDTH), BF16),
        scratch_shapes=[pltpu.VMEM((SAMPLE_BLOCK, N_KV, 2, SAMPLE_KEYS, PAIR), BF16),
                        pltpu.VMEM((SAMPLE_BLOCK, N_KV, 2, SAMPLE_KEYS, 2 * PAIR), BF16)],
        compiler_params=_params("parallel"),
        name="attn_sample",
    )(qa, k, v, cache_k, cache_v, bias, sink)


def _retention_head(q, k, v, g, state, decay, zeta, xi, g_chunk, gain):
    scores = lax.dot_general(q, k, NT_DIMS, preferred_element_type=F32) * decay
    if q.shape[0] % RET_DIM == 0:
        lhs = jnp.concatenate([(q.astype(F32) * xi).astype(BF16), scores.astype(BF16)], axis=1)
        rhs = jnp.concatenate([state.astype(BF16), v], axis=0)
        o = jnp.dot(lhs, rhs, preferred_element_type=F32)
    else:
        o = jnp.dot(scores.astype(BF16), v, preferred_element_type=F32)
        o = o + jnp.dot(q, state.astype(BF16), preferred_element_type=F32) * xi
    kz = (k.astype(F32) * zeta).astype(BF16)
    upd = lax.dot_general(kz, v, TN_DIMS, preferred_element_type=F32)
    new_state = g_chunk * state + upd
    o = o * lax.rsqrt(jnp.mean(o * o, axis=-1, keepdims=True) + EPS) * gain
    gf = g.astype(F32)
    return gf * jax.nn.sigmoid(gf) * o, new_state


def _ret_prompt_kernel(gc_ref, q_ref, k_ref, v_ref, g_ref, decay_ref, zeta_ref, xi_ref, gain_ref,
                       o_ref, st_ref, state):
    i = pl.program_id(0)

    @pl.when(i == 0)
    def _():
        state[...] = jnp.zeros(state.shape, F32)

    for h in range(RET_HEADS):
        sl = slice(h * RET_DIM, (h + 1) * RET_DIM)
        o, s_new = _retention_head(q_ref[:, sl], k_ref[:, sl], v_ref[:, sl], g_ref[:, sl],
                                   state[h], decay_ref[h], zeta_ref[h], xi_ref[h], gc_ref[h],
                                   gain_ref[:, sl])
        state[h] = s_new
        o_ref[:, sl] = o.astype(BF16)

    @pl.when(i == pl.num_programs(0) - 1)
    def _():
        st_ref[...] = state[...]


def _ret_prompt(qr, kr, vr, gr, consts, gain):
    g_chunk, decay, zeta, xi = consts
    c = RET_TILE
    row = pl.BlockSpec((c, RET_WIDTH), lambda i: (i, 0))
    st_shape = (RET_HEADS, RET_DIM, RET_DIM)
    return pl.pallas_call(
        _ret_prompt_kernel,
        grid=(SEQ // c,),
        in_specs=[pl.BlockSpec(memory_space=pltpu.SMEM), row, row, row, row,
                  _resident(decay.shape), _resident(zeta.shape), _resident(xi.shape),
                  _resident(gain.shape)],
        out_specs=(row, pl.BlockSpec(st_shape, lambda i: (0, 0, 0))),
        out_shape=(jax.ShapeDtypeStruct((SEQ, RET_WIDTH), BF16),
                   jax.ShapeDtypeStruct(st_shape, F32)),
        scratch_shapes=[pltpu.VMEM(st_shape, F32)],
        compiler_params=_params("arbitrary"),
        name="ret_prompt",
    )(g_chunk, qr, kr, vr, gr, decay, zeta, xi, gain)


def _ret_sample_kernel(gc_ref, q_ref, k_ref, v_ref, g_ref, s_ref, decay_ref, zeta_ref, xi_ref,
                       gain_ref, o_ref, st_ref):
    for b in range(RET_SAMPLE_BLOCK):
        rows = slice(b * DEC_SEQ, (b + 1) * DEC_SEQ)
        for h in range(RET_HEADS):
            sl = slice(h * RET_DIM, (h + 1) * RET_DIM)
            o, s_new = _retention_head(q_ref[rows, sl], k_ref[rows, sl], v_ref[rows, sl],
                                       g_ref[rows, sl], s_ref[b, h], decay_ref[h], zeta_ref[h],
                                       xi_ref[h], gc_ref[h], gain_ref[:, sl])
            st_ref[b, h] = s_new
            o_ref[rows, sl] = o.astype(BF16)


def _ret_sample(qr, kr, vr, gr, state, layer, consts, gain):
    g_chunk, decay, zeta, xi = consts
    nb = RET_SAMPLE_BLOCK
    first = SEQ // (nb * DEC_SEQ)
    row = pl.BlockSpec((nb * DEC_SEQ, RET_WIDTH), lambda i: (first + i, 0))
    st_block = (nb, RET_HEADS, RET_DIM, RET_DIM)
    return pl.pallas_call(
        _ret_sample_kernel,
        grid=(DEC_BATCH // nb,),
        in_specs=[pl.BlockSpec(memory_space=pltpu.SMEM), row, row, row, row,
                  pl.BlockSpec((None,) + st_block, lambda i: (layer, i, 0, 0, 0)),
                  _resident(decay.shape), _resident(zeta.shape), _resident(xi.shape),
                  _resident(gain.shape)],
        out_specs=(pl.BlockSpec((nb * DEC_SEQ, RET_WIDTH), lambda i: (i, 0)),
                   pl.BlockSpec(st_block, lambda i: (i, 0, 0, 0))),
        out_shape=(jax.ShapeDtypeStruct((N_SAMPLE, RET_WIDTH), BF16),
                   jax.ShapeDtypeStruct(state.shape[1:], F32)),
        compiler_params=_params("parallel"),
        name="ret_sample",
    )(g_chunk, qr, kr, vr, gr, state, decay, zeta, xi, gain)


def _retention_consts(chunk):
    lg = jnp.log(1.0 - 2.0 ** (-5.0 - jnp.arange(RET_HEADS, dtype=F32)))
    idx = jnp.arange(chunk, dtype=F32)
    diff = idx[:, None] - idx[None, :]
    decay = jnp.where(diff >= 0, jnp.exp(lg[:, None, None] * jnp.maximum(diff, 0.0)), 0.0)
    zeta = jnp.exp(lg[:, None] * (chunk - 1 - idx))[:, :, None]
    xi = jnp.exp(lg[:, None] * (idx + 1))[:, :, None]
    return jnp.exp(lg * chunk), decay, zeta, xi


def _out_proj_kernel(oap_ref, oas_ref, orp_ref, ors_ref, hp_ref, hs_ref, w_hbm, g_ref,
                     h1_ref, xn_ref, w_ref, stage, sem, *, layer):
    @pl.when(pl.program_id(0) == 0)
    def _():
        _load_weight_bf16(w_hbm, layer, w_ref, stage, sem)

    for rows in _row_parts():
        h1 = (_pick_rows(hp_ref, hs_ref, rows)
              + jnp.dot(_pick_rows(oap_ref, oas_ref, rows), w_ref[0:ATT_WIDTH, :],
                        preferred_element_type=F32)
              + jnp.dot(_pick_rows(orp_ref, ors_ref, rows), w_ref[ATT_WIDTH:, :],
                        preferred_element_type=F32))
        h1_ref[rows, :] = h1
        xn_ref[rows, :] = _rms(h1, g_ref[...]).astype(BF16)


def _out_proj(oa_p, oa_s, or_p, or_s, hp, hs, w, g, layer):
    row = _stacked_rows
    return pl.pallas_call(
        functools.partial(_out_proj_kernel, layer=layer),
        grid=(N_TOK // ROW_TILE,),
        in_specs=[_prompt_rows(ATT_WIDTH), _sample_rows(ATT_WIDTH),
                  _prompt_rows(RET_WIDTH), _sample_rows(RET_WIDTH),
                  _prompt_rows(D_MODEL), _sample_rows(D_MODEL),
                  _hbm_layer(w), _resident_layer(g.shape, layer)],
        out_specs=(row(D_MODEL), row(D_MODEL)),
        out_shape=(jax.ShapeDtypeStruct((N_TOK, D_MODEL), F32),
                   jax.ShapeDtypeStruct((N_TOK, D_MODEL), BF16)),
        scratch_shapes=_weight_scratch(w),
        compiler_params=_params("arbitrary"),
        name="out_proj",
    )(oa_p, oa_s, or_p, or_s, hp, hs, w, g)


def _mlp_kernel(xn_ref, wu_hbm, wd_hbm, o_ref, wu_buf, wd_buf, sem, *, layer):
    tf = MLP_FF_TILE
    n_ff = D_FF // tf
    assert n_ff % 2 == 0
    i = pl.program_id(0)
    last_tile = i == pl.num_programs(0) - 1

    def tile_copies(f, slot):
        cols = pl.ds(pl.multiple_of(f * tf, tf), tf)
        return (pltpu.make_async_copy(wu_hbm.at[layer, :, cols], wu_buf.at[slot], sem.at[0, slot]),
                pltpu.make_async_copy(wd_hbm.at[layer, cols, :], wd_buf.at[slot], sem.at[1, slot]))

    @pl.when(i == 0)
    def _():
        for c in tile_copies(0, 0):
            c.start()

    o_ref[...] = jnp.zeros(o_ref.shape, F32)

    def ff_slice(f, carry):
        slot = f % 2
        nxt = f + 1

        @pl.when(jnp.logical_or(nxt < n_ff, jnp.logical_not(last_tile)))
        def _():
            for c in tile_copies(nxt % n_ff, 1 - slot):
                c.start()

        for c in tile_copies(f, slot):
            c.wait()
        up = jnp.dot(xn_ref[...], wu_buf[slot].astype(BF16), preferred_element_type=F32)
        act = jnp.square(jnp.maximum(up, 0.0)).astype(BF16)
        o_ref[...] += jnp.dot(act, wd_buf[slot].astype(BF16), preferred_element_type=F32)
        return carry

    lax.fori_loop(0, n_ff, ff_slice, 0)


def _mlp(xn, w_up, w_down, layer):
    tm, tf = MLP_ROW_TILE, MLP_FF_TILE
    return pl.pallas_call(
        functools.partial(_mlp_kernel, layer=layer),
        grid=(N_TOK // tm,),
        in_specs=[pl.BlockSpec((tm, D_MODEL), lambda i: (i, 0)),
                  pl.BlockSpec(memory_space=pl.ANY),
                  pl.BlockSpec(memory_space=pl.ANY)],
        out_specs=pl.BlockSpec((tm, D_MODEL), lambda i: (i, 0)),
        out_shape=jax.ShapeDtypeStruct((N_TOK, D_MODEL), F32),
        scratch_shapes=[pltpu.VMEM((2, D_MODEL, tf), F32),
                        pltpu.VMEM((2, tf, D_MODEL), F32),
                        pltpu.SemaphoreType.DMA((2, 2))],
        compiler_params=_params("arbitrary"),
        name="mlp",
    )(xn, w_up, w_down)


def _gate_kernel(h1_ref, m_ref, pp_ref, ps_ref, wg_hbm, wp_hbm, g_ref, gf_ref, op_ref, os_ref,
                 wg_ref, stage, sem, wp_ref, *, layer, final):
    @pl.when(pl.program_id(0) == 0)
    def _():
        _load_weight_bf16(wg_hbm, layer, wg_ref, stage, sem)
        _load_weight_bf16(wp_hbm, layer, wp_ref, stage, sem)

    for rows in _row_parts():
        h2 = h1_ref[rows, :] + m_ref[rows, :]
        xn = _rms(h2, g_ref[...]).astype(BF16)
        gate = jax.nn.sigmoid(jnp.dot(xn, wg_ref[...], preferred_element_type=F32))
        emb = jnp.dot(_pick_rows(pp_ref, ps_ref, rows).astype(BF16), wp_ref[...],
                      preferred_element_type=F32)
        h3 = h2 + gate * emb
        op_ref[rows, :] = _rms(h3, gf_ref[...]) if final else h3

    @pl.when(pl.program_id(0) == 0)
    def _():
        os_ref[...] = op_ref[...]


def _gate(h1, mlp_out, p_prompt, p_sample, layer, w_gate, w_ple, g, g_final, final):
    row = _stacked_rows
    return pl.pallas_call(
        functools.partial(_gate_kernel, layer=layer, final=final),
        grid=(N_TOK // ROW_TILE,),
        in_specs=[row(D_MODEL), row(D_MODEL),
                  _prompt_rows(PLE_DIM, layer, 0), _sample_rows(PLE_DIM, layer),
                  _hbm_layer(w_gate), _hbm_layer(w_ple),
                  _resident_layer(g.shape, layer), _resident(g_final.shape)],
        out_specs=(_prompt_rows(D_MODEL), _sample_rows(D_MODEL, single_buffer=False)),
        out_shape=(jax.ShapeDtypeStruct((SEQ, D_MODEL), F32),
                   jax.ShapeDtypeStruct((N_SAMPLE, D_MODEL), F32)),
        scratch_shapes=_weight_scratch(w_gate) + [pltpu.VMEM(w_ple.shape[1:], BF16)],
        compiler_params=_params("arbitrary"),
        name="gate",
    )(h1, mlp_out, p_prompt, p_sample, w_gate, w_ple, g, g_final)


def _t5_bucket(rel):
    nb = N_BUCKETS // 2
    max_exact = nb // 2
    ret = jnp.where(rel > 0, nb, 0)
    n = jnp.abs(rel)
    nf = jnp.maximum(n, max_exact).astype(F32)
    large = max_exact + (jnp.log(nf / max_exact) / math.log(MAX_DISTANCE / max_exact)
                         * (nb - max_exact)).astype(jnp.int32)
    large = jnp.minimum(large, nb - 1)
    return ret + jnp.where(n < max_exact, n, large)


def _rotary_tables():
    half = RET_DIM // 2
    pos = jnp.concatenate([jnp.arange(SEQ, dtype=jnp.int32),
                           jnp.tile(PAST_LEN + jnp.arange(DEC_SEQ, dtype=jnp.int32), DEC_BATCH)])
    inv = ROPE_BASE ** (-jnp.arange(half, dtype=F32) / half)
    ang = pos.astype(F32)[:, None] * inv[None, :]
    cos, sin = jnp.cos(ang), jnp.sin(ang)
    return jnp.concatenate([cos, cos], axis=-1), jnp.concatenate([-sin, sin], axis=-1)


def kernel(x_prompt, x_sample, cache_k, cache_v, state_ret, p_prompt, p_sample, rel_bias,
           norm_mix, w_in, attn_sink, ret_gain, w_out, norm_mlp, w_up, w_down, norm_ple,
           w_gate, w_ple, norm_final):
    q_idx = jnp.arange(Q_BLOCK, dtype=jnp.int32)[:, None]
    s_idx = jnp.arange(K_BLOCK, dtype=jnp.int32)[None, :]
    bucket_p = _t5_bucket(s_idx - WINDOW - q_idx)
    bucket_s = _t5_bucket(jnp.arange(WINDOW + DEC_SEQ, dtype=jnp.int32)[None, :] - WINDOW
                          - jnp.arange(DEC_SEQ, dtype=jnp.int32)[:, None])
    bias_p, bias_s = _rel_bias(rel_bias, bucket_p, bucket_s)
    back = q_idx // CHUNK + WINDOW // CHUNK - s_idx // CHUNK
    seen = jnp.logical_and(back >= 0, back <= WINDOW // CHUNK)
    seen_first = jnp.logical_and(seen, s_idx >= WINDOW)
    bias_p = _merge_sides(_pair_layout(jnp.stack([jnp.where(seen, bias_p, -jnp.inf),
                                                  jnp.where(seen_first, bias_p, -jnp.inf)])))
    bias_s = jnp.pad(bias_s, ((0, 0), (0, 0), (0, SAMPLE_KEYS - WINDOW - DEC_SEQ)),
                     constant_values=-jnp.inf)
    bias_s = _merge_sides(_pair_layout(bias_s))

    cos2, sin2 = _rotary_tables()
    consts_p = _retention_consts(RET_TILE)
    consts_s = _retention_consts(DEC_SEQ)

    norm_mix, norm_mlp, norm_ple = (g.reshape(DEPTH, 1, D_MODEL)
                                    for g in (norm_mix, norm_mlp, norm_ple))
    hp = x_prompt.reshape(SEQ, D_MODEL)
    hs = x_sample.reshape(N_SAMPLE, D_MODEL)
    p_sample = p_sample.reshape(DEPTH, N_SAMPLE, PLE_DIM)
    k_p, v_p, s_p, k_s, v_s, s_s = [], [], [], [], [], []
    for l in range(DEPTH):
        sink = attn_sink[l].astype(F32).reshape(N_HEADS, 1, 1)
        sink_p = _pair_layout(jnp.broadcast_to(sink, (N_HEADS, Q_BLOCK, PAIR)))
        sink_s = _pair_layout(jnp.broadcast_to(sink, (N_HEADS, DEC_SEQ, PAIR)))
        gain = ret_gain[l].reshape(1, RET_WIDTH)

        qa, k, v, qr, kr, vr, gr = _in_proj(hp, hs, norm_mix, w_in, l, cos2, sin2)
        oa_p = _attn_prompt(qa, k, v, bias_p, sink_p)
        oa_s = _attn_sample(qa, k, v, cache_k[l].reshape(DEC_BATCH, WINDOW, KV_WIDTH),
                            cache_v[l].reshape(DEC_BATCH, WINDOW, KV_WIDTH), bias_s, sink_s)
        or_p, st_p = _ret_prompt(qr, kr, vr, gr, consts_p, gain)
        or_s, st_s = _ret_sample(qr, kr, vr, gr, state_ret, l, consts_s, gain)

        h1, xn = _out_proj(oa_p, oa_s, or_p, or_s, hp, hs, w_out, norm_mlp, l)
        mlp_out = _mlp(xn, w_up, w_down, l)
        hp, hs = _gate(h1, mlp_out, p_prompt, p_sample, l, w_gate, w_ple, norm_ple,
                       norm_final.reshape(1, D_MODEL), final=(l == DEPTH - 1))

        k_p.append(k[SEQ - WINDOW:SEQ].reshape(1, WINDOW, N_KV, HEAD_DIM))
        v_p.append(v[SEQ - WINDOW:SEQ].reshape(1, WINDOW, N_KV, HEAD_DIM))
        s_p.append(st_p[None])
        k_s.append(k[SEQ:].reshape(DEC_BATCH, DEC_SEQ, N_KV, HEAD_DIM))
        v_s.append(v[SEQ:].reshape(DEC_BATCH, DEC_SEQ, N_KV, HEAD_DIM))
        s_s.append(st_s)

    return (hp.reshape(1, SEQ, D_MODEL), hs.reshape(DEC_BATCH, DEC_SEQ, D_MODEL),
            jnp.stack(k_p), jnp.stack(v_p), jnp.stack(s_p),
            jnp.stack(k_s), jnp.stack(v_s), jnp.stack(s_s))
```

```python
import functools
import math

import jax
import jax.numpy as jnp
from jax import lax
from jax.experimental import pallas as pl
from jax.experimental.pallas import tpu as pltpu

F32 = jnp.float32
BF16 = jnp.bfloat16

D_MODEL = 2048
SEQ = 8192
DEPTH = 2
DEC_BATCH = 32
DEC_SEQ = 16
PAST_LEN = 4096
N_SAMPLE = DEC_BATCH * DEC_SEQ
N_TOK = SEQ + N_SAMPLE

CHUNK = 64
WINDOW = 128
BAND = WINDOW + CHUNK
HEAD_DIM = 64
N_HEADS = 16
N_KV = 4
GROUP = N_HEADS // N_KV
ATT_WIDTH = N_HEADS * HEAD_DIM
KV_WIDTH = N_KV * HEAD_DIM
RET_DIM = 128
RET_HEADS = 8
RET_WIDTH = RET_HEADS * RET_DIM
IN_WIDTH = ATT_WIDTH + 2 * KV_WIDTH + 4 * RET_WIDTH
D_FF = 4 * D_MODEL
PLE_DIM = 256
N_BUCKETS = 32
MAX_DISTANCE = 128
ROPE_BASE = 10000.0
EPS = 1e-6
ATT_SCALE = HEAD_DIM ** -0.5
LOG2E = 1.0 / math.log(2.0)
RET_SCALE = RET_DIM ** -0.5

ROW_TILE = 512
ROW_PARTS = 2
MLP_ROW_TILE = 1088
MLP_FF_TILE = 512
ATT_TILE = 512
Q_BLOCK = 2 * CHUNK
K_BLOCK = WINDOW + Q_BLOCK
RET_TILE = 256
SAMPLE_BLOCK = 8
SAMPLE_KEYS = 256
RET_SAMPLE_BLOCK = 4
VMEM_LIMIT = 58 * 1024 * 1024

NT_DIMS = (((1,), (1,)), ((), ()))
TN_DIMS = (((0,), (0,)), ((), ()))


def _params(*sem):
    return pltpu.CompilerParams(dimension_semantics=sem, vmem_limit_bytes=VMEM_LIMIT)


def _resident(shape):
    nd = len(shape)
    return pl.BlockSpec(shape, lambda *_: (0,) * nd, pipeline_mode=pl.Buffered(1))


def _resident_layer(shape, layer):
    nd = len(shape) - 1
    return pl.BlockSpec((None,) + tuple(shape[1:]), lambda *_: (layer,) + (0,) * nd,
                        pipeline_mode=pl.Buffered(1))


def _rms(x, g):
    ms = jnp.mean(x * x, axis=-1, keepdims=True)
    return x * lax.rsqrt(ms + EPS) * g


N_PROMPT_TILES = SEQ // ROW_TILE
assert N_SAMPLE == ROW_TILE


def _stacked_rows(width):
    return pl.BlockSpec((ROW_TILE, width),
                        lambda i: ((i + N_PROMPT_TILES) % (N_PROMPT_TILES + 1), 0))


def _prompt_rows(width, *lead):
    none = (None,) * len(lead)
    return pl.BlockSpec(none + (ROW_TILE, width), lambda i: lead + (jnp.maximum(i - 1, 0), 0))


def _sample_rows(width, *lead):
    none = (None,) * len(lead)
    return pl.BlockSpec(none + (ROW_TILE, width), lambda i: lead + (0, 0))


def _row_parts():
    part = ROW_TILE // ROW_PARTS
    return [slice(r * part, (r + 1) * part) for r in range(ROW_PARTS)]


def _pick_rows(prompt_ref, sample_ref, rows=slice(None)):
    return jnp.where(pl.program_id(0) == 0, sample_ref[rows, :], prompt_ref[rows, :])


def _bias_kernel(table_ref, bp_ref, bs_ref, op_ref, os_ref):
    bp = bp_ref[...]
    bs = bs_ref[...]

    def per_head(h, c):
        ap = jnp.zeros(bp.shape, F32)
        as_ = jnp.zeros(bs.shape, F32)
        for b in range(N_BUCKETS):
            t = table_ref[b, h]
            ap = jnp.where(bp == b, t, ap)
            as_ = jnp.where(bs == b, t, as_)
        op_ref[h] = ap
        os_ref[h] = as_
        return c

    lax.fori_loop(0, N_HEADS, per_head, 0)


def _rel_bias(table, bucket_p, bucket_s):
    return pl.pallas_call(
        _bias_kernel,
        out_shape=(jax.ShapeDtypeStruct((N_HEADS,) + bucket_p.shape, F32),
                   jax.ShapeDtypeStruct((N_HEADS,) + bucket_s.shape, F32)),
        in_specs=[pl.BlockSpec(memory_space=pltpu.SMEM),
                  pl.BlockSpec(memory_space=pltpu.VMEM),
                  pl.BlockSpec(memory_space=pltpu.VMEM)],
        out_specs=(pl.BlockSpec(memory_space=pltpu.VMEM), pl.BlockSpec(memory_space=pltpu.VMEM)),
        name="rel_bias",
    )(table, bucket_p, bucket_s)


def _in_proj_kernel(hp_ref, hs_ref, g_ref, w_ref, cos_ref, sin_ref,
                    qa_ref, k_ref, v_ref, qr_ref, kr_ref, vr_ref, gr_ref):
    base = ATT_WIDTH + 2 * KV_WIDTH
    for rows in _row_parts():
        xn = _rms(_pick_rows(hp_ref, hs_ref, rows), g_ref[...]).astype(BF16)

        def proj(lo, width):
            return jnp.dot(xn, w_ref[:, lo:lo + width], preferred_element_type=F32)

        qa_ref[rows, :] = (proj(0, ATT_WIDTH) * (ATT_SCALE * LOG2E)).astype(BF16)
        k_ref[rows, :] = proj(ATT_WIDTH, KV_WIDTH)
        v_ref[rows, :] = proj(ATT_WIDTH + KV_WIDTH, KV_WIDTH)
        cos = cos_ref[rows, :]
        sin = sin_ref[rows, :]

        def rot(x):
            return x * cos + pltpu.roll(x, RET_DIM // 2, 1) * sin

        q = proj(base, RET_WIDTH)
        for h in range(RET_HEADS):
            sl = slice(h * RET_DIM, (h + 1) * RET_DIM)
            qr_ref[rows, sl] = rot(q[:, sl]).astype(BF16)
        k = proj(base + RET_WIDTH, RET_WIDTH)
        for h in range(RET_HEADS):
            sl = slice(h * RET_DIM, (h + 1) * RET_DIM)
            kr_ref[rows, sl] = (rot(k[:, sl]) * RET_SCALE).astype(BF16)
        vr_ref[rows, :] = proj(base + 2 * RET_WIDTH, RET_WIDTH).astype(BF16)
        gr_ref[rows, :] = proj(base + 3 * RET_WIDTH, RET_WIDTH).astype(BF16)


def _in_proj(hp, hs, g, w, layer, cos2, sin2):
    tm = ROW_TILE
    row = _stacked_rows
    out = lambda width, dt: jax.ShapeDtypeStruct((N_TOK, width), dt)
    return pl.pallas_call(
        _in_proj_kernel,
        grid=(N_TOK // tm,),
        in_specs=[_prompt_rows(D_MODEL), _sample_rows(D_MODEL),
                  _resident_layer(g.shape, layer), _resident(w.shape),
                  row(RET_DIM), row(RET_DIM)],
        out_specs=(row(ATT_WIDTH), row(KV_WIDTH), row(KV_WIDTH),
                   row(RET_WIDTH), row(RET_WIDTH), row(RET_WIDTH), row(RET_WIDTH)),
        out_shape=(out(ATT_WIDTH, BF16), out(KV_WIDTH, F32), out(KV_WIDTH, F32),
                   out(RET_WIDTH, BF16), out(RET_WIDTH, BF16), out(RET_WIDTH, BF16),
                   out(RET_WIDTH, BF16)),
        compiler_params=_params("parallel"),
        name="in_proj",
    )(hp, hs, g, w, cos2, sin2)


PAIR = 2 * HEAD_DIM


def _attend_pairs(qst, kcat, vext, bias, sinks):
    half = kcat.shape[0] // 2
    s = lax.dot_general(qst, kcat, NT_DIMS, preferred_element_type=F32) + bias
    ps, es = [], []
    for side in range(2):
        ss = s[:, side * half:(side + 1) * half]
        m = jnp.maximum(jnp.max(ss, axis=-1, keepdims=True), sinks[side])
        ps.append(jnp.exp2(ss - jnp.concatenate([m] * (half // PAIR), axis=1)))
        es.append(jnp.exp2(sinks[side] - m))
    p = jnp.concatenate(ps, axis=1).astype(BF16)
    r = jnp.dot(p, vext, preferred_element_type=F32)
    lo = lax.broadcasted_iota(jnp.int32, qst.shape, 1) < HEAD_DIM
    den = r[:, PAIR:] + jnp.where(lo, es[0], es[1])
    return r[:, :PAIR] / den


def _fill_padded(dst, rows, x, sum_lanes=False):
    lo = lax.broadcasted_iota(jnp.int32, (x.shape[0], PAIR), 1) < HEAD_DIM
    for pair in range(N_KV // 2):
        xp = x[:, pair * PAIR:(pair + 1) * PAIR]
        xr = pltpu.roll(xp, HEAD_DIM, 1)
        zero = jnp.zeros_like(xp)
        dst[2 * pair, 0, rows, 0:PAIR] = jnp.where(lo, xp, zero).astype(BF16)
        dst[2 * pair, 1, rows, 0:PAIR] = jnp.where(lo, zero, xr).astype(BF16)
        dst[2 * pair + 1, 0, rows, 0:PAIR] = jnp.where(lo, xr, zero).astype(BF16)
        dst[2 * pair + 1, 1, rows, 0:PAIR] = jnp.where(lo, zero, xp).astype(BF16)
    if sum_lanes:
        for kv in range(N_KV):
            dst[kv, 0, rows, PAIR:2 * PAIR] = jnp.where(lo, 1.0, 0.0).astype(BF16)
            dst[kv, 1, rows, PAIR:2 * PAIR] = jnp.where(lo, 0.0, 1.0).astype(BF16)


def _attn_prompt_kernel(q_ref, kc_ref, kp_ref, vc_ref, vp_ref, bias_ref, sink_ref, o_ref, kz, vz):
    i = pl.program_id(0)
    _fill_padded(kz, slice(0, WINDOW), kp_ref[...])
    _fill_padded(kz, slice(WINDOW, WINDOW + ATT_TILE), kc_ref[...])
    _fill_padded(vz, slice(0, WINDOW), vp_ref[...], sum_lanes=True)
    _fill_padded(vz, slice(WINDOW, WINDOW + ATT_TILE), vc_ref[...], sum_lanes=True)

    def block(j, c):
        r0 = pl.multiple_of(j * Q_BLOCK, Q_BLOCK)
        rows = pl.ds(r0, Q_BLOCK)
        band = pl.ds(r0, K_BLOCK)
        first = jnp.where(jnp.logical_and(i == 0, j == 0), 1, 0)
        for kv in range(N_KV):
            c0 = kv * GROUP * HEAD_DIM
            qst = jnp.concatenate([q_ref[rows, c0:c0 + 2 * HEAD_DIM],
                                   q_ref[rows, c0 + 2 * HEAD_DIM:c0 + 4 * HEAD_DIM]], axis=0)
            o = _attend_pairs(
                qst,
                jnp.concatenate([kz[kv, side, band, :] for side in range(2)], axis=0),
                jnp.concatenate([vz[kv, side, band, :] for side in range(2)], axis=0),
                bias_ref[first, kv], [sink_ref[kv, side] for side in range(2)])
            o_ref[rows, c0:c0 + 2 * HEAD_DIM] = o[0:Q_BLOCK].astype(BF16)
            o_ref[rows, c0 + 2 * HEAD_DIM:c0 + 4 * HEAD_DIM] = o[Q_BLOCK:].astype(BF16)
        return c

    lax.fori_loop(0, ATT_TILE // Q_BLOCK, block, 0)


def _attn_prompt(qa, k, v, bias, sink):
    t = ATT_TILE
    per = t // WINDOW
    cur = lambda width: pl.BlockSpec((t, width), lambda i: (i, 0))
    prev = pl.BlockSpec((WINDOW, KV_WIDTH), lambda i: (jnp.maximum(i * per - 1, 0), 0))
    return pl.pallas_call(
        _attn_prompt_kernel,
        grid=(SEQ // t,),
        in_specs=[cur(ATT_WIDTH), cur(KV_WIDTH), prev, cur(KV_WIDTH), prev,
                  _resident(bias.shape), _resident(sink.shape)],
        out_specs=cur(ATT_WIDTH),
        out_shape=jax.ShapeDtypeStruct((SEQ, ATT_WIDTH), BF16),
        scratch_shapes=[pltpu.VMEM((N_KV, 2, t + WINDOW, PAIR), BF16),
                        pltpu.VMEM((N_KV, 2, t + WINDOW, 2 * PAIR), BF16)],
        compiler_params=_params("parallel"),
        name="attn_prompt",
    )(qa, k, k, v, v, bias, sink)


def _pair_layout(x):
    lead = x.shape[:-3]
    rows, cols = x.shape[-2:]
    x = x.reshape(lead + (N_KV, 2, 2, rows, cols))
    x = jnp.swapaxes(x, -4, -3)
    return x.reshape(lead + (N_KV, 2, 2 * rows, cols))


def _merge_sides(x):
    rows, cols = x.shape[-2:]
    return jnp.swapaxes(x, -3, -2).reshape(x.shape[:-3] + (rows, 2 * cols))


def _attn_sample_kernel(q_ref, kn_ref, vn_ref, ck_ref, cv_ref, bias_ref, sink_ref, o_ref, kz, vz):
    keys = WINDOW + DEC_SEQ
    kz[:, :, :, keys:, :] = jnp.zeros((SAMPLE_BLOCK, N_KV, 2, SAMPLE_KEYS - keys, PAIR), BF16)
    vz[:, :, :, keys:, :] = jnp.zeros((SAMPLE_BLOCK, N_KV, 2, SAMPLE_KEYS - keys, 2 * PAIR), BF16)
    for b in range(SAMPLE_BLOCK):
        new = slice(b * DEC_SEQ, (b + 1) * DEC_SEQ)
        _fill_padded(kz.at[b], slice(0, WINDOW), ck_ref[b])
        _fill_padded(kz.at[b], slice(WINDOW, keys), kn_ref[new, :])
        _fill_padded(vz.at[b], slice(0, WINDOW), cv_ref[b], sum_lanes=True)
        _fill_padded(vz.at[b], slice(WINDOW, keys), vn_ref[new, :], sum_lanes=True)
    for b in range(SAMPLE_BLOCK):
        rows = slice(b * DEC_SEQ, (b + 1) * DEC_SEQ)
        for kv in range(N_KV):
            c0 = kv * GROUP * HEAD_DIM
            qst = jnp.concatenate([q_ref[rows, c0:c0 + 2 * HEAD_DIM],
                                   q_ref[rows, c0 + 2 * HEAD_DIM:c0 + 4 * HEAD_DIM]], axis=0)
            o = _attend_pairs(qst, kz[b, kv].reshape(2 * SAMPLE_KEYS, PAIR),
                              vz[b, kv].reshape(2 * SAMPLE_KEYS, 2 * PAIR),
                              bias_ref[kv], [sink_ref[kv, side] for side in range(2)])
            o_ref[rows, c0:c0 + 2 * HEAD_DIM] = o[0:DEC_SEQ].astype(BF16)
            o_ref[rows, c0 + 2 * HEAD_DIM:c0 + 4 * HEAD_DIM] = o[DEC_SEQ:].astype(BF16)


def _attn_sample(qa, k, v, cache_k, cache_v, bias, sink):
    rows = SAMPLE_BLOCK * DEC_SEQ
    first = SEQ // rows
    new = lambda width: pl.BlockSpec((rows, width), lambda i: (first + i, 0))
    cache = pl.BlockSpec((SAMPLE_BLOCK, WINDOW, KV_WIDTH), lambda i: (i, 0, 0))
    return pl.pallas_call(
        _attn_sample_kernel,
        grid=(DEC_BATCH // SAMPLE_BLOCK,),
        in_specs=[new(ATT_WIDTH), new(KV_WIDTH), new(KV_WIDTH), cache, cache,
                  _resident(bias.shape), _resident(sink.shape)],
        out_specs=pl.BlockSpec((rows, ATT_WIDTH), lambda i: (i, 0)),
        out_shape=jax.ShapeDtypeStruct((N_SAMPLE, ATT_WIDTH), BF16),
        scratch_shapes=[pltpu.VMEM((SAMPLE_BLOCK, N_KV, 2, SAMPLE_KEYS, PAIR), BF16),
                        pltpu.VMEM((SAMPLE_BLOCK, N_KV, 2, SAMPLE_KEYS, 2 * PAIR), BF16)],
        compiler_params=_params("parallel"),
        name="attn_sample",
    )(qa, k, v, cache_k, cache_v, bias, sink)


def _retention_head(q, k, v, g, state, decay, zeta, xi, g_chunk, gain):
    scores = lax.dot_general(q, k, NT_DIMS, preferred_element_type=F32) * decay
    if q.shape[0] % RET_DIM == 0:
        lhs = jnp.concatenate([(q.astype(F32) * xi).astype(BF16), scores.astype(BF16)], axis=1)
        rhs = jnp.concatenate([state.astype(BF16), v], axis=0)
        o = jnp.dot(lhs, rhs, preferred_element_type=F32)
    else:
        o = jnp.dot(scores.astype(BF16), v, preferred_element_type=F32)
        o = o + jnp.dot(q, state.astype(BF16), preferred_element_type=F32) * xi
    kz = (k.astype(F32) * zeta).astype(BF16)
    upd = lax.dot_general(kz, v, TN_DIMS, preferred_element_type=F32)
    new_state = g_chunk * state + upd
    o = o * lax.rsqrt(jnp.mean(o * o, axis=-1, keepdims=True) + EPS) * gain
    gf = g.astype(F32)
    return gf * jax.nn.sigmoid(gf) * o, new_state


def _ret_prompt_kernel(gc_ref, q_ref, k_ref, v_ref, g_ref, decay_ref, zeta_ref, xi_ref, gain_ref,
                       o_ref, st_ref, state):
    i = pl.program_id(0)

    @pl.when(i == 0)
    def _():
        state[...] = jnp.zeros(state.shape, F32)

    for h in range(RET_HEADS):
        sl = slice(h * RET_DIM, (h + 1) * RET_DIM)
        o, s_new = _retention_head(q_ref[:, sl], k_ref[:, sl], v_ref[:, sl], g_ref[:, sl],
                                   state[h], decay_ref[h], zeta_ref[h], xi_ref[h], gc_ref[h],
                                   gain_ref[:, sl])
        state[h] = s_new
        o_ref[:, sl] = o.astype(BF16)

    @pl.when(i == pl.num_programs(0) - 1)
    def _():
        st_ref[...] = state[...]


def _ret_prompt(qr, kr, vr, gr, consts, gain):
    g_chunk, decay, zeta, xi = consts
    c = RET_TILE
    row = pl.BlockSpec((c, RET_WIDTH), lambda i: (i, 0))
    st_shape = (RET_HEADS, RET_DIM, RET_DIM)
    return pl.pallas_call(
        _ret_prompt_kernel,
        grid=(SEQ // c,),
        in_specs=[pl.BlockSpec(memory_space=pltpu.SMEM), row, row, row, row,
                  _resident(decay.shape), _resident(zeta.shape), _resident(xi.shape),
                  _resident(gain.shape)],
        out_specs=(row, pl.BlockSpec(st_shape, lambda i: (0, 0, 0))),
        out_shape=(jax.ShapeDtypeStruct((SEQ, RET_WIDTH), BF16),
                   jax.ShapeDtypeStruct(st_shape, F32)),
        scratch_shapes=[pltpu.VMEM(st_shape, F32)],
        compiler_params=_params("arbitrary"),
        name="ret_prompt",
    )(g_chunk, qr, kr, vr, gr, decay, zeta, xi, gain)


def _ret_sample_kernel(gc_ref, q_ref, k_ref, v_ref, g_ref, s_ref, decay_ref, zeta_ref, xi_ref,
                       gain_ref, o_ref, st_ref):
    for b in range(RET_SAMPLE_BLOCK):
        rows = slice(b * DEC_SEQ, (b + 1) * DEC_SEQ)
        for h in range(RET_HEADS):
            sl = slice(h * RET_DIM, (h + 1) * RET_DIM)
            o, s_new = _retention_head(q_ref[rows, sl], k_ref[rows, sl], v_ref[rows, sl],
                                       g_ref[rows, sl], s_ref[b, h], decay_ref[h], zeta_ref[h],
                                       xi_ref[h], gc_ref[h], gain_ref[:, sl])
            st_ref[b, h] = s_new
            o_ref[rows, sl] = o.astype(BF16)


def _ret_sample(qr, kr, vr, gr, state, layer, consts, gain):
    g_chunk, decay, zeta, xi = consts
    nb = RET_SAMPLE_BLOCK
    first = SEQ // (nb * DEC_SEQ)
    row = pl.BlockSpec((nb * DEC_SEQ, RET_WIDTH), lambda i: (first + i, 0))
    st_block = (nb, RET_HEADS, RET_DIM, RET_DIM)
    return pl.pallas_call(
        _ret_sample_kernel,
        grid=(DEC_BATCH // nb,),
        in_specs=[pl.BlockSpec(memory_space=pltpu.SMEM), row, row, row, row,
                  pl.BlockSpec((None,) + st_block, lambda i: (layer, i, 0, 0, 0)),
                  _resident(decay.shape), _resident(zeta.shape), _resident(xi.shape),
                  _resident(gain.shape)],
        out_specs=(pl.BlockSpec((nb * DEC_SEQ, RET_WIDTH), lambda i: (i, 0)),
                   pl.BlockSpec(st_block, lambda i: (i, 0, 0, 0))),
        out_shape=(jax.ShapeDtypeStruct((N_SAMPLE, RET_WIDTH), BF16),
                   jax.ShapeDtypeStruct(state.shape[1:], F32)),
        compiler_params=_params("parallel"),
        name="ret_sample",
    )(g_chunk, qr, kr, vr, gr, state, decay, zeta, xi, gain)


def _retention_consts(chunk):
    lg = jnp.log(1.0 - 2.0 ** (-5.0 - jnp.arange(RET_HEADS, dtype=F32)))
    idx = jnp.arange(chunk, dtype=F32)
    diff = idx[:, None] - idx[None, :]
    decay = jnp.where(diff >= 0, jnp.exp(lg[:, None, None] * jnp.maximum(diff, 0.0)), 0.0)
    zeta = jnp.exp(lg[:, None] * (chunk - 1 - idx))[:, :, None]
    xi = jnp.exp(lg[:, None] * (idx + 1))[:, :, None]
    return jnp.exp(lg * chunk), decay, zeta, xi


def _out_proj_kernel(oap_ref, oas_ref, orp_ref, ors_ref, hp_ref, hs_ref, w_ref, g_ref,
                     h1_ref, xn_ref):
    for rows in _row_parts():
        h1 = (_pick_rows(hp_ref, hs_ref, rows)
              + jnp.dot(_pick_rows(oap_ref, oas_ref, rows), w_ref[0:ATT_WIDTH, :],
                        preferred_element_type=F32)
              + jnp.dot(_pick_rows(orp_ref, ors_ref, rows), w_ref[ATT_WIDTH:, :],
                        preferred_element_type=F32))
        h1_ref[rows, :] = h1
        xn_ref[rows, :] = _rms(h1, g_ref[...]).astype(BF16)


def _out_proj(oa_p, oa_s, or_p, or_s, hp, hs, w, g, layer):
    row = _stacked_rows
    return pl.pallas_call(
        _out_proj_kernel,
        grid=(N_TOK // ROW_TILE,),
        in_specs=[_prompt_rows(ATT_WIDTH), _sample_rows(ATT_WIDTH),
                  _prompt_rows(RET_WIDTH), _sample_rows(RET_WIDTH),
                  _prompt_rows(D_MODEL), _sample_rows(D_MODEL),
                  _resident(w.shape), _resident_layer(g.shape, layer)],
        out_specs=(row(D_MODEL), row(D_MODEL)),
        out_shape=(jax.ShapeDtypeStruct((N_TOK, D_MODEL), F32),
                   jax.ShapeDtypeStruct((N_TOK, D_MODEL), BF16)),
        compiler_params=_params("parallel"),
        name="out_proj",
    )(oa_p, oa_s, or_p, or_s, hp, hs, w, g)


def _mlp_kernel(xn_ref, wu_ref, wd_ref, *refs):
    n_extra = (len(refs) - 1) // 2
    o_ref = refs[n_extra]

    @pl.when(pl.program_id(1) == 0)
    def _():
        o_ref[...] = jnp.zeros(o_ref.shape, F32)

    up = jnp.dot(xn_ref[...], wu_ref[...].astype(BF16), preferred_element_type=F32)
    act = jnp.square(jnp.maximum(up, 0.0)).astype(BF16)
    o_ref[...] += jnp.dot(act, wd_ref[...].astype(BF16), preferred_element_type=F32)
    for src, dst in zip(refs[:n_extra], refs[n_extra + 1:]):
        dst[...] = src[...].astype(BF16)


def _mlp(xn, w_up, w_down, layer, cast_jobs=()):
    tm, tf = MLP_ROW_TILE, MLP_FF_TILE
    n_ff = D_FF // tf
    steps = (N_TOK // tm) * n_ff
    slab_in, slab_out, slab_shape = [], [], []
    for w, lyr in cast_jobs:
        _, k, n = w.shape
        rows = k // steps
        assert rows * steps == k and rows % 16 == 0
        slab_in.append(pl.BlockSpec((None, rows, n), lambda i, f, lyr=lyr: (lyr, i * n_ff + f, 0)))
        slab_out.append(pl.BlockSpec((rows, n), lambda i, f: (i * n_ff + f, 0)))
        slab_shape.append(jax.ShapeDtypeStruct((k, n), BF16))
    return pl.pallas_call(
        _mlp_kernel,
        grid=(N_TOK // tm, n_ff),
        in_specs=[pl.BlockSpec((tm, D_MODEL), lambda i, f: (i, 0)),
                  pl.BlockSpec((None, D_MODEL, tf), lambda i, f: (layer, 0, f)),
                  pl.BlockSpec((None, tf, D_MODEL), lambda i, f: (layer, f, 0))] + slab_in,
        out_specs=[pl.BlockSpec((tm, D_MODEL), lambda i, f: (i, 0))] + slab_out,
        out_shape=[jax.ShapeDtypeStruct((N_TOK, D_MODEL), F32)] + slab_shape,
        compiler_params=_params("arbitrary", "arbitrary"),
        name="mlp",
    )(xn, w_up, w_down, *[w for w, _ in cast_jobs])


def _gate_kernel(h1_ref, m_ref, pp_ref, ps_ref, wg_ref, wp_ref, g_ref, gf_ref, op_ref, os_ref,
                 *, final):
    for rows in _row_parts():
        h2 = h1_ref[rows, :] + m_ref[rows, :]
        xn = _rms(h2, g_ref[...]).astype(BF16)
        gate = jax.nn.sigmoid(jnp.dot(xn, wg_ref[...], preferred_element_type=F32))
        emb = jnp.dot(_pick_rows(pp_ref, ps_ref, rows).astype(BF16), wp_ref[...],
                      preferred_element_type=F32)
        h3 = h2 + gate * emb
        op_ref[rows, :] = _rms(h3, gf_ref[...]) if final else h3

    @pl.when(pl.program_id(0) == 0)
    def _():
        os_ref[...] = op_ref[...]


def _gate(h1, mlp_out, p_prompt, p_sample, layer, w_gate, w_ple, g, g_final, final):
    row = _stacked_rows
    return pl.pallas_call(
        functools.partial(_gate_kernel, final=final),
        grid=(N_TOK // ROW_TILE,),
        in_specs=[row(D_MODEL), row(D_MODEL),
                  _prompt_rows(PLE_DIM, layer, 0), _sample_rows(PLE_DIM, layer),
                  _resident(w_gate.shape), _resident_layer(w_ple.shape, layer),
                  _resident_layer(g.shape, layer), _resident(g_final.shape)],
        out_specs=(_prompt_rows(D_MODEL), _sample_rows(D_MODEL)),
        out_shape=(jax.ShapeDtypeStruct((SEQ, D_MODEL), F32),
                   jax.ShapeDtypeStruct((N_SAMPLE, D_MODEL), F32)),
        compiler_params=_params("arbitrary"),
        name="gate",
    )(h1, mlp_out, p_prompt, p_sample, w_gate, w_ple, g, g_final)


def _t5_bucket(rel):
    nb = N_BUCKETS // 2
    max_exact = nb // 2
    ret = jnp.where(rel > 0, nb, 0)
    n = jnp.abs(rel)
    nf = jnp.maximum(n, max_exact).astype(F32)
    large = max_exact + (jnp.log(nf / max_exact) / math.log(MAX_DISTANCE / max_exact)
                         * (nb - max_exact)).astype(jnp.int32)
    large = jnp.minimum(large, nb - 1)
    return ret + jnp.where(n < max_exact, n, large)


def _rotary_tables():
    half = RET_DIM // 2
    pos = jnp.concatenate([jnp.arange(SEQ, dtype=jnp.int32),
                           jnp.tile(PAST_LEN + jnp.arange(DEC_SEQ, dtype=jnp.int32), DEC_BATCH)])
    inv = ROPE_BASE ** (-jnp.arange(half, dtype=F32) / half)
    ang = pos.astype(F32)[:, None] * inv[None, :]
    cos, sin = jnp.cos(ang), jnp.sin(ang)
    return jnp.concatenate([cos, cos], axis=-1), jnp.concatenate([-sin, sin], axis=-1)


def kernel(x_prompt, x_sample, cache_k, cache_v, state_ret, p_prompt, p_sample, rel_bias,
           norm_mix, w_in, attn_sink, ret_gain, w_out, norm_mlp, w_up, w_down, norm_ple,
           w_gate, w_ple, norm_final):
    q_idx = jnp.arange(Q_BLOCK, dtype=jnp.int32)[:, None]
    s_idx = jnp.arange(K_BLOCK, dtype=jnp.int32)[None, :]
    bucket_p = _t5_bucket(s_idx - WINDOW - q_idx)
    bucket_s = _t5_bucket(jnp.arange(WINDOW + DEC_SEQ, dtype=jnp.int32)[None, :] - WINDOW
                          - jnp.arange(DEC_SEQ, dtype=jnp.int32)[:, None])
    bias_p, bias_s = _rel_bias(rel_bias, bucket_p, bucket_s)
    bias_p, bias_s = bias_p * LOG2E, bias_s * LOG2E
    back = q_idx // CHUNK + WINDOW // CHUNK - s_idx // CHUNK
    seen = jnp.logical_and(back >= 0, back <= WINDOW // CHUNK)
    seen_first = jnp.logical_and(seen, s_idx >= WINDOW)
    bias_p = _merge_sides(_pair_layout(jnp.stack([jnp.where(seen, bias_p, -jnp.inf),
                                                  jnp.where(seen_first, bias_p, -jnp.inf)])))
    bias_s = jnp.pad(bias_s, ((0, 0), (0, 0), (0, SAMPLE_KEYS - WINDOW - DEC_SEQ)),
                     constant_values=-jnp.inf)
    bias_s = _merge_sides(_pair_layout(bias_s))

    cos2, sin2 = _rotary_tables()
    consts_p = _retention_consts(RET_TILE)
    consts_s = _retention_consts(DEC_SEQ)

    w_in_bf = {0: w_in[0].astype(BF16)}
    w_out_bf = {0: w_out[0].astype(BF16)}
    w_gate_bf = {}
    w_ple = w_ple.astype(BF16)
    norm_mix, norm_mlp, norm_ple = (g.reshape(DEPTH, 1, D_MODEL)
                                    for g in (norm_mix, norm_mlp, norm_ple))
    hp = x_prompt.reshape(SEQ, D_MODEL)
    hs = x_sample.reshape(N_SAMPLE, D_MODEL)
    p_sample = p_sample.reshape(DEPTH, N_SAMPLE, PLE_DIM)
    k_p, v_p, s_p, k_s, v_s, s_s = [], [], [], [], [], []
    for l in range(DEPTH):
        sink = attn_sink[l].astype(F32).reshape(N_HEADS, 1, 1) * LOG2E
        sink_p = _pair_layout(jnp.broadcast_to(sink, (N_HEADS, Q_BLOCK, PAIR)))
        sink_s = _pair_layout(jnp.broadcast_to(sink, (N_HEADS, DEC_SEQ, PAIR)))
        gain = ret_gain[l].reshape(1, RET_WIDTH)

        qa, k, v, qr, kr, vr, gr = _in_proj(hp, hs, norm_mix, w_in_bf[l], l, cos2, sin2)
        oa_p = _attn_prompt(qa, k, v, bias_p, sink_p)
        oa_s = _attn_sample(qa, k, v, cache_k[l].reshape(DEC_BATCH, WINDOW, KV_WIDTH),
                            cache_v[l].reshape(DEC_BATCH, WINDOW, KV_WIDTH), bias_s, sink_s)
        or_p, st_p = _ret_prompt(qr, kr, vr, gr, consts_p, gain)
        or_s, st_s = _ret_sample(qr, kr, vr, gr, state_ret, l, consts_s, gain)

        h1, xn = _out_proj(oa_p, oa_s, or_p, or_s, hp, hs, w_out_bf[l], norm_mlp, l)
        jobs = [(w_gate, l)]
        if l + 1 < DEPTH:
            jobs += [(w_in, l + 1), (w_out, l + 1)]
        mlp_out, *cast = _mlp(xn, w_up, w_down, l, jobs)
        w_gate_bf[l] = cast[0]
        if l + 1 < DEPTH:
            w_in_bf[l + 1], w_out_bf[l + 1] = cast[1], cast[2]
        hp, hs = _gate(h1, mlp_out, p_prompt, p_sample, l, w_gate_bf[l], w_ple, norm_ple,
                       norm_final.reshape(1, D_MODEL), final=(l == DEPTH - 1))

        k_p.append(k[SEQ - WINDOW:SEQ].reshape(1, WINDOW, N_KV, HEAD_DIM))
        v_p.append(v[SEQ - WINDOW:SEQ].reshape(1, WINDOW, N_KV, HEAD_DIM))
        s_p.append(st_p[None])
        k_s.append(k[SEQ:].reshape(DEC_BATCH, DEC_SEQ, N_KV, HEAD_DIM))
        v_s.append(v[SEQ:].reshape(DEC_BATCH, DEC_SEQ, N_KV, HEAD_DIM))
        s_s.append(st_s)

    return (hp.reshape(1, SEQ, D_MODEL), hs.reshape(DEC_BATCH, DEC_SEQ, D_MODEL),
            jnp.stack(k_p), jnp.stack(v_p), jnp.stack(s_p),
            jnp.stack(k_s), jnp.stack(v_s), jnp.stack(s_s))
```

```python
import functools
import math

import jax
import jax.numpy as jnp
from jax import lax
from jax.experimental import pallas as pl
from jax.experimental.pallas import tpu as pltpu

F32 = jnp.float32
BF16 = jnp.bfloat16

D_MODEL = 2048
SEQ = 8192
DEPTH = 2
DEC_BATCH = 32
DEC_SEQ = 16
PAST_LEN = 4096
N_SAMPLE = DEC_BATCH * DEC_SEQ
N_TOK = SEQ + N_SAMPLE

CHUNK = 64
WINDOW = 128
BAND = WINDOW + CHUNK
HEAD_DIM = 64
N_HEADS = 16
N_KV = 4
GROUP = N_HEADS // N_KV
ATT_WIDTH = N_HEADS * HEAD_DIM
KV_WIDTH = N_KV * HEAD_DIM
RET_DIM = 128
RET_HEADS = 8
RET_WIDTH = RET_HEADS * RET_DIM
IN_WIDTH = ATT_WIDTH + 2 * KV_WIDTH + 4 * RET_WIDTH
D_FF = 4 * D_MODEL
PLE_DIM = 256
N_BUCKETS = 32
MAX_DISTANCE = 128
ROPE_BASE = 10000.0
EPS = 1e-6
ATT_SCALE = HEAD_DIM ** -0.5
LOG2E = 1.0 / math.log(2.0)
RET_SCALE = RET_DIM ** -0.5

ROW_TILE = 512
ROW_PARTS = 2
MLP_ROW_TILE = 1088
MLP_FF_TILE = 512
ATT_TILE = 512
Q_BLOCK = 2 * CHUNK
K_BLOCK = WINDOW + Q_BLOCK
RET_TILE = 256
SAMPLE_BLOCK = 8
SAMPLE_KEYS = 256
RET_SAMPLE_BLOCK = 4
VMEM_LIMIT = 58 * 1024 * 1024

NT_DIMS = (((1,), (1,)), ((), ()))
TN_DIMS = (((0,), (0,)), ((), ()))


def _params(*sem):
    return pltpu.CompilerParams(dimension_semantics=sem, vmem_limit_bytes=VMEM_LIMIT)


def _resident(shape):
    nd = len(shape)
    return pl.BlockSpec(shape, lambda *_: (0,) * nd, pipeline_mode=pl.Buffered(1))


def _resident_layer(shape, layer):
    nd = len(shape) - 1
    return pl.BlockSpec((None,) + tuple(shape[1:]), lambda *_: (layer,) + (0,) * nd,
                        pipeline_mode=pl.Buffered(1))


def _rms(x, g):
    ms = jnp.mean(x * x, axis=-1, keepdims=True)
    return x * lax.rsqrt(ms + EPS) * g


N_PROMPT_TILES = SEQ // ROW_TILE
assert N_SAMPLE == ROW_TILE


def _stacked_rows(width):
    return pl.BlockSpec((ROW_TILE, width),
                        lambda i: ((i + N_PROMPT_TILES) % (N_PROMPT_TILES + 1), 0))


def _prompt_rows(width, *lead):
    none = (None,) * len(lead)
    return pl.BlockSpec(none + (ROW_TILE, width), lambda i: lead + (jnp.maximum(i - 1, 0), 0))


def _sample_rows(width, *lead):
    none = (None,) * len(lead)
    return pl.BlockSpec(none + (ROW_TILE, width), lambda i: lead + (0, 0))


def _row_parts():
    part = ROW_TILE // ROW_PARTS
    return [slice(r * part, (r + 1) * part) for r in range(ROW_PARTS)]


def _pick_rows(prompt_ref, sample_ref, rows=slice(None)):
    return jnp.where(pl.program_id(0) == 0, sample_ref[rows, :], prompt_ref[rows, :])


def _bias_kernel(table_ref, bp_ref, bs_ref, op_ref, os_ref):
    bp = bp_ref[...]
    bs = bs_ref[...]

    def per_head(h, c):
        ap = jnp.zeros(bp.shape, F32)
        as_ = jnp.zeros(bs.shape, F32)
        for b in range(N_BUCKETS):
            t = table_ref[b, h]
            ap = jnp.where(bp == b, t, ap)
            as_ = jnp.where(bs == b, t, as_)
        op_ref[h] = ap
        os_ref[h] = as_
        return c

    lax.fori_loop(0, N_HEADS, per_head, 0)


def _rel_bias(table, bucket_p, bucket_s):
    return pl.pallas_call(
        _bias_kernel,
        out_shape=(jax.ShapeDtypeStruct((N_HEADS,) + bucket_p.shape, F32),
                   jax.ShapeDtypeStruct((N_HEADS,) + bucket_s.shape, F32)),
        in_specs=[pl.BlockSpec(memory_space=pltpu.SMEM),
                  pl.BlockSpec(memory_space=pltpu.VMEM),
                  pl.BlockSpec(memory_space=pltpu.VMEM)],
        out_specs=(pl.BlockSpec(memory_space=pltpu.VMEM), pl.BlockSpec(memory_space=pltpu.VMEM)),
        name="rel_bias",
    )(table, bucket_p, bucket_s)


def _in_proj_kernel(hp_ref, hs_ref, g_ref, w_ref, cos_ref, sin_ref,
                    qa_ref, k_ref, v_ref, qr_ref, kr_ref, vr_ref, gr_ref):
    base = ATT_WIDTH + 2 * KV_WIDTH
    for rows in _row_parts():
        xn = _rms(_pick_rows(hp_ref, hs_ref, rows), g_ref[...]).astype(BF16)

        def proj(lo, width):
            return jnp.dot(xn, w_ref[:, lo:lo + width], preferred_element_type=F32)

        qa_ref[rows, :] = (proj(0, ATT_WIDTH) * (ATT_SCALE * LOG2E)).astype(BF16)
        k_ref[rows, :] = proj(ATT_WIDTH, KV_WIDTH)
        v_ref[rows, :] = proj(ATT_WIDTH + KV_WIDTH, KV_WIDTH)
        cos = cos_ref[rows, :]
        sin = sin_ref[rows, :]

        def rot(x):
            return x * cos + pltpu.roll(x, RET_DIM // 2, 1) * sin

        q = proj(base, RET_WIDTH)
        for h in range(RET_HEADS):
            sl = slice(h * RET_DIM, (h + 1) * RET_DIM)
            qr_ref[rows, sl] = rot(q[:, sl]).astype(BF16)
        k = proj(base + RET_WIDTH, RET_WIDTH)
        for h in range(RET_HEADS):
            sl = slice(h * RET_DIM, (h + 1) * RET_DIM)
            kr_ref[rows, sl] = (rot(k[:, sl]) * RET_SCALE).astype(BF16)
        vr_ref[rows, :] = proj(base + 2 * RET_WIDTH, RET_WIDTH).astype(BF16)
        gr_ref[rows, :] = proj(base + 3 * RET_WIDTH, RET_WIDTH).astype(BF16)


def _in_proj(hp, hs, g, w, layer, cos2, sin2):
    tm = ROW_TILE
    row = _stacked_rows
    out = lambda width, dt: jax.ShapeDtypeStruct((N_TOK, width), dt)
    return pl.pallas_call(
        _in_proj_kernel,
        grid=(N_TOK // tm,),
        in_specs=[_prompt_rows(D_MODEL), _sample_rows(D_MODEL),
                  _resident_layer(g.shape, layer), _resident(w.shape),
                  row(RET_DIM), row(RET_DIM)],
        out_specs=(row(ATT_WIDTH), row(KV_WIDTH), row(KV_WIDTH),
                   row(RET_WIDTH), row(RET_WIDTH), row(RET_WIDTH), row(RET_WIDTH)),
        out_shape=(out(ATT_WIDTH, BF16), out(KV_WIDTH, F32), out(KV_WIDTH, F32),
                   out(RET_WIDTH, BF16), out(RET_WIDTH, BF16), out(RET_WIDTH, BF16),
                   out(RET_WIDTH, BF16)),
        compiler_params=_params("parallel"),
        name="in_proj",
    )(hp, hs, g, w, cos2, sin2)


PAIR = 2 * HEAD_DIM


def _attend_pairs(qst, kcat, vext, bias, sinks):
    half = kcat.shape[0] // 2
    s = lax.dot_general(qst, kcat, NT_DIMS, preferred_element_type=F32) + bias
    ps, es = [], []
    for side in range(2):
        ss = s[:, side * half:(side + 1) * half]
        m = jnp.maximum(jnp.max(ss, axis=-1, keepdims=True), sinks[side])
        ps.append(jnp.exp2(ss - jnp.concatenate([m] * (half // PAIR), axis=1)))
        es.append(jnp.exp2(sinks[side] - m))
    p = jnp.concatenate(ps, axis=1).astype(BF16)
    r = jnp.dot(p, vext, preferred_element_type=F32)
    lo = lax.broadcasted_iota(jnp.int32, qst.shape, 1) < HEAD_DIM
    den = r[:, PAIR:] + jnp.where(lo, es[0], es[1])
    return r[:, :PAIR] / den


def _fill_padded(dst, rows, x, sum_lanes=False):
    lo = lax.broadcasted_iota(jnp.int32, (x.shape[0], PAIR), 1) < HEAD_DIM
    for pair in range(N_KV // 2):
        xp = x[:, pair * PAIR:(pair + 1) * PAIR]
        xr = pltpu.roll(xp, HEAD_DIM, 1)
        zero = jnp.zeros_like(xp)
        dst[2 * pair, 0, rows, 0:PAIR] = jnp.where(lo, xp, zero).astype(BF16)
        dst[2 * pair, 1, rows, 0:PAIR] = jnp.where(lo, zero, xr).astype(BF16)
        dst[2 * pair + 1, 0, rows, 0:PAIR] = jnp.where(lo, xr, zero).astype(BF16)
        dst[2 * pair + 1, 1, rows, 0:PAIR] = jnp.where(lo, zero, xp).astype(BF16)
    if sum_lanes:
        for kv in range(N_KV):
            dst[kv, 0, rows, PAIR:2 * PAIR] = jnp.where(lo, 1.0, 0.0).astype(BF16)
            dst[kv, 1, rows, PAIR:2 * PAIR] = jnp.where(lo, 0.0, 1.0).astype(BF16)


def _attn_prompt_kernel(q_ref, kc_ref, kp_ref, vc_ref, vp_ref, bias_ref, sink_ref, o_ref, kz, vz):
    i = pl.program_id(0)
    _fill_padded(kz, slice(0, WINDOW), kp_ref[...])
    _fill_padded(kz, slice(WINDOW, WINDOW + ATT_TILE), kc_ref[...])
    _fill_padded(vz, slice(0, WINDOW), vp_ref[...], sum_lanes=True)
    _fill_padded(vz, slice(WINDOW, WINDOW + ATT_TILE), vc_ref[...], sum_lanes=True)

    def block(j, c):
        r0 = pl.multiple_of(j * Q_BLOCK, Q_BLOCK)
        rows = pl.ds(r0, Q_BLOCK)
        band = pl.ds(r0, K_BLOCK)
        first = jnp.where(jnp.logical_and(i == 0, j == 0), 1, 0)
        for kv in range(N_KV):
            c0 = kv * GROUP * HEAD_DIM
            qst = jnp.concatenate([q_ref[rows, c0:c0 + 2 * HEAD_DIM],
                                   q_ref[rows, c0 + 2 * HEAD_DIM:c0 + 4 * HEAD_DIM]], axis=0)
            o = _attend_pairs(
                qst,
                jnp.concatenate([kz[kv, side, band, :] for side in range(2)], axis=0),
                jnp.concatenate([vz[kv, side, band, :] for side in range(2)], axis=0),
                bias_ref[first, kv], [sink_ref[kv, side] for side in range(2)])
            o_ref[rows, c0:c0 + 2 * HEAD_DIM] = o[0:Q_BLOCK].astype(BF16)
            o_ref[rows, c0 + 2 * HEAD_DIM:c0 + 4 * HEAD_DIM] = o[Q_BLOCK:].astype(BF16)
        return c

    lax.fori_loop(0, ATT_TILE // Q_BLOCK, block, 0)


def _attn_prompt(qa, k, v, bias, sink):
    t = ATT_TILE
    per = t // WINDOW
    cur = lambda width: pl.BlockSpec((t, width), lambda i: (i, 0))
    prev = pl.BlockSpec((WINDOW, KV_WIDTH), lambda i: (jnp.maximum(i * per - 1, 0), 0))
    return pl.pallas_call(
        _attn_prompt_kernel,
        grid=(SEQ // t,),
        in_specs=[cur(ATT_WIDTH), cur(KV_WIDTH), prev, cur(KV_WIDTH), prev,
                  _resident(bias.shape), _resident(sink.shape)],
        out_specs=cur(ATT_WIDTH),
        out_shape=jax.ShapeDtypeStruct((SEQ, ATT_WIDTH), BF16),
        scratch_shapes=[pltpu.VMEM((N_KV, 2, t + WINDOW, PAIR), BF16),
                        pltpu.VMEM((N_KV, 2, t + WINDOW, 2 * PAIR), BF16)],
        compiler_params=_params("parallel"),
        name="attn_prompt",
    )(qa, k, k, v, v, bias, sink)


def _pair_layout(x):
    lead = x.shape[:-3]
    rows, cols = x.shape[-2:]
    x = x.reshape(lead + (N_KV, 2, 2, rows, cols))
    x = jnp.swapaxes(x, -4, -3)
    return x.reshape(lead + (N_KV, 2, 2 * rows, cols))


def _merge_sides(x):
    rows, cols = x.shape[-2:]
    return jnp.swapaxes(x, -3, -2).reshape(x.shape[:-3] + (rows, 2 * cols))


def _attn_sample_kernel(q_ref, kn_ref, vn_ref, ck_ref, cv_ref, bias_ref, sink_ref, o_ref, kz, vz):
    keys = WINDOW + DEC_SEQ
    kz[:, :, :, keys:, :] = jnp.zeros((SAMPLE_BLOCK, N_KV, 2, SAMPLE_KEYS - keys, PAIR), BF16)
    vz[:, :, :, keys:, :] = jnp.zeros((SAMPLE_BLOCK, N_KV, 2, SAMPLE_KEYS - keys, 2 * PAIR), BF16)
    for b in range(SAMPLE_BLOCK):
        new = slice(b * DEC_SEQ, (b + 1) * DEC_SEQ)
        _fill_padded(kz.at[b], slice(0, WINDOW), ck_ref[b])
        _fill_padded(kz.at[b], slice(WINDOW, keys), kn_ref[new, :])
        _fill_padded(vz.at[b], slice(0, WINDOW), cv_ref[b], sum_lanes=True)
        _fill_padded(vz.at[b], slice(WINDOW, keys), vn_ref[new, :], sum_lanes=True)
    for b in range(SAMPLE_BLOCK):
        rows = slice(b * DEC_SEQ, (b + 1) * DEC_SEQ)
        for kv in range(N_KV):
            c0 = kv * GROUP * HEAD_DIM
            qst = jnp.concatenate([q_ref[rows, c0:c0 + 2 * HEAD_DIM],
                                   q_ref[rows, c0 + 2 * HEAD_DIM:c0 + 4 * HEAD_DIM]], axis=0)
            o = _attend_pairs(qst, kz[b, kv].reshape(2 * SAMPLE_KEYS, PAIR),
                              vz[b, kv].reshape(2 * SAMPLE_KEYS, 2 * PAIR),
                              bias_ref[kv], [sink_ref[kv, side] for side in range(2)])
            o_ref[rows, c0:c0 + 2 * HEAD_DIM] = o[0:DEC_SEQ].astype(BF16)
            o_ref[rows, c0 + 2 * HEAD_DIM:c0 + 4 * HEAD_DIM] = o[DEC_SEQ:].astype(BF16)


def _attn_sample(qa, k, v, cache_k, cache_v, bias, sink):
    rows = SAMPLE_BLOCK * DEC_SEQ
    first = SEQ // rows
    new = lambda width: pl.BlockSpec((rows, width), lambda i: (first + i, 0))
    cache = pl.BlockSpec((SAMPLE_BLOCK, WINDOW, KV_WIDTH), lambda i: (i, 0, 0))
    return pl.pallas_call(
        _attn_sample_kernel,
        grid=(DEC_BATCH // SAMPLE_BLOCK,),
        in_specs=[new(ATT_WIDTH), new(KV_WIDTH), new(KV_WIDTH), cache, cache,
                  _resident(bias.shape), _resident(sink.shape)],
        out_specs=pl.BlockSpec((rows, ATT_WIDTH), lambda i: (i, 0)),
        out_shape=jax.ShapeDtypeStruct((N_SAMPLE, ATT_WIDTH), BF16),
        scratch_shapes=[pltpu.VMEM((SAMPLE_BLOCK, N_KV, 2, SAMPLE_KEYS, PAIR), BF16),
                        pltpu.VMEM((SAMPLE_BLOCK, N_KV, 2, SAMPLE_KEYS, 2 * PAIR), BF16)],
        compiler_params=_params("parallel"),
        name="attn_sample",
    )(qa, k, v, cache_k, cache_v, bias, sink)


def _retention_head(q, k, v, g, state, decay, zeta, xi, g_chunk, gain):
    scores = lax.dot_general(q, k, NT_DIMS, preferred_element_type=F32) * decay
    if q.shape[0] % RET_DIM == 0:
        lhs = jnp.concatenate([(q.astype(F32) * xi).astype(BF16), scores.astype(BF16)], axis=1)
        rhs = jnp.concatenate([state.astype(BF16), v], axis=0)
        o = jnp.dot(lhs, rhs, preferred_element_type=F32)
    else:
        o = jnp.dot(scores.astype(BF16), v, preferred_element_type=F32)
        o = o + jnp.dot(q, state.astype(BF16), preferred_element_type=F32) * xi
    kz = (k.astype(F32) * zeta).astype(BF16)
    upd = lax.dot_general(kz, v, TN_DIMS, preferred_element_type=F32)
    new_state = g_chunk * state + upd
    o = o * lax.rsqrt(jnp.mean(o * o, axis=-1, keepdims=True) + EPS) * gain
    gf = g.astype(F32)
    return gf * jax.nn.sigmoid(gf) * o, new_state


def _ret_prompt_kernel(gc_ref, q_ref, k_ref, v_ref, g_ref, decay_ref, zeta_ref, xi_ref, gain_ref,
                       o_ref, st_ref, state):
    i = pl.program_id(0)

    @pl.when(i == 0)
    def _():
        state[...] = jnp.zeros(state.shape, F32)

    for h in range(RET_HEADS):
        sl = slice(h * RET_DIM, (h + 1) * RET_DIM)
        o, s_new = _retention_head(q_ref[:, sl], k_ref[:, sl], v_ref[:, sl], g_ref[:, sl],
                                   state[h], decay_ref[h], zeta_ref[h], xi_ref[h], gc_ref[h],
                                   gain_ref[:, sl])
        state[h] = s_new
        o_ref[:, sl] = o.astype(BF16)

    @pl.when(i == pl.num_programs(0) - 1)
    def _():
        st_ref[...] = state[...]


def _ret_prompt(qr, kr, vr, gr, consts, gain):
    g_chunk, decay, zeta, xi = consts
    c = RET_TILE
    row = pl.BlockSpec((c, RET_WIDTH), lambda i: (i, 0))
    st_shape = (RET_HEADS, RET_DIM, RET_DIM)
    return pl.pallas_call(
        _ret_prompt_kernel,
        grid=(SEQ // c,),
        in_specs=[pl.BlockSpec(memory_space=pltpu.SMEM), row, row, row, row,
                  _resident(decay.shape), _resident(zeta.shape), _resident(xi.shape),
                  _resident(gain.shape)],
        out_specs=(row, pl.BlockSpec(st_shape, lambda i: (0, 0, 0))),
        out_shape=(jax.ShapeDtypeStruct((SEQ, RET_WIDTH), BF16),
                   jax.ShapeDtypeStruct(st_shape, F32)),
        scratch_shapes=[pltpu.VMEM(st_shape, F32)],
        compiler_params=_params("arbitrary"),
        name="ret_prompt",
    )(g_chunk, qr, kr, vr, gr, decay, zeta, xi, gain)


def _ret_sample_kernel(gc_ref, q_ref, k_ref, v_ref, g_ref, s_ref, decay_ref, zeta_ref, xi_ref,
                       gain_ref, *refs):
    _, o_ref, st_ref = refs
    for b in range(RET_SAMPLE_BLOCK):
        rows = slice(b * DEC_SEQ, (b + 1) * DEC_SEQ)
        for h in range(RET_HEADS):
            sl = slice(h * RET_DIM, (h + 1) * RET_DIM)
            o, s_new = _retention_head(q_ref[rows, sl], k_ref[rows, sl], v_ref[rows, sl],
                                       g_ref[rows, sl], s_ref[b, h], decay_ref[h], zeta_ref[h],
                                       xi_ref[h], gc_ref[h], gain_ref[:, sl])
            st_ref[b, h] = s_new
            o_ref[rows, sl] = o.astype(BF16)


def _ret_sample(qr, kr, vr, gr, state, layer, consts, gain, new_state):
    g_chunk, decay, zeta, xi = consts
    nb = RET_SAMPLE_BLOCK
    first = SEQ // (nb * DEC_SEQ)
    row = pl.BlockSpec((nb * DEC_SEQ, RET_WIDTH), lambda i: (first + i, 0))
    st_spec = pl.BlockSpec((None, nb, RET_HEADS, RET_DIM, RET_DIM), lambda i: (layer, i, 0, 0, 0))
    in_specs = [pl.BlockSpec(memory_space=pltpu.SMEM), row, row, row, row, st_spec,
                _resident(decay.shape), _resident(zeta.shape), _resident(xi.shape),
                _resident(gain.shape)]
    args = [g_chunk, qr, kr, vr, gr, state, decay, zeta, xi, gain, new_state]
    in_specs.append(pl.BlockSpec(memory_space=pl.ANY))
    aliases = {len(args) - 1: 1}
    return pl.pallas_call(
        _ret_sample_kernel,
        grid=(DEC_BATCH // nb,),
        in_specs=in_specs,
        out_specs=(pl.BlockSpec((nb * DEC_SEQ, RET_WIDTH), lambda i: (i, 0)), st_spec),
        out_shape=(jax.ShapeDtypeStruct((N_SAMPLE, RET_WIDTH), BF16),
                   jax.ShapeDtypeStruct(state.shape, F32)),
        input_output_aliases=aliases,
        compiler_params=_params("parallel"),
        name="ret_sample",
    )(*args)


def _retention_consts(chunk):
    lg = jnp.log(1.0 - 2.0 ** (-5.0 - jnp.arange(RET_HEADS, dtype=F32)))
    idx = jnp.arange(chunk, dtype=F32)
    diff = idx[:, None] - idx[None, :]
    decay = jnp.where(diff >= 0, jnp.exp(lg[:, None, None] * jnp.maximum(diff, 0.0)), 0.0)
    zeta = jnp.exp(lg[:, None] * (chunk - 1 - idx))[:, :, None]
    xi = jnp.exp(lg[:, None] * (idx + 1))[:, :, None]
    dense = (RET_HEADS, chunk, RET_DIM)
    return (jnp.exp(lg * chunk), decay,
            jnp.broadcast_to(zeta, dense), jnp.broadcast_to(xi, dense))


def _out_proj_kernel(oap_ref, oas_ref, orp_ref, ors_ref, hp_ref, hs_ref, w_ref, g_ref,
                     h1_ref, xn_ref):
    for rows in _row_parts():
        h1 = (_pick_rows(hp_ref, hs_ref, rows)
              + jnp.dot(_pick_rows(oap_ref, oas_ref, rows), w_ref[0:ATT_WIDTH, :],
                        preferred_element_type=F32)
              + jnp.dot(_pick_rows(orp_ref, ors_ref, rows), w_ref[ATT_WIDTH:, :],
                        preferred_element_type=F32))
        h1_ref[rows, :] = h1
        xn_ref[rows, :] = _rms(h1, g_ref[...]).astype(BF16)


def _out_proj(oa_p, oa_s, or_p, or_s, hp, hs, w, g, layer):
    row = _stacked_rows
    return pl.pallas_call(
        _out_proj_kernel,
        grid=(N_TOK // ROW_TILE,),
        in_specs=[_prompt_rows(ATT_WIDTH), _sample_rows(ATT_WIDTH),
                  _prompt_rows(RET_WIDTH), _sample_rows(RET_WIDTH),
                  _prompt_rows(D_MODEL), _sample_rows(D_MODEL),
                  _resident(w.shape), _resident_layer(g.shape, layer)],
        out_specs=(row(D_MODEL), row(D_MODEL)),
        out_shape=(jax.ShapeDtypeStruct((N_TOK, D_MODEL), F32),
                   jax.ShapeDtypeStruct((N_TOK, D_MODEL), BF16)),
        compiler_params=_params("parallel"),
        name="out_proj",
    )(oa_p, oa_s, or_p, or_s, hp, hs, w, g)


def _mlp_kernel(xn_ref, wu_ref, wd_ref, *refs):
    n_extra = (len(refs) - 1) // 2
    o_ref = refs[n_extra]

    @pl.when(pl.program_id(1) == 0)
    def _():
        o_ref[...] = jnp.zeros(o_ref.shape, F32)

    up = jnp.dot(xn_ref[...], wu_ref[...].astype(BF16), preferred_element_type=F32)
    act = jnp.square(jnp.maximum(up, 0.0)).astype(BF16)
    o_ref[...] += jnp.dot(act, wd_ref[...].astype(BF16), preferred_element_type=F32)
    for src, dst in zip(refs[:n_extra], refs[n_extra + 1:]):
        dst[...] = src[...].astype(BF16)


def _mlp(xn, w_up, w_down, layer, cast_jobs=()):
    tm, tf = MLP_ROW_TILE, MLP_FF_TILE
    n_ff = D_FF // tf
    steps = (N_TOK // tm) * n_ff
    slab_in, slab_out, slab_shape = [], [], []
    for w, lyr in cast_jobs:
        _, k, n = w.shape
        rows = k // steps
        assert rows * steps == k and rows % 16 == 0
        slab_in.append(pl.BlockSpec((None, rows, n), lambda i, f, lyr=lyr: (lyr, i * n_ff + f, 0)))
        slab_out.append(pl.BlockSpec((rows, n), lambda i, f: (i * n_ff + f, 0)))
        slab_shape.append(jax.ShapeDtypeStruct((k, n), BF16))
    return pl.pallas_call(
        _mlp_kernel,
        grid=(N_TOK // tm, n_ff),
        in_specs=[pl.BlockSpec((tm, D_MODEL), lambda i, f: (i, 0)),
                  pl.BlockSpec((None, D_MODEL, tf), lambda i, f: (layer, 0, f)),
                  pl.BlockSpec((None, tf, D_MODEL), lambda i, f: (layer, f, 0))] + slab_in,
        out_specs=[pl.BlockSpec((tm, D_MODEL), lambda i, f: (i, 0))] + slab_out,
        out_shape=[jax.ShapeDtypeStruct((N_TOK, D_MODEL), F32)] + slab_shape,
        compiler_params=_params("arbitrary", "arbitrary"),
        name="mlp",
    )(xn, w_up, w_down, *[w for w, _ in cast_jobs])


def _gate_kernel(h1_ref, m_ref, pp_ref, ps_ref, wg_ref, wp_ref, g_ref, gf_ref, op_ref, os_ref,
                 *, final):
    for rows in _row_parts():
        h2 = h1_ref[rows, :] + m_ref[rows, :]
        xn = _rms(h2, g_ref[...]).astype(BF16)
        gate = jax.nn.sigmoid(jnp.dot(xn, wg_ref[...], preferred_element_type=F32))
        emb = jnp.dot(_pick_rows(pp_ref, ps_ref, rows).astype(BF16), wp_ref[...],
                      preferred_element_type=F32)
        h3 = h2 + gate * emb
        op_ref[rows, :] = _rms(h3, gf_ref[...]) if final else h3

    @pl.when(pl.program_id(0) == 0)
    def _():
        os_ref[...] = op_ref[...]


def _gate(h1, mlp_out, p_prompt, p_sample, layer, w_gate, w_ple, g, g_final, final):
    row = _stacked_rows
    return pl.pallas_call(
        functools.partial(_gate_kernel, final=final),
        grid=(N_TOK // ROW_TILE,),
        in_specs=[row(D_MODEL), row(D_MODEL),
                  _prompt_rows(PLE_DIM, layer, 0), _sample_rows(PLE_DIM, layer),
                  _resident(w_gate.shape), _resident_layer(w_ple.shape, layer),
                  _resident_layer(g.shape, layer), _resident(g_final.shape)],
        out_specs=(_prompt_rows(D_MODEL), _sample_rows(D_MODEL)),
        out_shape=(jax.ShapeDtypeStruct((SEQ, D_MODEL), F32),
                   jax.ShapeDtypeStruct((N_SAMPLE, D_MODEL), F32)),
        compiler_params=_params("arbitrary"),
        name="gate",
    )(h1, mlp_out, p_prompt, p_sample, w_gate, w_ple, g, g_final)


def _t5_bucket(rel):
    nb = N_BUCKETS // 2
    max_exact = nb // 2
    ret = jnp.where(rel > 0, nb, 0)
    n = jnp.abs(rel)
    nf = jnp.maximum(n, max_exact).astype(F32)
    large = max_exact + (jnp.log(nf / max_exact) / math.log(MAX_DISTANCE / max_exact)
                         * (nb - max_exact)).astype(jnp.int32)
    large = jnp.minimum(large, nb - 1)
    return ret + jnp.where(n < max_exact, n, large)


def _rotary_tables():
    half = RET_DIM // 2
    inv = ROPE_BASE ** (-jnp.arange(half, dtype=F32) / half)

    def cos_sin(pos):
        ang = pos.astype(F32)[:, None] * inv[None, :]
        return jnp.cos(ang), jnp.sin(ang)

    ca, sa = cos_sin(CHUNK * jnp.arange(SEQ // CHUNK, dtype=jnp.int32))
    cb, sb = cos_sin(jnp.arange(CHUNK, dtype=jnp.int32))
    cos_p = (ca[:, None] * cb[None] - sa[:, None] * sb[None]).reshape(SEQ, half)
    sin_p = (sa[:, None] * cb[None] + ca[:, None] * sb[None]).reshape(SEQ, half)
    cos_s, sin_s = cos_sin(PAST_LEN + jnp.arange(DEC_SEQ, dtype=jnp.int32))
    cos = jnp.concatenate([cos_p, jnp.tile(cos_s, (DEC_BATCH, 1))])
    sin = jnp.concatenate([sin_p, jnp.tile(sin_s, (DEC_BATCH, 1))])
    return jnp.concatenate([cos, cos], axis=-1), jnp.concatenate([-sin, sin], axis=-1)


def kernel(x_prompt, x_sample, cache_k, cache_v, state_ret, p_prompt, p_sample, rel_bias,
           norm_mix, w_in, attn_sink, ret_gain, w_out, norm_mlp, w_up, w_down, norm_ple,
           w_gate, w_ple, norm_final):
    q_idx = jnp.arange(Q_BLOCK, dtype=jnp.int32)[:, None]
    s_idx = jnp.arange(K_BLOCK, dtype=jnp.int32)[None, :]
    bucket_p = _t5_bucket(s_idx - WINDOW - q_idx)
    bucket_s = _t5_bucket(jnp.arange(WINDOW + DEC_SEQ, dtype=jnp.int32)[None, :] - WINDOW
                          - jnp.arange(DEC_SEQ, dtype=jnp.int32)[:, None])
    bias_p, bias_s = _rel_bias(rel_bias, bucket_p, bucket_s)
    bias_p, bias_s = bias_p * LOG2E, bias_s * LOG2E
    back = q_idx // CHUNK + WINDOW // CHUNK - s_idx // CHUNK
    seen = jnp.logical_and(back >= 0, back <= WINDOW // CHUNK)
    seen_first = jnp.logical_and(seen, s_idx >= WINDOW)
    bias_p = _merge_sides(_pair_layout(jnp.stack([jnp.where(seen, bias_p, -jnp.inf),
                                                  jnp.where(seen_first, bias_p, -jnp.inf)])))
    bias_s = jnp.pad(bias_s, ((0, 0), (0, 0), (0, SAMPLE_KEYS - WINDOW - DEC_SEQ)),
                     constant_values=-jnp.inf)
    bias_s = _merge_sides(_pair_layout(bias_s))

    cos2, sin2 = _rotary_tables()
    consts_p = _retention_consts(RET_TILE)
    consts_s = _retention_consts(DEC_SEQ)

    w_in_bf = {0: w_in[0].astype(BF16)}
    w_out_bf = {0: w_out[0].astype(BF16)}
    w_gate_bf = {}
    w_ple = w_ple.astype(BF16)
    norm_mix, norm_mlp, norm_ple = (g.reshape(DEPTH, 1, D_MODEL)
                                    for g in (norm_mix, norm_mlp, norm_ple))
    hp = x_prompt.reshape(SEQ, D_MODEL)
    hs = x_sample.reshape(N_SAMPLE, D_MODEL)
    p_sample = p_sample.reshape(DEPTH, N_SAMPLE, PLE_DIM)
    k_p, v_p, s_p, k_s, v_s = [], [], [], [], []
    state_s = jnp.zeros(state_ret.shape, F32)
    for l in range(DEPTH):
        sink = attn_sink[l].astype(F32).reshape(N_HEADS, 1, 1) * LOG2E
        sink_p = _pair_layout(jnp.broadcast_to(sink, (N_HEADS, Q_BLOCK, PAIR)))
        sink_s = _pair_layout(jnp.broadcast_to(sink, (N_HEADS, DEC_SEQ, PAIR)))
        gain = ret_gain[l].reshape(1, RET_WIDTH)

        qa, k, v, qr, kr, vr, gr = _in_proj(hp, hs, norm_mix, w_in_bf[l], l, cos2, sin2)
        oa_p = _attn_prompt(qa, k, v, bias_p, sink_p)
        oa_s = _attn_sample(qa, k, v, cache_k[l].reshape(DEC_BATCH, WINDOW, KV_WIDTH),
                            cache_v[l].reshape(DEC_BATCH, WINDOW, KV_WIDTH), bias_s, sink_s)
        or_p, st_p = _ret_prompt(qr, kr, vr, gr, consts_p, gain)
        or_s, state_s = _ret_sample(qr, kr, vr, gr, state_ret, l, consts_s, gain, state_s)

        h1, xn = _out_proj(oa_p, oa_s, or_p, or_s, hp, hs, w_out_bf[l], norm_mlp, l)
        jobs = [(w_gate, l)]
        if l + 1 < DEPTH:
            jobs += [(w_in, l + 1), (w_out, l + 1)]
        mlp_out, *cast = _mlp(xn, w_up, w_down, l, jobs)
        w_gate_bf[l] = cast[0]
        if l + 1 < DEPTH:
            w_in_bf[l + 1], w_out_bf[l + 1] = cast[1], cast[2]
        hp, hs = _gate(h1, mlp_out, p_prompt, p_sample, l, w_gate_bf[l], w_ple, norm_ple,
                       norm_final.reshape(1, D_MODEL), final=(l == DEPTH - 1))

        k_p.append(k[SEQ - WINDOW:SEQ].reshape(1, WINDOW, N_KV, HEAD_DIM))
        v_p.append(v[SEQ - WINDOW:SEQ].reshape(1, WINDOW, N_KV, HEAD_DIM))
        s_p.append(st_p[None])
        k_s.append(k[SEQ:].reshape(DEC_BATCH, DEC_SEQ, N_KV, HEAD_DIM))
        v_s.append(v[SEQ:].reshape(DEC_BATCH, DEC_SEQ, N_KV, HEAD_DIM))

    return (hp.reshape(1, SEQ, D_MODEL), hs.reshape(DEC_BATCH, DEC_SEQ, D_MODEL),
            jnp.stack(k_p), jnp.stack(v_p), jnp.stack(s_p),
            jnp.stack(k_s), jnp.stack(v_s), state_s)
```

```python
import functools
import math

import jax
import jax.numpy as jnp
from jax import lax
from jax.experimental import pallas as pl
from jax.experimental.pallas import tpu as pltpu

F32 = jnp.float32
BF16 = jnp.bfloat16

D_MODEL = 2048
SEQ = 8192
DEPTH = 2
DEC_BATCH = 32
DEC_SEQ = 16
PAST_LEN = 4096
N_SAMPLE = DEC_BATCH * DEC_SEQ
N_TOK = SEQ + N_SAMPLE

CHUNK = 64
WINDOW = 128
BAND = WINDOW + CHUNK
HEAD_DIM = 64
N_HEADS = 16
N_KV = 4
GROUP = N_HEADS // N_KV
ATT_WIDTH = N_HEADS * HEAD_DIM
KV_WIDTH = N_KV * HEAD_DIM
RET_DIM = 128
RET_HEADS = 8
RET_WIDTH = RET_HEADS * RET_DIM
IN_WIDTH = ATT_WIDTH + 2 * KV_WIDTH + 4 * RET_WIDTH
D_FF = 4 * D_MODEL
PLE_DIM = 256
N_BUCKETS = 32
MAX_DISTANCE = 128
ROPE_BASE = 10000.0
EPS = 1e-6
ATT_SCALE = HEAD_DIM ** -0.5
LOG2E = 1.0 / math.log(2.0)
RET_SCALE = RET_DIM ** -0.5

ROW_TILE = 512
ROW_PARTS = 2
MLP_ROW_TILE = 1088
MLP_FF_TILE = 512
Q_BLOCK = 2 * CHUNK
K_BLOCK = WINDOW + Q_BLOCK
RET_TILE = 256
SAMPLE_BLOCK = 8
SAMPLE_KEYS = 256
RET_SAMPLE_BLOCK = 4
VMEM_LIMIT = 58 * 1024 * 1024
MIX_OUT_VMEM_LIMIT = 62 * 1024 * 1024

NT_DIMS = (((1,), (1,)), ((), ()))
TN_DIMS = (((0,), (0,)), ((), ()))


def _params(*sem, vmem_limit=VMEM_LIMIT):
    return pltpu.CompilerParams(dimension_semantics=sem, vmem_limit_bytes=vmem_limit)


def _resident(shape):
    nd = len(shape)
    return pl.BlockSpec(shape, lambda *_: (0,) * nd, pipeline_mode=pl.Buffered(1))


def _resident_layer(shape, layer):
    nd = len(shape) - 1
    return pl.BlockSpec((None,) + tuple(shape[1:]), lambda *_: (layer,) + (0,) * nd,
                        pipeline_mode=pl.Buffered(1))


def _rms(x, g):
    ms = jnp.mean(x * x, axis=-1, keepdims=True)
    return x * lax.rsqrt(ms + EPS) * g


N_PROMPT_TILES = SEQ // ROW_TILE
assert N_SAMPLE == ROW_TILE


def _stacked_rows(width):
    return pl.BlockSpec((ROW_TILE, width),
                        lambda i: ((i + N_PROMPT_TILES) % (N_PROMPT_TILES + 1), 0))


def _prompt_rows(width, *lead):
    none = (None,) * len(lead)
    return pl.BlockSpec(none + (ROW_TILE, width), lambda i: lead + (jnp.maximum(i - 1, 0), 0))


def _sample_rows(width, *lead, single_buffer=True):
    none = (None,) * len(lead)
    mode = dict(pipeline_mode=pl.Buffered(1)) if single_buffer else {}
    return pl.BlockSpec(none + (ROW_TILE, width), lambda i: lead + (0, 0), **mode)


def _row_parts():
    part = ROW_TILE // ROW_PARTS
    return [slice(r * part, (r + 1) * part) for r in range(ROW_PARTS)]


def _pick_rows(prompt_ref, sample_ref, rows=slice(None)):
    return jnp.where(pl.program_id(0) == 0, sample_ref[rows, :], prompt_ref[rows, :])


def _bias_kernel(table_ref, bp_ref, bs_ref, op_ref, os_ref):
    bp = bp_ref[...]
    bs = bs_ref[...]

    def per_head(h, c):
        ap = jnp.zeros(bp.shape, F32)
        as_ = jnp.zeros(bs.shape, F32)
        for b in range(N_BUCKETS):
            t = table_ref[b, h]
            ap = jnp.where(bp == b, t, ap)
            as_ = jnp.where(bs == b, t, as_)
        op_ref[h] = ap
        os_ref[h] = as_
        return c

    lax.fori_loop(0, N_HEADS, per_head, 0)


def _rel_bias(table, bucket_p, bucket_s):
    return pl.pallas_call(
        _bias_kernel,
        out_shape=(jax.ShapeDtypeStruct((N_HEADS,) + bucket_p.shape, F32),
                   jax.ShapeDtypeStruct((N_HEADS,) + bucket_s.shape, F32)),
        in_specs=[pl.BlockSpec(memory_space=pltpu.SMEM),
                  pl.BlockSpec(memory_space=pltpu.VMEM),
                  pl.BlockSpec(memory_space=pltpu.VMEM)],
        out_specs=(pl.BlockSpec(memory_space=pltpu.VMEM), pl.BlockSpec(memory_space=pltpu.VMEM)),
        name="rel_bias",
    )(table, bucket_p, bucket_s)


def _in_proj_kernel(hp_ref, hs_ref, g_ref, w_ref, cos_ref, sin_ref,
                    qa_ref, k_ref, v_ref, qr_ref, kr_ref, vr_ref, gr_ref):
    base = ATT_WIDTH + 2 * KV_WIDTH
    for rows in _row_parts():
        xn = _rms(_pick_rows(hp_ref, hs_ref, rows), g_ref[...]).astype(BF16)

        def proj(lo, width):
            return jnp.dot(xn, w_ref[:, lo:lo + width], preferred_element_type=F32)

        qa_ref[rows, :] = (proj(0, ATT_WIDTH) * (ATT_SCALE * LOG2E)).astype(BF16)
        k_ref[rows, :] = proj(ATT_WIDTH, KV_WIDTH)
        v_ref[rows, :] = proj(ATT_WIDTH + KV_WIDTH, KV_WIDTH)
        cos = cos_ref[rows, :]
        sin = sin_ref[rows, :]

        def rot(x):
            return x * cos + pltpu.roll(x, RET_DIM // 2, 1) * sin

        q = proj(base, RET_WIDTH)
        for h in range(RET_HEADS):
            sl = slice(h * RET_DIM, (h + 1) * RET_DIM)
            qr_ref[rows, sl] = rot(q[:, sl]).astype(BF16)
        k = proj(base + RET_WIDTH, RET_WIDTH)
        for h in range(RET_HEADS):
            sl = slice(h * RET_DIM, (h + 1) * RET_DIM)
            kr_ref[rows, sl] = (rot(k[:, sl]) * RET_SCALE).astype(BF16)
        vr_ref[rows, :] = proj(base + 2 * RET_WIDTH, RET_WIDTH).astype(BF16)
        gr_ref[rows, :] = proj(base + 3 * RET_WIDTH, RET_WIDTH).astype(BF16)


def _in_proj(hp, hs, g, w, layer, cos2, sin2):
    tm = ROW_TILE
    row = _stacked_rows
    out = lambda width, dt: jax.ShapeDtypeStruct((N_TOK, width), dt)
    return pl.pallas_call(
        _in_proj_kernel,
        grid=(N_TOK // tm,),
        in_specs=[_prompt_rows(D_MODEL), _sample_rows(D_MODEL),
                  _resident_layer(g.shape, layer), _resident(w.shape),
                  row(RET_DIM), row(RET_DIM)],
        out_specs=(row(ATT_WIDTH), row(KV_WIDTH), row(KV_WIDTH),
                   row(RET_WIDTH), row(RET_WIDTH), row(RET_WIDTH), row(RET_WIDTH)),
        out_shape=(out(ATT_WIDTH, BF16), out(KV_WIDTH, F32), out(KV_WIDTH, F32),
                   out(RET_WIDTH, BF16), out(RET_WIDTH, BF16), out(RET_WIDTH, BF16),
                   out(RET_WIDTH, BF16)),
        compiler_params=_params("parallel"),
        name="in_proj",
    )(hp, hs, g, w, cos2, sin2)


PAIR = 2 * HEAD_DIM


def _attend_pairs(qst, kcat, vcat, bias, sinks):
    half = kcat.shape[0] // 2
    side0 = lax.broadcasted_iota(jnp.int32, vcat.shape, 0) < half
    low = lax.broadcasted_iota(jnp.int32, vcat.shape, 1) < HEAD_DIM
    vext = jnp.concatenate([vcat, jnp.where(side0 == low, 1.0, 0.0).astype(BF16)], axis=1)
    s = lax.dot_general(qst, kcat, NT_DIMS, preferred_element_type=F32) + bias
    ps, es = [], []
    for side in range(2):
        ss = s[:, side * half:(side + 1) * half]
        m = jnp.maximum(jnp.max(ss, axis=-1, keepdims=True), sinks[side])
        ps.append(jnp.exp2(ss - jnp.concatenate([m] * (half // PAIR), axis=1)))
        es.append(jnp.exp2(sinks[side] - m))
    p = jnp.concatenate(ps, axis=1).astype(BF16)
    r = jnp.dot(p, vext, preferred_element_type=F32)
    lo = lax.broadcasted_iota(jnp.int32, qst.shape, 1) < HEAD_DIM
    den = r[:, PAIR:] + jnp.where(lo, es[0], es[1])
    return r[:, :PAIR] / den


def _fill_padded(dst, rows, x):
    lo = lax.broadcasted_iota(jnp.int32, (x.shape[0], PAIR), 1) < HEAD_DIM
    for pair in range(N_KV // 2):
        xp = x[:, pair * PAIR:(pair + 1) * PAIR]
        xr = pltpu.roll(xp, HEAD_DIM, 1)
        zero = jnp.zeros_like(xp)
        dst[2 * pair, 0, rows, :] = jnp.where(lo, xp, zero).astype(BF16)
        dst[2 * pair, 1, rows, :] = jnp.where(lo, zero, xr).astype(BF16)
        dst[2 * pair + 1, 0, rows, :] = jnp.where(lo, xr, zero).astype(BF16)
        dst[2 * pair + 1, 1, rows, :] = jnp.where(lo, zero, xp).astype(BF16)


def _pair_layout(x):
    lead = x.shape[:-3]
    rows, cols = x.shape[-2:]
    x = x.reshape(lead + (N_KV, 2, 2, rows, cols))
    x = jnp.swapaxes(x, -4, -3)
    return x.reshape(lead + (N_KV, 2, 2 * rows, cols))


def _merge_sides(x):
    rows, cols = x.shape[-2:]
    return jnp.swapaxes(x, -3, -2).reshape(x.shape[:-3] + (rows, 2 * cols))


def _attn_sample_kernel(q_ref, kn_ref, vn_ref, ck_ref, cv_ref, bias_ref, sink_ref, o_ref, kz, vz):
    keys = WINDOW + DEC_SEQ
    pad = jnp.zeros((SAMPLE_BLOCK, N_KV, 2, SAMPLE_KEYS - keys, PAIR), BF16)
    kz[:, :, :, keys:, :] = pad
    vz[:, :, :, keys:, :] = pad
    for b in range(SAMPLE_BLOCK):
        new = slice(b * DEC_SEQ, (b + 1) * DEC_SEQ)
        _fill_padded(kz.at[b], slice(0, WINDOW), ck_ref[b])
        _fill_padded(kz.at[b], slice(WINDOW, keys), kn_ref[new, :])
        _fill_padded(vz.at[b], slice(0, WINDOW), cv_ref[b])
        _fill_padded(vz.at[b], slice(WINDOW, keys), vn_ref[new, :])
    for b in range(SAMPLE_BLOCK):
        rows = slice(b * DEC_SEQ, (b + 1) * DEC_SEQ)
        for kv in range(N_KV):
            c0 = kv * GROUP * HEAD_DIM
            qst = jnp.concatenate([q_ref[rows, c0:c0 + 2 * HEAD_DIM],
                                   q_ref[rows, c0 + 2 * HEAD_DIM:c0 + 4 * HEAD_DIM]], axis=0)
            o = _attend_pairs(qst, kz[b, kv].reshape(2 * SAMPLE_KEYS, PAIR),
                              vz[b, kv].reshape(2 * SAMPLE_KEYS, PAIR),
                              bias_ref[kv], [sink_ref[kv, side] for side in range(2)])
            o_ref[rows, c0:c0 + 2 * HEAD_DIM] = o[0:DEC_SEQ].astype(BF16)
            o_ref[rows, c0 + 2 * HEAD_DIM:c0 + 4 * HEAD_DIM] = o[DEC_SEQ:].astype(BF16)


def _attn_sample(qa, k, v, cache_k, cache_v, bias, sink):
    rows = SAMPLE_BLOCK * DEC_SEQ
    first = SEQ // rows
    new = lambda width: pl.BlockSpec((rows, width), lambda i: (first + i, 0))
    cache = pl.BlockSpec((SAMPLE_BLOCK, WINDOW, KV_WIDTH), lambda i: (i, 0, 0))
    return pl.pallas_call(
        _attn_sample_kernel,
        grid=(DEC_BATCH // SAMPLE_BLOCK,),
        in_specs=[new(ATT_WIDTH), new(KV_WIDTH), new(KV_WIDTH), cache, cache,
                  _resident(bias.shape), _resident(sink.shape)],
        out_specs=pl.BlockSpec((rows, ATT_WIDTH), lambda i: (i, 0)),
        out_shape=jax.ShapeDtypeStruct((N_SAMPLE, ATT_WIDTH), BF16),
        scratch_shapes=[pltpu.VMEM((SAMPLE_BLOCK, N_KV, 2, SAMPLE_KEYS, PAIR), BF16)] * 2,
        compiler_params=_params("parallel"),
        name="attn_sample",
    )(qa, k, v, cache_k, cache_v, bias, sink)


def _retention_head(q, k, v, g, state, decay, zeta, xi, g_chunk, gain):
    scores = lax.dot_general(q, k, NT_DIMS, preferred_element_type=F32) * decay
    if q.shape[0] % RET_DIM == 0:
        lhs = jnp.concatenate([(q.astype(F32) * xi).astype(BF16), scores.astype(BF16)], axis=1)
        rhs = jnp.concatenate([state.astype(BF16), v], axis=0)
        o = jnp.dot(lhs, rhs, preferred_element_type=F32)
    else:
        o = jnp.dot(scores.astype(BF16), v, preferred_element_type=F32)
        o = o + jnp.dot(q, state.astype(BF16), preferred_element_type=F32) * xi
    kz = (k.astype(F32) * zeta).astype(BF16)
    upd = lax.dot_general(kz, v, TN_DIMS, preferred_element_type=F32)
    new_state = g_chunk * state + upd
    o = o * lax.rsqrt(jnp.mean(o * o, axis=-1, keepdims=True) + EPS) * gain
    gf = g.astype(F32)
    return gf * jax.nn.sigmoid(gf) * o, new_state


def _ret_sample_kernel(gc_ref, q_ref, k_ref, v_ref, g_ref, s_ref, decay_ref, zeta_ref, xi_ref,
                       gain_ref, *refs):
    _, o_ref, st_ref = refs
    for b in range(RET_SAMPLE_BLOCK):
        rows = slice(b * DEC_SEQ, (b + 1) * DEC_SEQ)
        for h in range(RET_HEADS):
            sl = slice(h * RET_DIM, (h + 1) * RET_DIM)
            o, s_new = _retention_head(q_ref[rows, sl], k_ref[rows, sl], v_ref[rows, sl],
                                       g_ref[rows, sl], s_ref[b, h], decay_ref[h], zeta_ref[h],
                                       xi_ref[h], gc_ref[h], gain_ref[:, sl])
            st_ref[b, h] = s_new
            o_ref[rows, sl] = o.astype(BF16)


def _ret_sample(qr, kr, vr, gr, state, layer, consts, gain, new_state):
    g_chunk, decay, zeta, xi = consts
    nb = RET_SAMPLE_BLOCK
    first = SEQ // (nb * DEC_SEQ)
    row = pl.BlockSpec((nb * DEC_SEQ, RET_WIDTH), lambda i: (first + i, 0))
    st_spec = pl.BlockSpec((None, nb, RET_HEADS, RET_DIM, RET_DIM), lambda i: (layer, i, 0, 0, 0))
    in_specs = [pl.BlockSpec(memory_space=pltpu.SMEM), row, row, row, row, st_spec,
                _resident(decay.shape), _resident(zeta.shape), _resident(xi.shape),
                _resident(gain.shape)]
    args = [g_chunk, qr, kr, vr, gr, state, decay, zeta, xi, gain, new_state]
    in_specs.append(pl.BlockSpec(memory_space=pl.ANY))
    aliases = {len(args) - 1: 1}
    return pl.pallas_call(
        _ret_sample_kernel,
        grid=(DEC_BATCH // nb,),
        in_specs=in_specs,
        out_specs=(pl.BlockSpec((nb * DEC_SEQ, RET_WIDTH), lambda i: (i, 0)), st_spec),
        out_shape=(jax.ShapeDtypeStruct((N_SAMPLE, RET_WIDTH), BF16),
                   jax.ShapeDtypeStruct(state.shape, F32)),
        input_output_aliases=aliases,
        compiler_params=_params("parallel"),
        name="ret_sample",
    )(*args)


def _retention_consts(chunk):
    lg = jnp.log(1.0 - 2.0 ** (-5.0 - jnp.arange(RET_HEADS, dtype=F32)))
    idx = jnp.arange(chunk, dtype=F32)
    diff = idx[:, None] - idx[None, :]
    decay = jnp.where(diff >= 0, jnp.exp(lg[:, None, None] * jnp.maximum(diff, 0.0)), 0.0)
    zeta = jnp.exp(lg[:, None] * (chunk - 1 - idx))[:, :, None]
    xi = jnp.exp(lg[:, None] * (idx + 1))[:, :, None]
    dense = (RET_HEADS, chunk, RET_DIM)
    return (jnp.exp(lg * chunk), decay,
            jnp.broadcast_to(zeta, dense), jnp.broadcast_to(xi, dense))


def _mix_out_kernel(gc_ref, q_ref, kc_ref, kp_ref, vc_ref, vp_ref, bias_ref, sink_ref,
                    qr_ref, kr_ref, vr_ref, gr_ref, decay_ref, zeta_ref, xi_ref, gain_ref,
                    oas_ref, ors_ref, hp_ref, hs_ref, w_ref, g_ref,
                    h1_ref, xn_ref, st_ref, kz, vz, state):
    i = pl.program_id(0)

    def project(rows, h, oa, orr):
        h1 = (h + jnp.dot(oa, w_ref[0:ATT_WIDTH, :], preferred_element_type=F32)
              + jnp.dot(orr, w_ref[ATT_WIDTH:, :], preferred_element_type=F32))
        h1_ref[rows, :] = h1
        xn_ref[rows, :] = _rms(h1, g_ref[...]).astype(BF16)

    @pl.when(i == 0)
    def _():
        state[...] = jnp.zeros(state.shape, F32)
        for rows in _row_parts():
            project(rows, hs_ref[rows, :], oas_ref[rows, :], ors_ref[rows, :])

    @pl.when(i > 0)
    def _():
        _fill_padded(kz, slice(0, WINDOW), kp_ref[...])
        _fill_padded(kz, slice(WINDOW, WINDOW + ROW_TILE), kc_ref[...])
        _fill_padded(vz, slice(0, WINDOW), vp_ref[...])
        _fill_padded(vz, slice(WINDOW, WINDOW + ROW_TILE), vc_ref[...])
        col = lax.broadcasted_iota(jnp.int32, (2 * Q_BLOCK, 2 * K_BLOCK), 1)
        before_start = jnp.logical_and(i == 1, col % K_BLOCK < WINDOW)
        for part, rows in enumerate(_row_parts()):
            blocks = []
            for j in range(part * Q_PER_PART, (part + 1) * Q_PER_PART):
                qrows = slice(j * Q_BLOCK, (j + 1) * Q_BLOCK)
                band = slice(j * Q_BLOCK, j * Q_BLOCK + K_BLOCK)
                pieces = []
                for kv in range(N_KV):
                    c0 = kv * GROUP * HEAD_DIM
                    qst = jnp.concatenate([q_ref[qrows, c0:c0 + PAIR],
                                           q_ref[qrows, c0 + PAIR:c0 + 2 * PAIR]], axis=0)
                    bias = bias_ref[kv]
                    if j == 0:
                        bias = jnp.where(before_start, -jnp.inf, bias)
                    o = _attend_pairs(
                        qst,
                        jnp.concatenate([kz[kv, side, band, :] for side in range(2)], axis=0),
                        jnp.concatenate([vz[kv, side, band, :] for side in range(2)], axis=0),
                        bias, [sink_ref[kv, side] for side in range(2)])
                    pieces += [o[0:Q_BLOCK].astype(BF16), o[Q_BLOCK:].astype(BF16)]
                blocks.append(jnp.concatenate(pieces, axis=1))
            oa = jnp.concatenate(blocks, axis=0)
            heads = []
            for h in range(RET_HEADS):
                sl = slice(h * RET_DIM, (h + 1) * RET_DIM)
                o, s_new = _retention_head(qr_ref[rows, sl], kr_ref[rows, sl], vr_ref[rows, sl],
                                           gr_ref[rows, sl], state[h], decay_ref[h], zeta_ref[h],
                                           xi_ref[h], gc_ref[h], gain_ref[:, sl])
                state[h] = s_new
                heads.append(o.astype(BF16))
            project(rows, hp_ref[rows, :], oa, jnp.concatenate(heads, axis=1))

    @pl.when(i == pl.num_programs(0) - 1)
    def _():
        st_ref[...] = state[...]


assert ROW_TILE // ROW_PARTS == RET_TILE and RET_TILE % Q_BLOCK == 0
Q_PER_PART = RET_TILE // Q_BLOCK


def _mix_out(qa, k, v, bias, sink, qr, kr, vr, gr, consts, gain, oa_s, or_s, hp, hs, w, g, layer):
    g_chunk, decay, zeta, xi = consts
    per = ROW_TILE // WINDOW
    prev = pl.BlockSpec((WINDOW, KV_WIDTH),
                        lambda i: (jnp.maximum(jnp.maximum(i - 1, 0) * per - 1, 0), 0))
    row = _stacked_rows
    st_shape = (RET_HEADS, RET_DIM, RET_DIM)
    return pl.pallas_call(
        _mix_out_kernel,
        grid=(N_TOK // ROW_TILE,),
        in_specs=[pl.BlockSpec(memory_space=pltpu.SMEM),
                  _prompt_rows(ATT_WIDTH), _prompt_rows(KV_WIDTH), prev,
                  _prompt_rows(KV_WIDTH), prev, _resident(bias.shape), _resident(sink.shape),
                  _prompt_rows(RET_WIDTH), _prompt_rows(RET_WIDTH), _prompt_rows(RET_WIDTH),
                  _prompt_rows(RET_WIDTH), _resident(decay.shape), _resident(zeta.shape),
                  _resident(xi.shape), _resident(gain.shape),
                  _sample_rows(ATT_WIDTH), _sample_rows(RET_WIDTH),
                  _prompt_rows(D_MODEL), _sample_rows(D_MODEL),
                  _resident(w.shape), _resident_layer(g.shape, layer)],
        out_specs=(row(D_MODEL), row(D_MODEL), pl.BlockSpec(st_shape, lambda i: (0, 0, 0))),
        out_shape=(jax.ShapeDtypeStruct((N_TOK, D_MODEL), F32),
                   jax.ShapeDtypeStruct((N_TOK, D_MODEL), BF16),
                   jax.ShapeDtypeStruct(st_shape, F32)),
        scratch_shapes=[pltpu.VMEM((N_KV, 2, ROW_TILE + WINDOW, PAIR), BF16),
                        pltpu.VMEM((N_KV, 2, ROW_TILE + WINDOW, PAIR), BF16),
                        pltpu.VMEM(st_shape, F32)],
        compiler_params=_params("arbitrary", vmem_limit=MIX_OUT_VMEM_LIMIT),
        name="mix_out",
    )(g_chunk, qa, k, k, v, v, bias, sink, qr, kr, vr, gr, decay, zeta, xi, gain,
      oa_s, or_s, hp, hs, w, g)


def _mlp_kernel(xn_ref, wu_ref, wd_ref, *refs):
    n_extra = (len(refs) - 1) // 2
    o_ref = refs[n_extra]

    @pl.when(pl.program_id(1) == 0)
    def _():
        o_ref[...] = jnp.zeros(o_ref.shape, F32)

    up = jnp.dot(xn_ref[...], wu_ref[...].astype(BF16), preferred_element_type=F32)
    act = jnp.square(jnp.maximum(up, 0.0)).astype(BF16)
    o_ref[...] += jnp.dot(act, wd_ref[...].astype(BF16), preferred_element_type=F32)
    for src, dst in zip(refs[:n_extra], refs[n_extra + 1:]):
        dst[...] = src[...].astype(BF16)


def _mlp(xn, w_up, w_down, layer, cast_jobs=()):
    tm, tf = MLP_ROW_TILE, MLP_FF_TILE
    n_ff = D_FF // tf
    steps = (N_TOK // tm) * n_ff
    slab_in, slab_out, slab_shape = [], [], []
    for w, lyr in cast_jobs:
        _, k, n = w.shape
        rows = k // steps
        assert rows * steps == k and rows % 16 == 0
        slab_in.append(pl.BlockSpec((None, rows, n), lambda i, f, lyr=lyr: (lyr, i * n_ff + f, 0)))
        slab_out.append(pl.BlockSpec((rows, n), lambda i, f: (i * n_ff + f, 0)))
        slab_shape.append(jax.ShapeDtypeStruct((k, n), BF16))
    return pl.pallas_call(
        _mlp_kernel,
        grid=(N_TOK // tm, n_ff),
        in_specs=[pl.BlockSpec((tm, D_MODEL), lambda i, f: (i, 0)),
                  pl.BlockSpec((None, D_MODEL, tf), lambda i, f: (layer, 0, f)),
                  pl.BlockSpec((None, tf, D_MODEL), lambda i, f: (layer, f, 0))] + slab_in,
        out_specs=[pl.BlockSpec((tm, D_MODEL), lambda i, f: (i, 0))] + slab_out,
        out_shape=[jax.ShapeDtypeStruct((N_TOK, D_MODEL), F32)] + slab_shape,
        compiler_params=_params("arbitrary", "arbitrary"),
        name="mlp",
    )(xn, w_up, w_down, *[w for w, _ in cast_jobs])


def _gate_kernel(h1_ref, m_ref, pp_ref, ps_ref, wg_ref, wp_ref, g_ref, gf_ref, op_ref, os_ref,
                 *, final):
    for rows in _row_parts():
        h2 = h1_ref[rows, :] + m_ref[rows, :]
        xn = _rms(h2, g_ref[...]).astype(BF16)
        gate = jax.nn.sigmoid(jnp.dot(xn, wg_ref[...], preferred_element_type=F32))
        emb = jnp.dot(_pick_rows(pp_ref, ps_ref, rows).astype(BF16), wp_ref[...],
                      preferred_element_type=F32)
        h3 = h2 + gate * emb
        op_ref[rows, :] = _rms(h3, gf_ref[...]) if final else h3

    @pl.when(pl.program_id(0) == 0)
    def _():
        os_ref[...] = op_ref[...]


def _gate(h1, mlp_out, p_prompt, p_sample, layer, w_gate, w_ple, g, g_final, final):
    row = _stacked_rows
    return pl.pallas_call(
        functools.partial(_gate_kernel, final=final),
        grid=(N_TOK // ROW_TILE,),
        in_specs=[row(D_MODEL), row(D_MODEL),
                  _prompt_rows(PLE_DIM, layer, 0), _sample_rows(PLE_DIM, layer),
                  _resident(w_gate.shape), _resident_layer(w_ple.shape, layer),
                  _resident_layer(g.shape, layer), _resident(g_final.shape)],
        out_specs=(_prompt_rows(D_MODEL), _sample_rows(D_MODEL, single_buffer=False)),
        out_shape=(jax.ShapeDtypeStruct((SEQ, D_MODEL), F32),
                   jax.ShapeDtypeStruct((N_SAMPLE, D_MODEL), F32)),
        compiler_params=_params("arbitrary"),
        name="gate",
    )(h1, mlp_out, p_prompt, p_sample, w_gate, w_ple, g, g_final)


def _t5_bucket(rel):
    nb = N_BUCKETS // 2
    max_exact = nb // 2
    ret = jnp.where(rel > 0, nb, 0)
    n = jnp.abs(rel)
    nf = jnp.maximum(n, max_exact).astype(F32)
    large = max_exact + (jnp.log(nf / max_exact) / math.log(MAX_DISTANCE / max_exact)
                         * (nb - max_exact)).astype(jnp.int32)
    large = jnp.minimum(large, nb - 1)
    return ret + jnp.where(n < max_exact, n, large)


def _rotary_tables():
    half = RET_DIM // 2
    inv = ROPE_BASE ** (-jnp.arange(half, dtype=F32) / half)

    def cos_sin(pos):
        ang = pos.astype(F32)[:, None] * inv[None, :]
        return jnp.cos(ang), jnp.sin(ang)

    ca, sa = cos_sin(CHUNK * jnp.arange(SEQ // CHUNK, dtype=jnp.int32))
    cb, sb = cos_sin(jnp.arange(CHUNK, dtype=jnp.int32))
    cos_p = (ca[:, None] * cb[None] - sa[:, None] * sb[None]).reshape(SEQ, half)
    sin_p = (sa[:, None] * cb[None] + ca[:, None] * sb[None]).reshape(SEQ, half)
    cos_s, sin_s = cos_sin(PAST_LEN + jnp.arange(DEC_SEQ, dtype=jnp.int32))
    cos = jnp.concatenate([cos_p, jnp.tile(cos_s, (DEC_BATCH, 1))])
    sin = jnp.concatenate([sin_p, jnp.tile(sin_s, (DEC_BATCH, 1))])
    return jnp.concatenate([cos, cos], axis=-1), jnp.concatenate([-sin, sin], axis=-1)


def kernel(x_prompt, x_sample, cache_k, cache_v, state_ret, p_prompt, p_sample, rel_bias,
           norm_mix, w_in, attn_sink, ret_gain, w_out, norm_mlp, w_up, w_down, norm_ple,
           w_gate, w_ple, norm_final):
    q_idx = jnp.arange(Q_BLOCK, dtype=jnp.int32)[:, None]
    s_idx = jnp.arange(K_BLOCK, dtype=jnp.int32)[None, :]
    bucket_p = _t5_bucket(s_idx - WINDOW - q_idx)
    bucket_s = _t5_bucket(jnp.arange(WINDOW + DEC_SEQ, dtype=jnp.int32)[None, :] - WINDOW
                          - jnp.arange(DEC_SEQ, dtype=jnp.int32)[:, None])
    bias_p, bias_s = _rel_bias(rel_bias, bucket_p, bucket_s)
    bias_p, bias_s = bias_p * LOG2E, bias_s * LOG2E
    back = q_idx // CHUNK + WINDOW // CHUNK - s_idx // CHUNK
    seen = jnp.logical_and(back >= 0, back <= WINDOW // CHUNK)
    bias_p = _merge_sides(_pair_layout(jnp.where(seen, bias_p, -jnp.inf)))
    bias_s = jnp.pad(bias_s, ((0, 0), (0, 0), (0, SAMPLE_KEYS - WINDOW - DEC_SEQ)),
                     constant_values=-jnp.inf)
    bias_s = _merge_sides(_pair_layout(bias_s))

    cos2, sin2 = _rotary_tables()
    consts_p = _retention_consts(RET_TILE)
    consts_s = _retention_consts(DEC_SEQ)

    w_in_bf = {0: w_in[0].astype(BF16)}
    w_out_bf = {0: w_out[0].astype(BF16)}
    w_gate_bf = {}
    w_ple = w_ple.astype(BF16)
    norm_mix, norm_mlp, norm_ple = (g.reshape(DEPTH, 1, D_MODEL)
                                    for g in (norm_mix, norm_mlp, norm_ple))
    hp = x_prompt.reshape(SEQ, D_MODEL)
    hs = x_sample.reshape(N_SAMPLE, D_MODEL)
    p_sample = p_sample.reshape(DEPTH, N_SAMPLE, PLE_DIM)
    k_p, v_p, s_p, k_s, v_s = [], [], [], [], []
    state_s = jnp.zeros(state_ret.shape, F32)
    for l in range(DEPTH):
        sink = attn_sink[l].astype(F32).reshape(N_HEADS, 1, 1) * LOG2E
        sink_p = _pair_layout(jnp.broadcast_to(sink, (N_HEADS, Q_BLOCK, PAIR)))
        sink_s = _pair_layout(jnp.broadcast_to(sink, (N_HEADS, DEC_SEQ, PAIR)))
        gain = ret_gain[l].reshape(1, RET_WIDTH)

        qa, k, v, qr, kr, vr, gr = _in_proj(hp, hs, norm_mix, w_in_bf[l], l, cos2, sin2)
        oa_s = _attn_sample(qa, k, v, cache_k[l].reshape(DEC_BATCH, WINDOW, KV_WIDTH),
                            cache_v[l].reshape(DEC_BATCH, WINDOW, KV_WIDTH), bias_s, sink_s)
        or_s, state_s = _ret_sample(qr, kr, vr, gr, state_ret, l, consts_s, gain, state_s)
        h1, xn, st_p = _mix_out(qa, k, v, bias_p, sink_p, qr, kr, vr, gr, consts_p, gain,
                                oa_s, or_s, hp, hs, w_out_bf[l], norm_mlp, l)
        jobs = [(w_gate, l)]
        if l + 1 < DEPTH:
            jobs += [(w_in, l + 1), (w_out, l + 1)]
        mlp_out, *cast = _mlp(xn, w_up, w_down, l, jobs)
        w_gate_bf[l] = cast[0]
        if l + 1 < DEPTH:
            w_in_bf[l + 1], w_out_bf[l + 1] = cast[1], cast[2]
        hp, hs = _gate(h1, mlp_out, p_prompt, p_sample, l, w_gate_bf[l], w_ple, norm_ple,
                       norm_final.reshape(1, D_MODEL), final=(l == DEPTH - 1))

        k_p.append(k[SEQ - WINDOW:SEQ].reshape(1, WINDOW, N_KV, HEAD_DIM))
        v_p.append(v[SEQ - WINDOW:SEQ].reshape(1, WINDOW, N_KV, HEAD_DIM))
        s_p.append(st_p[None])
        k_s.append(k[SEQ:].reshape(DEC_BATCH, DEC_SEQ, N_KV, HEAD_DIM))
        v_s.append(v[SEQ:].reshape(DEC_BATCH, DEC_SEQ, N_KV, HEAD_DIM))

    return (hp.reshape(1, SEQ, D_MODEL), hs.reshape(DEC_BATCH, DEC_SEQ, D_MODEL),
            jnp.stack(k_p), jnp.stack(v_p), jnp.stack(s_p),
            jnp.stack(k_s), jnp.stack(v_s), state_s)
```

```python
import functools
import math

import jax
import jax.numpy as jnp
from jax import lax
from jax.experimental import pallas as pl
from jax.experimental.pallas import tpu as pltpu

F32 = jnp.float32
BF16 = jnp.bfloat16

D_MODEL = 2048
SEQ = 8192
DEPTH = 2
DEC_BATCH = 32
DEC_SEQ = 16
PAST_LEN = 4096
N_SAMPLE = DEC_BATCH * DEC_SEQ
N_TOK = SEQ + N_SAMPLE

CHUNK = 64
WINDOW = 128
HEAD_DIM = 64
N_HEADS = 16
N_KV = 4
GROUP = N_HEADS // N_KV
ATT_WIDTH = N_HEADS * HEAD_DIM
KV_WIDTH = N_KV * HEAD_DIM
RET_DIM = 128
RET_HEADS = 8
RET_WIDTH = RET_HEADS * RET_DIM
IN_WIDTH = ATT_WIDTH + 2 * KV_WIDTH + 4 * RET_WIDTH
D_FF = 4 * D_MODEL
PLE_DIM = 256
N_BUCKETS = 32
MAX_DISTANCE = 128
ROPE_BASE = 10000.0
EPS = 1e-6
ATT_SCALE = HEAD_DIM ** -0.5
LOG2E = 1.0 / math.log(2.0)
RET_SCALE = RET_DIM ** -0.5

ROW_TILE = 512
ROW_PARTS = 2
MLP_ROW_TILE = 1088
MLP_FF_TILE = 512
Q_BLOCK = 2 * CHUNK
K_BLOCK = WINDOW + Q_BLOCK
RET_TILE = 256
SAMPLE_BLOCK = 8
SAMPLE_KEYS = 256
RET_SAMPLE_BLOCK = 4
VMEM_LIMIT = 58 * 1024 * 1024
MIX_OUT_VMEM_LIMIT = 62 * 1024 * 1024

NT_DIMS = (((1,), (1,)), ((), ()))
TN_DIMS = (((0,), (0,)), ((), ()))


def _params(*sem, vmem_limit=VMEM_LIMIT):
    return pltpu.CompilerParams(dimension_semantics=sem, vmem_limit_bytes=vmem_limit)


def _resident(shape):
    nd = len(shape)
    return pl.BlockSpec(shape, lambda *_: (0,) * nd, pipeline_mode=pl.Buffered(1))


def _resident_layer(shape, layer):
    nd = len(shape) - 1
    return pl.BlockSpec((None,) + tuple(shape[1:]), lambda *_: (layer,) + (0,) * nd,
                        pipeline_mode=pl.Buffered(1))


def _rms(x, g):
    ms = jnp.mean(x * x, axis=-1, keepdims=True)
    return x * lax.rsqrt(ms + EPS) * g


N_PROMPT_TILES = SEQ // ROW_TILE
assert N_SAMPLE == ROW_TILE


def _stacked_rows(width):
    return pl.BlockSpec((ROW_TILE, width),
                        lambda i: ((i + N_PROMPT_TILES) % (N_PROMPT_TILES + 1), 0))


def _prompt_rows(width, *lead):
    none = (None,) * len(lead)
    return pl.BlockSpec(none + (ROW_TILE, width), lambda i: lead + (jnp.maximum(i - 1, 0), 0))


def _sample_rows(width, *lead, single_buffer=True):
    none = (None,) * len(lead)
    mode = dict(pipeline_mode=pl.Buffered(1)) if single_buffer else {}
    return pl.BlockSpec(none + (ROW_TILE, width), lambda i: lead + (0, 0), **mode)


def _row_parts(parts=ROW_PARTS):
    part = ROW_TILE // parts
    return [slice(r * part, (r + 1) * part) for r in range(parts)]


def _pick_rows(prompt_ref, sample_ref, rows=slice(None)):
    return jnp.where(pl.program_id(0) == 0, sample_ref[rows, :], prompt_ref[rows, :])


def _bias_kernel(table_ref, bp_ref, bs_ref, op_ref, os_ref):
    bp = bp_ref[...]
    bs = bs_ref[...]

    def lookup(bucket, head):
        acc = jnp.full(bucket.shape, -jnp.inf, F32)
        for b in range(N_BUCKETS):
            acc = jnp.where(bucket == b, table_ref[b, head] * LOG2E, acc)
        return acc

    def per_kv(kv, c):
        for half in range(2):
            for side in range(2):
                head = kv * GROUP + 2 * half + side
                rp, cp = bp.shape
                op_ref[kv, half * rp:(half + 1) * rp, side * cp:(side + 1) * cp] = lookup(bp, head)
                rs, cs = bs.shape
                os_ref[kv, half * rs:(half + 1) * rs, side * cs:(side + 1) * cs] = lookup(bs, head)
        return c

    lax.fori_loop(0, N_KV, per_kv, 0)


def _rel_bias(table, bucket_p, bucket_s):
    def out(bucket):
        rows, cols = bucket.shape
        return jax.ShapeDtypeStruct((N_KV, 2 * rows, 2 * cols), F32)

    return pl.pallas_call(
        _bias_kernel,
        out_shape=(out(bucket_p), out(bucket_s)),
        in_specs=[pl.BlockSpec(memory_space=pltpu.SMEM),
                  pl.BlockSpec(memory_space=pltpu.VMEM),
                  pl.BlockSpec(memory_space=pltpu.VMEM)],
        out_specs=(pl.BlockSpec(memory_space=pltpu.VMEM), pl.BlockSpec(memory_space=pltpu.VMEM)),
        name="rel_bias",
    )(table, bucket_p, bucket_s)


def _in_proj_kernel(hp_ref, hs_ref, g_ref, w_ref, cos_ref, sin_ref,
                    qa_ref, k_ref, v_ref, qr_ref, kr_ref, vr_ref, gr_ref):
    base = ATT_WIDTH + 2 * KV_WIDTH
    for rows in _row_parts():
        xn = _rms(_pick_rows(hp_ref, hs_ref, rows), g_ref[...]).astype(BF16)

        def proj(lo, width):
            return jnp.dot(xn, w_ref[:, lo:lo + width], preferred_element_type=F32)

        qa_ref[rows, :] = (proj(0, ATT_WIDTH) * (ATT_SCALE * LOG2E)).astype(BF16)
        k_ref[rows, :] = proj(ATT_WIDTH, KV_WIDTH)
        v_ref[rows, :] = proj(ATT_WIDTH + KV_WIDTH, KV_WIDTH)
        cos = cos_ref[rows, :]
        sin = sin_ref[rows, :]

        def rot(x):
            return x * cos + pltpu.roll(x, RET_DIM // 2, 1) * sin

        q = proj(base, RET_WIDTH)
        for h in range(RET_HEADS):
            sl = slice(h * RET_DIM, (h + 1) * RET_DIM)
            qr_ref[rows, sl] = rot(q[:, sl]).astype(BF16)
        k = proj(base + RET_WIDTH, RET_WIDTH)
        for h in range(RET_HEADS):
            sl = slice(h * RET_DIM, (h + 1) * RET_DIM)
            kr_ref[rows, sl] = (rot(k[:, sl]) * RET_SCALE).astype(BF16)
        vr_ref[rows, :] = proj(base + 2 * RET_WIDTH, RET_WIDTH).astype(BF16)
        gr_ref[rows, :] = proj(base + 3 * RET_WIDTH, RET_WIDTH).astype(BF16)


def _in_proj(hp, hs, g, w, layer, cos2, sin2):
    tm = ROW_TILE
    row = _stacked_rows
    out = lambda width, dt: jax.ShapeDtypeStruct((N_TOK, width), dt)
    return pl.pallas_call(
        _in_proj_kernel,
        grid=(N_TOK // tm,),
        in_specs=[_prompt_rows(D_MODEL), _sample_rows(D_MODEL),
                  _resident_layer(g.shape, layer), _resident(w.shape),
                  row(RET_DIM), row(RET_DIM)],
        out_specs=(row(ATT_WIDTH), row(KV_WIDTH), row(KV_WIDTH),
                   row(RET_WIDTH), row(RET_WIDTH), row(RET_WIDTH), row(RET_WIDTH)),
        out_shape=(out(ATT_WIDTH, BF16), out(KV_WIDTH, F32), out(KV_WIDTH, F32),
                   out(RET_WIDTH, BF16), out(RET_WIDTH, BF16), out(RET_WIDTH, BF16),
                   out(RET_WIDTH, BF16)),
        compiler_params=_params("parallel"),
        name="in_proj",
    )(hp, hs, g, w, cos2, sin2)


PAIR = 2 * HEAD_DIM


def _attend_pairs(qst, kcat, vcat, bias, sinks):
    half = kcat.shape[0] // 2
    side0 = lax.broadcasted_iota(jnp.int32, vcat.shape, 0) < half
    low = lax.broadcasted_iota(jnp.int32, vcat.shape, 1) < HEAD_DIM
    vext = jnp.concatenate([vcat, jnp.where(side0 == low, 1.0, 0.0).astype(BF16)], axis=1)
    s = lax.dot_general(qst, kcat, NT_DIMS, preferred_element_type=F32) + bias
    ps, es = [], []
    for side in range(2):
        ss = s[:, side * half:(side + 1) * half]
        m = jnp.maximum(jnp.max(ss, axis=-1, keepdims=True), sinks[side])
        ps.append(jnp.exp2(ss - jnp.concatenate([m] * (half // PAIR), axis=1)))
        es.append(jnp.exp2(sinks[side] - m))
    p = jnp.concatenate(ps, axis=1).astype(BF16)
    r = jnp.dot(p, vext, preferred_element_type=F32)
    lo = lax.broadcasted_iota(jnp.int32, qst.shape, 1) < HEAD_DIM
    den = r[:, PAIR:] + jnp.where(lo, es[0], es[1])
    return r[:, :PAIR] / den


def _fill_padded(dst, rows, x):
    lo = lax.broadcasted_iota(jnp.int32, (x.shape[0], PAIR), 1) < HEAD_DIM
    for pair in range(N_KV // 2):
        xp = x[:, pair * PAIR:(pair + 1) * PAIR]
        xr = pltpu.roll(xp, HEAD_DIM, 1)
        zero = jnp.zeros_like(xp)
        dst[2 * pair, 0, rows, :] = jnp.where(lo, xp, zero).astype(BF16)
        dst[2 * pair, 1, rows, :] = jnp.where(lo, zero, xr).astype(BF16)
        dst[2 * pair + 1, 0, rows, :] = jnp.where(lo, xr, zero).astype(BF16)
        dst[2 * pair + 1, 1, rows, :] = jnp.where(lo, zero, xp).astype(BF16)


def _pair_layout(x):
    lead = x.shape[:-3]
    rows, cols = x.shape[-2:]
    x = x.reshape(lead + (N_KV, 2, 2, rows, cols))
    x = jnp.swapaxes(x, -4, -3)
    return x.reshape(lead + (N_KV, 2, 2 * rows, cols))


def _attn_sample_kernel(q_ref, kn_ref, vn_ref, ck_ref, cv_ref, bias_ref, sink_ref, o_ref, kz, vz):
    keys = WINDOW + DEC_SEQ
    pad = jnp.zeros((SAMPLE_BLOCK, N_KV, 2, SAMPLE_KEYS - keys, PAIR), BF16)
    kz[:, :, :, keys:, :] = pad
    vz[:, :, :, keys:, :] = pad
    for b in range(SAMPLE_BLOCK):
        new = slice(b * DEC_SEQ, (b + 1) * DEC_SEQ)
        _fill_padded(kz.at[b], slice(0, WINDOW), ck_ref[b])
        _fill_padded(kz.at[b], slice(WINDOW, keys), kn_ref[new, :])
        _fill_padded(vz.at[b], slice(0, WINDOW), cv_ref[b])
        _fill_padded(vz.at[b], slice(WINDOW, keys), vn_ref[new, :])
    for b in range(SAMPLE_BLOCK):
        rows = slice(b * DEC_SEQ, (b + 1) * DEC_SEQ)
        for kv in range(N_KV):
            c0 = kv * GROUP * HEAD_DIM
            qst = jnp.concatenate([q_ref[rows, c0:c0 + 2 * HEAD_DIM],
                                   q_ref[rows, c0 + 2 * HEAD_DIM:c0 + 4 * HEAD_DIM]], axis=0)
            o = _attend_pairs(qst, kz[b, kv].reshape(2 * SAMPLE_KEYS, PAIR),
                              vz[b, kv].reshape(2 * SAMPLE_KEYS, PAIR),
                              bias_ref[kv], [sink_ref[kv, side] for side in range(2)])
            o_ref[rows, c0:c0 + 2 * HEAD_DIM] = o[0:DEC_SEQ].astype(BF16)
            o_ref[rows, c0 + 2 * HEAD_DIM:c0 + 4 * HEAD_DIM] = o[DEC_SEQ:].astype(BF16)


def _attn_sample(qa, k, v, cache_k, cache_v, bias, sink, layer):
    rows = SAMPLE_BLOCK * DEC_SEQ
    first = SEQ // rows
    new = lambda width: pl.BlockSpec((rows, width), lambda i: (first + i, 0))
    cache = pl.BlockSpec((SAMPLE_BLOCK, WINDOW, KV_WIDTH), lambda i: (i, 0, 0))
    return pl.pallas_call(
        _attn_sample_kernel,
        grid=(DEC_BATCH // SAMPLE_BLOCK,),
        in_specs=[new(ATT_WIDTH), new(KV_WIDTH), new(KV_WIDTH), cache, cache,
                  _resident(bias.shape), _resident_layer(sink.shape, layer)],
        out_specs=pl.BlockSpec((rows, ATT_WIDTH), lambda i: (i, 0)),
        out_shape=jax.ShapeDtypeStruct((N_SAMPLE, ATT_WIDTH), BF16),
        scratch_shapes=[pltpu.VMEM((SAMPLE_BLOCK, N_KV, 2, SAMPLE_KEYS, PAIR), BF16)] * 2,
        compiler_params=_params("parallel"),
        name="attn_sample",
    )(qa, k, v, cache_k, cache_v, bias, sink)


def _retention_head(q, k, v, g, state, decay, zeta, xi, g_chunk, gain):
    scores = lax.dot_general(q, k, NT_DIMS, preferred_element_type=F32) * decay
    if q.shape[0] % RET_DIM == 0:
        lhs = jnp.concatenate([(q.astype(F32) * xi).astype(BF16), scores.astype(BF16)], axis=1)
        rhs = jnp.concatenate([state.astype(BF16), v], axis=0)
        o = jnp.dot(lhs, rhs, preferred_element_type=F32)
    else:
        o = jnp.dot(scores.astype(BF16), v, preferred_element_type=F32)
        o = o + jnp.dot(q, state.astype(BF16), preferred_element_type=F32) * xi
    kz = (k.astype(F32) * zeta).astype(BF16)
    upd = lax.dot_general(kz, v, TN_DIMS, preferred_element_type=F32)
    new_state = g_chunk * state + upd
    o = o * lax.rsqrt(jnp.mean(o * o, axis=-1, keepdims=True) + EPS) * gain
    gf = g.astype(F32)
    return gf * jax.nn.sigmoid(gf) * o, new_state


def _ret_sample_kernel(gc_ref, q_ref, k_ref, v_ref, g_ref, s_ref, decay_ref, zeta_ref, xi_ref,
                       gain_ref, *refs, opening):
    o_ref, st_ref = refs[-2:]
    if opening:
        st_ref[1:] = jnp.zeros((DEPTH - 1,) + st_ref.shape[1:], F32)
        st_ref = st_ref.at[0]
    for b in range(RET_SAMPLE_BLOCK):
        rows = slice(b * DEC_SEQ, (b + 1) * DEC_SEQ)
        for h in range(RET_HEADS):
            sl = slice(h * RET_DIM, (h + 1) * RET_DIM)
            o, s_new = _retention_head(q_ref[rows, sl], k_ref[rows, sl], v_ref[rows, sl],
                                       g_ref[rows, sl], s_ref[b, h], decay_ref[h], zeta_ref[h],
                                       xi_ref[h], gc_ref[h], gain_ref[:, sl])
            st_ref[b, h] = s_new
            o_ref[rows, sl] = o.astype(BF16)


def _ret_sample(qr, kr, vr, gr, state, layer, consts, gain, new_state):
    g_chunk, decay, zeta, xi = consts
    nb = RET_SAMPLE_BLOCK
    first = SEQ // (nb * DEC_SEQ)
    row = pl.BlockSpec((nb * DEC_SEQ, RET_WIDTH), lambda i: (first + i, 0))
    st_block = (nb, RET_HEADS, RET_DIM, RET_DIM)
    st_spec = pl.BlockSpec((None,) + st_block, lambda i: (layer, i, 0, 0, 0))
    in_specs = [pl.BlockSpec(memory_space=pltpu.SMEM), row, row, row, row, st_spec,
                _resident(decay.shape), _resident(zeta.shape), _resident(xi.shape),
                _resident(gain.shape)]
    args = [g_chunk, qr, kr, vr, gr, state, decay, zeta, xi, gain]
    opening = new_state is None
    if opening:
        assert layer == 0
        aliases = {}
        st_out = pl.BlockSpec((DEPTH,) + st_block, lambda i: (0, i, 0, 0, 0))
    else:
        args.append(new_state)
        in_specs.append(pl.BlockSpec(memory_space=pl.ANY))
        aliases = {len(args) - 1: 1}
        st_out = st_spec
    return pl.pallas_call(
        functools.partial(_ret_sample_kernel, opening=opening),
        grid=(DEC_BATCH // nb,),
        in_specs=in_specs,
        out_specs=(pl.BlockSpec((nb * DEC_SEQ, RET_WIDTH), lambda i: (i, 0)), st_out),
        out_shape=(jax.ShapeDtypeStruct((N_SAMPLE, RET_WIDTH), BF16),
                   jax.ShapeDtypeStruct(state.shape, F32)),
        input_output_aliases=aliases,
        compiler_params=_params("parallel"),
        name="ret_sample",
    )(*args)


def _retention_consts(chunk):
    lg = jnp.log(1.0 - 2.0 ** (-5.0 - jnp.arange(RET_HEADS, dtype=F32)))
    idx = jnp.arange(chunk, dtype=F32)
    diff = idx[:, None] - idx[None, :]
    decay = jnp.where(diff >= 0, jnp.exp(lg[:, None, None] * jnp.maximum(diff, 0.0)), 0.0)
    zeta = jnp.exp(lg[:, None] * (chunk - 1 - idx))[:, :, None]
    xi = jnp.exp(lg[:, None] * (idx + 1))[:, :, None]
    dense = (RET_HEADS, chunk, RET_DIM)
    return (jnp.exp(lg * chunk), decay,
            jnp.broadcast_to(zeta, dense), jnp.broadcast_to(xi, dense))


def _mix_out_kernel(gc_ref, q_ref, kc_ref, kp_ref, vc_ref, vp_ref, bias_ref, sink_ref,
                    qr_ref, kr_ref, vr_ref, gr_ref, decay_ref, zeta_ref, xi_ref, gain_ref,
                    oas_ref, ors_ref, hp_ref, hs_ref, w_ref, g_ref,
                    h1_ref, xn_ref, st_ref, kz, vz, state):
    i = pl.program_id(0)

    def project(rows, h, oa, orr):
        h1 = (h + jnp.dot(oa, w_ref[0:ATT_WIDTH, :], preferred_element_type=F32)
              + jnp.dot(orr, w_ref[ATT_WIDTH:, :], preferred_element_type=F32))
        h1_ref[rows, :] = h1
        xn_ref[rows, :] = _rms(h1, g_ref[...]).astype(BF16)

    @pl.when(i == 0)
    def _():
        state[...] = jnp.zeros(state.shape, F32)
        for rows in _row_parts():
            project(rows, hs_ref[rows, :], oas_ref[rows, :], ors_ref[rows, :])

    @pl.when(i > 0)
    def _():
        _fill_padded(kz, slice(0, WINDOW), kp_ref[...])
        _fill_padded(kz, slice(WINDOW, WINDOW + ROW_TILE), kc_ref[...])
        _fill_padded(vz, slice(0, WINDOW), vp_ref[...])
        _fill_padded(vz, slice(WINDOW, WINDOW + ROW_TILE), vc_ref[...])
        col = lax.broadcasted_iota(jnp.int32, (2 * Q_BLOCK, 2 * K_BLOCK), 1)
        before_start = jnp.logical_and(i == 1, col % K_BLOCK < WINDOW)
        for part, rows in enumerate(_row_parts(MIX_PARTS)):
            blocks = []
            for j in range(part * Q_PER_PART, (part + 1) * Q_PER_PART):
                qrows = slice(j * Q_BLOCK, (j + 1) * Q_BLOCK)
                band = slice(j * Q_BLOCK, j * Q_BLOCK + K_BLOCK)
                pieces = []
                for kv in range(N_KV):
                    c0 = kv * GROUP * HEAD_DIM
                    qst = jnp.concatenate([q_ref[qrows, c0:c0 + PAIR],
                                           q_ref[qrows, c0 + PAIR:c0 + 2 * PAIR]], axis=0)
                    bias = bias_ref[kv]
                    if j == 0:
                        bias = jnp.where(before_start, -jnp.inf, bias)
                    o = _attend_pairs(
                        qst,
                        jnp.concatenate([kz[kv, side, band, :] for side in range(2)], axis=0),
                        jnp.concatenate([vz[kv, side, band, :] for side in range(2)], axis=0),
                        bias, [sink_ref[kv, side] for side in range(2)])
                    pieces += [o[0:Q_BLOCK].astype(BF16), o[Q_BLOCK:].astype(BF16)]
                blocks.append(jnp.concatenate(pieces, axis=1))
            oa = jnp.concatenate(blocks, axis=0)
            heads = []
            for h in range(RET_HEADS):
                sl = slice(h * RET_DIM, (h + 1) * RET_DIM)
                o, s_new = _retention_head(qr_ref[rows, sl], kr_ref[rows, sl], vr_ref[rows, sl],
                                           gr_ref[rows, sl], state[h], decay_ref[h], zeta_ref[h],
                                           xi_ref[h], gc_ref[h], gain_ref[:, sl])
                state[h] = s_new
                heads.append(o.astype(BF16))
            project(rows, hp_ref[rows, :], oa, jnp.concatenate(heads, axis=1))

    @pl.when(i == pl.num_programs(0) - 1)
    def _():
        st_ref[...] = state[...]


MIX_PARTS = ROW_TILE // RET_TILE
assert RET_TILE % Q_BLOCK == 0
Q_PER_PART = RET_TILE // Q_BLOCK


def _mix_out(qa, k, v, bias, sink, qr, kr, vr, gr, consts, gain, oa_s, or_s, hp, hs, w, g, layer):
    g_chunk, decay, zeta, xi = consts
    per = ROW_TILE // WINDOW
    prev = pl.BlockSpec((WINDOW, KV_WIDTH),
                        lambda i: (jnp.maximum(jnp.maximum(i - 1, 0) * per - 1, 0), 0))
    row = _stacked_rows
    st_shape = (RET_HEADS, RET_DIM, RET_DIM)
    return pl.pallas_call(
        _mix_out_kernel,
        grid=(N_TOK // ROW_TILE,),
        in_specs=[pl.BlockSpec(memory_space=pltpu.SMEM),
                  _prompt_rows(ATT_WIDTH), _prompt_rows(KV_WIDTH), prev,
                  _prompt_rows(KV_WIDTH), prev, _resident(bias.shape),
                  _resident_layer(sink.shape, layer),
                  _prompt_rows(RET_WIDTH), _prompt_rows(RET_WIDTH), _prompt_rows(RET_WIDTH),
                  _prompt_rows(RET_WIDTH), _resident(decay.shape), _resident(zeta.shape),
                  _resident(xi.shape), _resident(gain.shape),
                  _sample_rows(ATT_WIDTH), _sample_rows(RET_WIDTH),
                  _prompt_rows(D_MODEL), _sample_rows(D_MODEL),
                  _resident(w.shape), _resident_layer(g.shape, layer)],
        out_specs=(row(D_MODEL), row(D_MODEL), pl.BlockSpec(st_shape, lambda i: (0, 0, 0))),
        out_shape=(jax.ShapeDtypeStruct((N_TOK, D_MODEL), F32),
                   jax.ShapeDtypeStruct((N_TOK, D_MODEL), BF16),
                   jax.ShapeDtypeStruct(st_shape, F32)),
        scratch_shapes=[pltpu.VMEM((N_KV, 2, ROW_TILE + WINDOW, PAIR), BF16),
                        pltpu.VMEM((N_KV, 2, ROW_TILE + WINDOW, PAIR), BF16),
                        pltpu.VMEM(st_shape, F32)],
        compiler_params=_params("arbitrary", vmem_limit=MIX_OUT_VMEM_LIMIT),
        name="mix_out",
    )(g_chunk, qa, k, k, v, v, bias, sink, qr, kr, vr, gr, decay, zeta, xi, gain,
      oa_s, or_s, hp, hs, w, g)


def _mlp_kernel(xn_ref, wu_ref, wd_ref, *refs):
    n_extra = (len(refs) - 1) // 2
    o_ref = refs[n_extra]

    @pl.when(pl.program_id(1) == 0)
    def _():
        o_ref[...] = jnp.zeros(o_ref.shape, F32)

    up = jnp.dot(xn_ref[...], wu_ref[...].astype(BF16), preferred_element_type=F32)
    act = jnp.square(jnp.maximum(up, 0.0)).astype(BF16)
    o_ref[...] += jnp.dot(act, wd_ref[...].astype(BF16), preferred_element_type=F32)
    for src, dst in zip(refs[:n_extra], refs[n_extra + 1:]):
        dst[...] = src[...].astype(BF16)


def _mlp(xn, w_up, w_down, layer, cast_jobs=()):
    tm, tf = MLP_ROW_TILE, MLP_FF_TILE
    n_ff = D_FF // tf
    steps = (N_TOK // tm) * n_ff
    slab_in, slab_out, slab_shape = [], [], []
    for w, lyr in cast_jobs:
        _, k, n = w.shape
        rows = k // steps
        assert rows * steps == k and rows % 16 == 0
        slab_in.append(pl.BlockSpec((None, rows, n), lambda i, f, lyr=lyr: (lyr, i * n_ff + f, 0)))
        slab_out.append(pl.BlockSpec((rows, n), lambda i, f: (i * n_ff + f, 0)))
        slab_shape.append(jax.ShapeDtypeStruct((k, n), BF16))
    return pl.pallas_call(
        _mlp_kernel,
        grid=(N_TOK // tm, n_ff),
        in_specs=[pl.BlockSpec((tm, D_MODEL), lambda i, f: (i, 0)),
                  pl.BlockSpec((None, D_MODEL, tf), lambda i, f: (layer, 0, f)),
                  pl.BlockSpec((None, tf, D_MODEL), lambda i, f: (layer, f, 0))] + slab_in,
        out_specs=[pl.BlockSpec((tm, D_MODEL), lambda i, f: (i, 0))] + slab_out,
        out_shape=[jax.ShapeDtypeStruct((N_TOK, D_MODEL), F32)] + slab_shape,
        compiler_params=_params("arbitrary", "arbitrary"),
        name="mlp",
    )(xn, w_up, w_down, *[w for w, _ in cast_jobs])


def _gate_kernel(h1_ref, m_ref, pp_ref, ps_ref, wg_ref, wp_ref, g_ref, gf_ref, op_ref, os_ref,
                 *, final):
    for rows in _row_parts():
        h2 = h1_ref[rows, :] + m_ref[rows, :]
        xn = _rms(h2, g_ref[...]).astype(BF16)
        gate = jax.nn.sigmoid(jnp.dot(xn, wg_ref[...], preferred_element_type=F32))
        emb = jnp.dot(_pick_rows(pp_ref, ps_ref, rows).astype(BF16), wp_ref[...],
                      preferred_element_type=F32)
        h3 = h2 + gate * emb
        op_ref[rows, :] = _rms(h3, gf_ref[...]) if final else h3

    @pl.when(pl.program_id(0) == 0)
    def _():
        os_ref[...] = op_ref[...]


def _gate(h1, mlp_out, p_prompt, p_sample, layer, w_gate, w_ple, g, g_final, final):
    row = _stacked_rows
    return pl.pallas_call(
        functools.partial(_gate_kernel, final=final),
        grid=(N_TOK // ROW_TILE,),
        in_specs=[row(D_MODEL), row(D_MODEL),
                  _prompt_rows(PLE_DIM, layer, 0), _sample_rows(PLE_DIM, layer),
                  _resident(w_gate.shape), _resident_layer(w_ple.shape, layer),
                  _resident_layer(g.shape, layer), _resident(g_final.shape)],
        out_specs=(_prompt_rows(D_MODEL), _sample_rows(D_MODEL, single_buffer=False)),
        out_shape=(jax.ShapeDtypeStruct((SEQ, D_MODEL), F32),
                   jax.ShapeDtypeStruct((N_SAMPLE, D_MODEL), F32)),
        compiler_params=_params("arbitrary"),
        name="gate",
    )(h1, mlp_out, p_prompt, p_sample, w_gate, w_ple, g, g_final)


def _t5_bucket(rel):
    nb = N_BUCKETS // 2
    max_exact = nb // 2
    ret = jnp.where(rel > 0, nb, 0)
    n = jnp.abs(rel)
    nf = jnp.maximum(n, max_exact).astype(F32)
    large = max_exact + (jnp.log(nf / max_exact) / math.log(MAX_DISTANCE / max_exact)
                         * (nb - max_exact)).astype(jnp.int32)
    large = jnp.minimum(large, nb - 1)
    return ret + jnp.where(n < max_exact, n, large)


def _rotary_tables():
    half = RET_DIM // 2
    inv = ROPE_BASE ** (-jnp.arange(half, dtype=F32) / half)

    def cos_sin(pos):
        ang = pos.astype(F32)[:, None] * inv[None, :]
        return jnp.cos(ang), jnp.sin(ang)

    ca, sa = cos_sin(CHUNK * jnp.arange(SEQ // CHUNK, dtype=jnp.int32))
    cb, sb = cos_sin(jnp.arange(CHUNK, dtype=jnp.int32))
    cos_p = (ca[:, None] * cb[None] - sa[:, None] * sb[None]).reshape(SEQ, half)
    sin_p = (sa[:, None] * cb[None] + ca[:, None] * sb[None]).reshape(SEQ, half)
    cos_s, sin_s = cos_sin(PAST_LEN + jnp.arange(DEC_SEQ, dtype=jnp.int32))
    cos = jnp.concatenate([cos_p, jnp.tile(cos_s, (DEC_BATCH, 1))])
    sin = jnp.concatenate([sin_p, jnp.tile(sin_s, (DEC_BATCH, 1))])
    return jnp.concatenate([cos, cos], axis=-1), jnp.concatenate([-sin, sin], axis=-1)


def kernel(x_prompt, x_sample, cache_k, cache_v, state_ret, p_prompt, p_sample, rel_bias,
           norm_mix, w_in, attn_sink, ret_gain, w_out, norm_mlp, w_up, w_down, norm_ple,
           w_gate, w_ple, norm_final):
    q_idx = jnp.arange(Q_BLOCK, dtype=jnp.int32)[:, None]
    s_idx = jnp.arange(K_BLOCK, dtype=jnp.int32)[None, :]
    bucket_p = _t5_bucket(s_idx - WINDOW - q_idx)
    bucket_s = _t5_bucket(jnp.arange(WINDOW + DEC_SEQ, dtype=jnp.int32)[None, :] - WINDOW
                          - jnp.arange(DEC_SEQ, dtype=jnp.int32)[:, None])
    back = q_idx // CHUNK + WINDOW // CHUNK - s_idx // CHUNK
    seen = jnp.logical_and(back >= 0, back <= WINDOW // CHUNK)
    bucket_p = jnp.where(seen, bucket_p, -1)
    bucket_s = jnp.pad(bucket_s, ((0, 0), (0, SAMPLE_KEYS - WINDOW - DEC_SEQ)), constant_values=-1)
    bias_p, bias_s = _rel_bias(rel_bias, bucket_p, bucket_s)

    cos2, sin2 = _rotary_tables()
    consts_p = _retention_consts(RET_TILE)
    consts_s = _retention_consts(DEC_SEQ)

    w_in_bf = {0: w_in[0].astype(BF16)}
    w_out_bf = {0: w_out[0].astype(BF16)}
    w_gate_bf = {}
    w_ple = w_ple.astype(BF16)
    norm_mix, norm_mlp, norm_ple = (g.reshape(DEPTH, 1, D_MODEL)
                                    for g in (norm_mix, norm_mlp, norm_ple))
    hp = x_prompt.reshape(SEQ, D_MODEL)
    hs = x_sample.reshape(N_SAMPLE, D_MODEL)
    p_sample = p_sample.reshape(DEPTH, N_SAMPLE, PLE_DIM)
    k_p, v_p, s_p, k_s, v_s = [], [], [], [], []
    state_s = None
    sink = attn_sink.astype(F32).reshape(DEPTH, N_HEADS, 1, 1) * LOG2E
    sink_p = _pair_layout(jnp.broadcast_to(sink, (DEPTH, N_HEADS, Q_BLOCK, PAIR)))
    sink_s = _pair_layout(jnp.broadcast_to(sink, (DEPTH, N_HEADS, DEC_SEQ, PAIR)))
    for l in range(DEPTH):
        gain = ret_gain[l].reshape(1, RET_WIDTH)

        qa, k, v, qr, kr, vr, gr = _in_proj(hp, hs, norm_mix, w_in_bf[l], l, cos2, sin2)
        oa_s = _attn_sample(qa, k, v, cache_k[l].reshape(DEC_BATCH, WINDOW, KV_WIDTH),
                            cache_v[l].reshape(DEC_BATCH, WINDOW, KV_WIDTH), bias_s, sink_s, l)
        or_s, state_s = _ret_sample(qr, kr, vr, gr, state_ret, l, consts_s, gain, state_s)
        h1, xn, st_p = _mix_out(qa, k, v, bias_p, sink_p, qr, kr, vr, gr, consts_p, gain,
                                oa_s, or_s, hp, hs, w_out_bf[l], norm_mlp, l)
        jobs = [(w_gate, l)]
        if l + 1 < DEPTH:
            jobs += [(w_in, l + 1), (w_out, l + 1)]
        mlp_out, *cast = _mlp(xn, w_up, w_down, l, jobs)
        w_gate_bf[l] = cast[0]
        if l + 1 < DEPTH:
            w_in_bf[l + 1], w_out_bf[l + 1] = cast[1], cast[2]
        hp, hs = _gate(h1, mlp_out, p_prompt, p_sample, l, w_gate_bf[l], w_ple, norm_ple,
                       norm_final.reshape(1, D_MODEL), final=(l == DEPTH - 1))

        k_p.append(k[SEQ - WINDOW:SEQ].reshape(1, WINDOW, N_KV, HEAD_DIM))
        v_p.append(v[SEQ - WINDOW:SEQ].reshape(1, WINDOW, N_KV, HEAD_DIM))
        s_p.append(st_p[None])
        k_s.append(k[SEQ:].reshape(DEC_BATCH, DEC_SEQ, N_KV, HEAD_DIM))
        v_s.append(v[SEQ:].reshape(DEC_BATCH, DEC_SEQ, N_KV, HEAD_DIM))

    return (hp.reshape(1, SEQ, D_MODEL), hs.reshape(DEC_BATCH, DEC_SEQ, D_MODEL),
            jnp.stack(k_p), jnp.stack(v_p), jnp.stack(s_p),
            jnp.stack(k_s), jnp.stack(v_s), state_s)
```

```python
import functools
import math

import jax
import jax.numpy as jnp
from jax import lax
from jax.experimental import pallas as pl
from jax.experimental.pallas import tpu as pltpu

F32 = jnp.float32
BF16 = jnp.bfloat16

D_MODEL = 2048
SEQ = 8192
DEPTH = 2
DEC_BATCH = 32
DEC_SEQ = 16
PAST_LEN = 4096
N_SAMPLE = DEC_BATCH * DEC_SEQ
N_TOK = SEQ + N_SAMPLE

CHUNK = 64
WINDOW = 128
HEAD_DIM = 64
N_HEADS = 16
N_KV = 4
GROUP = N_HEADS // N_KV
ATT_WIDTH = N_HEADS * HEAD_DIM
KV_WIDTH = N_KV * HEAD_DIM
RET_DIM = 128
RET_HEADS = 8
RET_WIDTH = RET_HEADS * RET_DIM
IN_WIDTH = ATT_WIDTH + 2 * KV_WIDTH + 4 * RET_WIDTH
D_FF = 4 * D_MODEL
PLE_DIM = 256
N_BUCKETS = 32
MAX_DISTANCE = 128
ROPE_BASE = 10000.0
EPS = 1e-6
ATT_SCALE = HEAD_DIM ** -0.5
LOG2E = 1.0 / math.log(2.0)
RET_SCALE = RET_DIM ** -0.5

ROW_TILE = 512
ROW_PARTS = 2
MLP_ROW_TILE = 1088
MLP_FF_TILE = 512
Q_BLOCK = 2 * CHUNK
K_BLOCK = WINDOW + Q_BLOCK
RET_TILE = 256
SAMPLE_BLOCK = 8
SAMPLE_KEYS = 256
VMEM_LIMIT = 58 * 1024 * 1024
MIX_OUT_VMEM_LIMIT = 62 * 1024 * 1024

NT_DIMS = (((1,), (1,)), ((), ()))
TN_DIMS = (((0,), (0,)), ((), ()))


def _params(*sem, vmem_limit=VMEM_LIMIT):
    return pltpu.CompilerParams(dimension_semantics=sem, vmem_limit_bytes=vmem_limit)


def _resident(shape):
    nd = len(shape)
    return pl.BlockSpec(shape, lambda *_: (0,) * nd, pipeline_mode=pl.Buffered(1))


def _resident_layer(shape, layer):
    nd = len(shape) - 1
    return pl.BlockSpec((None,) + tuple(shape[1:]), lambda *_: (layer,) + (0,) * nd,
                        pipeline_mode=pl.Buffered(1))


def _rms(x, g):
    ms = jnp.mean(x * x, axis=-1, keepdims=True)
    return x * lax.rsqrt(ms + EPS) * g


N_PROMPT_TILES = SEQ // ROW_TILE
assert N_SAMPLE == ROW_TILE


def _stacked_rows(width):
    return pl.BlockSpec((ROW_TILE, width),
                        lambda i: ((i + N_PROMPT_TILES) % (N_PROMPT_TILES + 1), 0))


def _prompt_rows(width, *lead):
    none = (None,) * len(lead)
    return pl.BlockSpec(none + (ROW_TILE, width), lambda i: lead + (jnp.maximum(i - 1, 0), 0))


def _sample_rows(width, *lead, single_buffer=True):
    none = (None,) * len(lead)
    mode = dict(pipeline_mode=pl.Buffered(1)) if single_buffer else {}
    return pl.BlockSpec(none + (ROW_TILE, width), lambda i: lead + (0, 0), **mode)


def _row_parts(parts=ROW_PARTS):
    part = ROW_TILE // parts
    return [slice(r * part, (r + 1) * part) for r in range(parts)]


def _pick_rows(prompt_ref, sample_ref, rows=slice(None)):
    return jnp.where(pl.program_id(0) == 0, sample_ref[rows, :], prompt_ref[rows, :])


def _bias_kernel(table_ref, bp_ref, bs_ref, op_ref, os_ref):
    bp = bp_ref[...]
    bs = bs_ref[...]

    def lookup(bucket, head):
        acc = jnp.full(bucket.shape, -jnp.inf, F32)
        for b in range(N_BUCKETS):
            acc = jnp.where(bucket == b, table_ref[b, head] * LOG2E, acc)
        return acc

    def per_kv(kv, c):
        for half in range(2):
            for side in range(2):
                head = kv * GROUP + 2 * half + side
                rp, cp = bp.shape
                op_ref[kv, half * rp:(half + 1) * rp, side * cp:(side + 1) * cp] = lookup(bp, head)
                rs, cs = bs.shape
                os_ref[kv, half * rs:(half + 1) * rs, side * cs:(side + 1) * cs] = lookup(bs, head)
        return c

    lax.fori_loop(0, N_KV, per_kv, 0)


def _rel_bias(table, bucket_p, bucket_s):
    def out(bucket):
        rows, cols = bucket.shape
        return jax.ShapeDtypeStruct((N_KV, 2 * rows, 2 * cols), F32)

    return pl.pallas_call(
        _bias_kernel,
        out_shape=(out(bucket_p), out(bucket_s)),
        in_specs=[pl.BlockSpec(memory_space=pltpu.SMEM),
                  pl.BlockSpec(memory_space=pltpu.VMEM),
                  pl.BlockSpec(memory_space=pltpu.VMEM)],
        out_specs=(pl.BlockSpec(memory_space=pltpu.VMEM), pl.BlockSpec(memory_space=pltpu.VMEM)),
        name="rel_bias",
    )(table, bucket_p, bucket_s)


def _in_proj_kernel(hp_ref, hs_ref, g_ref, w_ref, cos_ref, sin_ref,
                    qa_ref, k_ref, v_ref, qr_ref, kr_ref, vr_ref, gr_ref):
    base = ATT_WIDTH + 2 * KV_WIDTH
    for rows in _row_parts():
        xn = _rms(_pick_rows(hp_ref, hs_ref, rows), g_ref[...]).astype(BF16)

        def proj(lo, width):
            return jnp.dot(xn, w_ref[:, lo:lo + width], preferred_element_type=F32)

        qa_ref[rows, :] = (proj(0, ATT_WIDTH) * (ATT_SCALE * LOG2E)).astype(BF16)
        k_ref[rows, :] = proj(ATT_WIDTH, KV_WIDTH)
        v_ref[rows, :] = proj(ATT_WIDTH + KV_WIDTH, KV_WIDTH)
        cos = cos_ref[rows, :]
        sin = sin_ref[rows, :]

        def rot(x):
            return x * cos + pltpu.roll(x, RET_DIM // 2, 1) * sin

        q = proj(base, RET_WIDTH)
        for h in range(RET_HEADS):
            sl = slice(h * RET_DIM, (h + 1) * RET_DIM)
            qr_ref[rows, sl] = rot(q[:, sl]).astype(BF16)
        k = proj(base + RET_WIDTH, RET_WIDTH)
        for h in range(RET_HEADS):
            sl = slice(h * RET_DIM, (h + 1) * RET_DIM)
            kr_ref[rows, sl] = (rot(k[:, sl]) * RET_SCALE).astype(BF16)
        vr_ref[rows, :] = proj(base + 2 * RET_WIDTH, RET_WIDTH).astype(BF16)
        gr_ref[rows, :] = proj(base + 3 * RET_WIDTH, RET_WIDTH).astype(BF16)


def _in_proj(hp, hs, g, w, layer, cos2, sin2):
    tm = ROW_TILE
    row = _stacked_rows
    out = lambda width, dt: jax.ShapeDtypeStruct((N_TOK, width), dt)
    return pl.pallas_call(
        _in_proj_kernel,
        grid=(N_TOK // tm,),
        in_specs=[_prompt_rows(D_MODEL), _sample_rows(D_MODEL),
                  _resident_layer(g.shape, layer), _resident(w.shape),
                  row(RET_DIM), row(RET_DIM)],
        out_specs=(row(ATT_WIDTH), row(KV_WIDTH), row(KV_WIDTH),
                   row(RET_WIDTH), row(RET_WIDTH), row(RET_WIDTH), row(RET_WIDTH)),
        out_shape=(out(ATT_WIDTH, BF16), out(KV_WIDTH, F32), out(KV_WIDTH, F32),
                   out(RET_WIDTH, BF16), out(RET_WIDTH, BF16), out(RET_WIDTH, BF16),
                   out(RET_WIDTH, BF16)),
        compiler_params=_params("parallel"),
        name="in_proj",
    )(hp, hs, g, w, cos2, sin2)


PAIR = 2 * HEAD_DIM


def _attend_pairs(qst, kcat, vcat, bias, sinks):
    half = kcat.shape[0] // 2
    side0 = lax.broadcasted_iota(jnp.int32, vcat.shape, 0) < half
    low = lax.broadcasted_iota(jnp.int32, vcat.shape, 1) < HEAD_DIM
    vext = jnp.concatenate([vcat, jnp.where(side0 == low, 1.0, 0.0).astype(BF16)], axis=1)
    s = lax.dot_general(qst, kcat, NT_DIMS, preferred_element_type=F32) + bias
    ps, es = [], []
    for side in range(2):
        ss = s[:, side * half:(side + 1) * half]
        m = jnp.maximum(jnp.max(ss, axis=-1, keepdims=True), sinks[side])
        ps.append(jnp.exp2(ss - jnp.concatenate([m] * (half // PAIR), axis=1)))
        es.append(jnp.exp2(sinks[side] - m))
    p = jnp.concatenate(ps, axis=1).astype(BF16)
    r = jnp.dot(p, vext, preferred_element_type=F32)
    lo = lax.broadcasted_iota(jnp.int32, qst.shape, 1) < HEAD_DIM
    den = r[:, PAIR:] + jnp.where(lo, es[0], es[1])
    return r[:, :PAIR] / den


def _fill_padded(dst, rows, x):
    lo = lax.broadcasted_iota(jnp.int32, (x.shape[0], PAIR), 1) < HEAD_DIM
    for pair in range(N_KV // 2):
        xp = x[:, pair * PAIR:(pair + 1) * PAIR]
        xr = pltpu.roll(xp, HEAD_DIM, 1)
        zero = jnp.zeros_like(xp)
        dst[2 * pair, 0, rows, :] = jnp.where(lo, xp, zero).astype(BF16)
        dst[2 * pair, 1, rows, :] = jnp.where(lo, zero, xr).astype(BF16)
        dst[2 * pair + 1, 0, rows, :] = jnp.where(lo, xr, zero).astype(BF16)
        dst[2 * pair + 1, 1, rows, :] = jnp.where(lo, zero, xp).astype(BF16)


def _pair_layout(x):
    lead = x.shape[:-3]
    rows, cols = x.shape[-2:]
    x = x.reshape(lead + (N_KV, 2, 2, rows, cols))
    x = jnp.swapaxes(x, -4, -3)
    return x.reshape(lead + (N_KV, 2, 2 * rows, cols))


def _retention_head(q, k, v, g, state, decay, zeta, xi, g_chunk, gain):
    scores = lax.dot_general(q, k, NT_DIMS, preferred_element_type=F32) * decay
    if q.shape[0] % RET_DIM == 0:
        lhs = jnp.concatenate([(q.astype(F32) * xi).astype(BF16), scores.astype(BF16)], axis=1)
        rhs = jnp.concatenate([state.astype(BF16), v], axis=0)
        o = jnp.dot(lhs, rhs, preferred_element_type=F32)
    else:
        o = jnp.dot(scores.astype(BF16), v, preferred_element_type=F32)
        o = o + jnp.dot(q, state.astype(BF16), preferred_element_type=F32) * xi
    kz = (k.astype(F32) * zeta).astype(BF16)
    upd = lax.dot_general(kz, v, TN_DIMS, preferred_element_type=F32)
    new_state = g_chunk * state + upd
    o = o * lax.rsqrt(jnp.mean(o * o, axis=-1, keepdims=True) + EPS) * gain
    gf = g.astype(F32)
    return gf * jax.nn.sigmoid(gf) * o, new_state


def _mix_sample_kernel(gc_ref, q_ref, kn_ref, vn_ref, ck_ref, cv_ref, bias_ref, sink_ref,
                       qr_ref, kr_ref, vr_ref, gr_ref, s_ref, decay_ref, zeta_ref, xi_ref,
                       gain_ref, *refs, opening):
    oa_ref, or_ref, st_ref, kz, vz = refs[-5:]
    if opening:
        st_ref[1:] = jnp.zeros((DEPTH - 1,) + st_ref.shape[1:], F32)
        st_ref = st_ref.at[0]
    keys = WINDOW + DEC_SEQ
    pad = jnp.zeros((SAMPLE_BLOCK, N_KV, 2, SAMPLE_KEYS - keys, PAIR), BF16)
    kz[:, :, :, keys:, :] = pad
    vz[:, :, :, keys:, :] = pad
    for b in range(SAMPLE_BLOCK):
        rows = slice(b * DEC_SEQ, (b + 1) * DEC_SEQ)
        _fill_padded(kz.at[b], slice(0, WINDOW), ck_ref[b])
        _fill_padded(kz.at[b], slice(WINDOW, keys), kn_ref[rows, :])
        _fill_padded(vz.at[b], slice(0, WINDOW), cv_ref[b])
        _fill_padded(vz.at[b], slice(WINDOW, keys), vn_ref[rows, :])
    for b in range(SAMPLE_BLOCK):
        rows = slice(b * DEC_SEQ, (b + 1) * DEC_SEQ)
        for kv in range(N_KV):
            c0 = kv * GROUP * HEAD_DIM
            qst = jnp.concatenate([q_ref[rows, c0:c0 + PAIR],
                                   q_ref[rows, c0 + PAIR:c0 + 2 * PAIR]], axis=0)
            o = _attend_pairs(qst, kz[b, kv].reshape(2 * SAMPLE_KEYS, PAIR),
                              vz[b, kv].reshape(2 * SAMPLE_KEYS, PAIR),
                              bias_ref[kv], [sink_ref[kv, side] for side in range(2)])
            oa_ref[rows, c0:c0 + PAIR] = o[0:DEC_SEQ].astype(BF16)
            oa_ref[rows, c0 + PAIR:c0 + 2 * PAIR] = o[DEC_SEQ:].astype(BF16)
        for h in range(RET_HEADS):
            sl = slice(h * RET_DIM, (h + 1) * RET_DIM)
            o, s_new = _retention_head(qr_ref[rows, sl], kr_ref[rows, sl], vr_ref[rows, sl],
                                       gr_ref[rows, sl], s_ref[b, h], decay_ref[h], zeta_ref[h],
                                       xi_ref[h], gc_ref[h], gain_ref[:, sl])
            st_ref[b, h] = s_new
            or_ref[rows, sl] = o.astype(BF16)


def _mix_sample(qa, k, v, cache_k, cache_v, bias, sink, qr, kr, vr, gr, state, layer, consts,
                gain, new_state):
    g_chunk, decay, zeta, xi = consts
    nb = SAMPLE_BLOCK
    rows = nb * DEC_SEQ
    first = SEQ // rows
    new = lambda width: pl.BlockSpec((rows, width), lambda i: (first + i, 0))
    out = lambda width: pl.BlockSpec((rows, width), lambda i: (i, 0))
    cache = pl.BlockSpec((nb, WINDOW, KV_WIDTH), lambda i: (i, 0, 0))
    st_block = (nb, RET_HEADS, RET_DIM, RET_DIM)
    st_spec = pl.BlockSpec((None,) + st_block, lambda i: (layer, i, 0, 0, 0))
    in_specs = [pl.BlockSpec(memory_space=pltpu.SMEM),
                new(ATT_WIDTH), new(KV_WIDTH), new(KV_WIDTH), cache, cache,
                _resident(bias.shape), _resident_layer(sink.shape, layer),
                new(RET_WIDTH), new(RET_WIDTH), new(RET_WIDTH), new(RET_WIDTH), st_spec,
                _resident(decay.shape), _resident(zeta.shape), _resident(xi.shape),
                _resident(gain.shape)]
    args = [g_chunk, qa, k, v, cache_k, cache_v, bias, sink, qr, kr, vr, gr, state,
            decay, zeta, xi, gain]
    opening = new_state is None
    if opening:
        assert layer == 0
        aliases = {}
        st_out = pl.BlockSpec((DEPTH,) + st_block, lambda i: (0, i, 0, 0, 0))
    else:
        args.append(new_state)
        in_specs.append(pl.BlockSpec(memory_space=pl.ANY))
        aliases = {len(args) - 1: 2}
        st_out = st_spec
    return pl.pallas_call(
        functools.partial(_mix_sample_kernel, opening=opening),
        grid=(DEC_BATCH // nb,),
        in_specs=in_specs,
        out_specs=(out(ATT_WIDTH), out(RET_WIDTH), st_out),
        out_shape=(jax.ShapeDtypeStruct((N_SAMPLE, ATT_WIDTH), BF16),
                   jax.ShapeDtypeStruct((N_SAMPLE, RET_WIDTH), BF16),
                   jax.ShapeDtypeStruct(state.shape, F32)),
        scratch_shapes=[pltpu.VMEM((nb, N_KV, 2, SAMPLE_KEYS, PAIR), BF16)] * 2,
        input_output_aliases=aliases,
        compiler_params=_params("parallel"),
        name="mix_sample",
    )(*args)


def _retention_consts(chunk):
    lg = jnp.log(1.0 - 2.0 ** (-5.0 - jnp.arange(RET_HEADS, dtype=F32)))
    idx = jnp.arange(chunk, dtype=F32)
    diff = idx[:, None] - idx[None, :]
    decay = jnp.where(diff >= 0, jnp.exp(lg[:, None, None] * jnp.maximum(diff, 0.0)), 0.0)
    zeta = jnp.exp(lg[:, None] * (chunk - 1 - idx))[:, :, None]
    xi = jnp.exp(lg[:, None] * (idx + 1))[:, :, None]
    dense = (RET_HEADS, chunk, RET_DIM)
    return (jnp.exp(lg * chunk), decay,
            jnp.broadcast_to(zeta, dense), jnp.broadcast_to(xi, dense))


def _mix_out_kernel(gc_ref, q_ref, kc_ref, kp_ref, vc_ref, vp_ref, bias_ref, sink_ref,
                    qr_ref, kr_ref, vr_ref, gr_ref, decay_ref, zeta_ref, xi_ref, gain_ref,
                    oas_ref, ors_ref, hp_ref, hs_ref, w_ref, g_ref,
                    h1_ref, xn_ref, st_ref, kz, vz, state):
    i = pl.program_id(0)

    def project(rows, h, oa, orr):
        h1 = (h + jnp.dot(oa, w_ref[0:ATT_WIDTH, :], preferred_element_type=F32)
              + jnp.dot(orr, w_ref[ATT_WIDTH:, :], preferred_element_type=F32))
        h1_ref[rows, :] = h1
        xn_ref[rows, :] = _rms(h1, g_ref[...]).astype(BF16)

    @pl.when(i == 0)
    def _():
        state[...] = jnp.zeros(state.shape, F32)
        for rows in _row_parts():
            project(rows, hs_ref[rows, :], oas_ref[rows, :], ors_ref[rows, :])

    @pl.when(i > 0)
    def _():
        _fill_padded(kz, slice(0, WINDOW), kp_ref[...])
        _fill_padded(kz, slice(WINDOW, WINDOW + ROW_TILE), kc_ref[...])
        _fill_padded(vz, slice(0, WINDOW), vp_ref[...])
        _fill_padded(vz, slice(WINDOW, WINDOW + ROW_TILE), vc_ref[...])
        col = lax.broadcasted_iota(jnp.int32, (2 * Q_BLOCK, 2 * K_BLOCK), 1)
        before_start = jnp.logical_and(i == 1, col % K_BLOCK < WINDOW)
        for part, rows in enumerate(_row_parts(MIX_PARTS)):
            blocks = []
            for j in range(part * Q_PER_PART, (part + 1) * Q_PER_PART):
                qrows = slice(j * Q_BLOCK, (j + 1) * Q_BLOCK)
                band = slice(j * Q_BLOCK, j * Q_BLOCK + K_BLOCK)
                pieces = []
                for kv in range(N_KV):
                    c0 = kv * GROUP * HEAD_DIM
                    qst = jnp.concatenate([q_ref[qrows, c0:c0 + PAIR],
                                           q_ref[qrows, c0 + PAIR:c0 + 2 * PAIR]], axis=0)
                    bias = bias_ref[kv]
                    if j == 0:
                        bias = jnp.where(before_start, -jnp.inf, bias)
                    o = _attend_pairs(
                        qst,
                        jnp.concatenate([kz[kv, side, band, :] for side in range(2)], axis=0),
                        jnp.concatenate([vz[kv, side, band, :] for side in range(2)], axis=0),
                        bias, [sink_ref[kv, side] for side in range(2)])
                    pieces += [o[0:Q_BLOCK].astype(BF16), o[Q_BLOCK:].astype(BF16)]
                blocks.append(jnp.concatenate(pieces, axis=1))
            oa = jnp.concatenate(blocks, axis=0)
            heads = []
            for h in range(RET_HEADS):
                sl = slice(h * RET_DIM, (h + 1) * RET_DIM)
                o, s_new = _retention_head(qr_ref[rows, sl], kr_ref[rows, sl], vr_ref[rows, sl],
                                           gr_ref[rows, sl], state[h], decay_ref[h], zeta_ref[h],
                                           xi_ref[h], gc_ref[h], gain_ref[:, sl])
                state[h] = s_new
                heads.append(o.astype(BF16))
            project(rows, hp_ref[rows, :], oa, jnp.concatenate(heads, axis=1))

    @pl.when(i == pl.num_programs(0) - 1)
    def _():
        st_ref[...] = state[...]


MIX_PARTS = ROW_TILE // RET_TILE
assert RET_TILE % Q_BLOCK == 0
Q_PER_PART = RET_TILE // Q_BLOCK


def _mix_out(qa, k, v, bias, sink, qr, kr, vr, gr, consts, gain, oa_s, or_s, hp, hs, w, g, layer):
    g_chunk, decay, zeta, xi = consts
    per = ROW_TILE // WINDOW
    prev = pl.BlockSpec((WINDOW, KV_WIDTH),
                        lambda i: (jnp.maximum(jnp.maximum(i - 1, 0) * per - 1, 0), 0))
    row = _stacked_rows
    st_shape = (RET_HEADS, RET_DIM, RET_DIM)
    return pl.pallas_call(
        _mix_out_kernel,
        grid=(N_TOK // ROW_TILE,),
        in_specs=[pl.BlockSpec(memory_space=pltpu.SMEM),
                  _prompt_rows(ATT_WIDTH), _prompt_rows(KV_WIDTH), prev,
                  _prompt_rows(KV_WIDTH), prev, _resident(bias.shape),
                  _resident_layer(sink.shape, layer),
                  _prompt_rows(RET_WIDTH), _prompt_rows(RET_WIDTH), _prompt_rows(RET_WIDTH),
                  _prompt_rows(RET_WIDTH), _resident(decay.shape), _resident(zeta.shape),
                  _resident(xi.shape), _resident(gain.shape),
                  _sample_rows(ATT_WIDTH), _sample_rows(RET_WIDTH),
                  _prompt_rows(D_MODEL), _sample_rows(D_MODEL),
                  _resident(w.shape), _resident_layer(g.shape, layer)],
        out_specs=(row(D_MODEL), row(D_MODEL), pl.BlockSpec(st_shape, lambda i: (0, 0, 0))),
        out_shape=(jax.ShapeDtypeStruct((N_TOK, D_MODEL), F32),
                   jax.ShapeDtypeStruct((N_TOK, D_MODEL), BF16),
                   jax.ShapeDtypeStruct(st_shape, F32)),
        scratch_shapes=[pltpu.VMEM((N_KV, 2, ROW_TILE + WINDOW, PAIR), BF16),
                        pltpu.VMEM((N_KV, 2, ROW_TILE + WINDOW, PAIR), BF16),
                        pltpu.VMEM(st_shape, F32)],
        compiler_params=_params("arbitrary", vmem_limit=MIX_OUT_VMEM_LIMIT),
        name="mix_out",
    )(g_chunk, qa, k, k, v, v, bias, sink, qr, kr, vr, gr, decay, zeta, xi, gain,
      oa_s, or_s, hp, hs, w, g)


def _mlp_kernel(xn_ref, wu_ref, wd_ref, *refs):
    n_extra = (len(refs) - 1) // 2
    o_ref = refs[n_extra]

    @pl.when(pl.program_id(1) == 0)
    def _():
        o_ref[...] = jnp.zeros(o_ref.shape, F32)

    up = jnp.dot(xn_ref[...], wu_ref[...].astype(BF16), preferred_element_type=F32)
    act = jnp.square(jnp.maximum(up, 0.0)).astype(BF16)
    o_ref[...] += jnp.dot(act, wd_ref[...].astype(BF16), preferred_element_type=F32)
    for src, dst in zip(refs[:n_extra], refs[n_extra + 1:]):
        dst[...] = src[...].astype(BF16)


def _mlp(xn, w_up, w_down, layer, cast_jobs=()):
    tm, tf = MLP_ROW_TILE, MLP_FF_TILE
    n_ff = D_FF // tf
    steps = (N_TOK // tm) * n_ff
    slab_in, slab_out, slab_shape = [], [], []
    for w, lyr in cast_jobs:
        _, k, n = w.shape
        rows = k // steps
        assert rows * steps == k and rows % 16 == 0
        slab_in.append(pl.BlockSpec((None, rows, n), lambda i, f, lyr=lyr: (lyr, i * n_ff + f, 0)))
        slab_out.append(pl.BlockSpec((rows, n), lambda i, f: (i * n_ff + f, 0)))
        slab_shape.append(jax.ShapeDtypeStruct((k, n), BF16))
    return pl.pallas_call(
        _mlp_kernel,
        grid=(N_TOK // tm, n_ff),
        in_specs=[pl.BlockSpec((tm, D_MODEL), lambda i, f: (i, 0)),
                  pl.BlockSpec((None, D_MODEL, tf), lambda i, f: (layer, 0, f)),
                  pl.BlockSpec((None, tf, D_MODEL), lambda i, f: (layer, f, 0))] + slab_in,
        out_specs=[pl.BlockSpec((tm, D_MODEL), lambda i, f: (i, 0))] + slab_out,
        out_shape=[jax.ShapeDtypeStruct((N_TOK, D_MODEL), F32)] + slab_shape,
        compiler_params=_params("arbitrary", "arbitrary"),
        name="mlp",
    )(xn, w_up, w_down, *[w for w, _ in cast_jobs])


def _gate_kernel(h1_ref, m_ref, pp_ref, ps_ref, wg_ref, wp_ref, g_ref, gf_ref, op_ref, os_ref,
                 *, final):
    for rows in _row_parts():
        h2 = h1_ref[rows, :] + m_ref[rows, :]
        xn = _rms(h2, g_ref[...]).astype(BF16)
        gate = jax.nn.sigmoid(jnp.dot(xn, wg_ref[...], preferred_element_type=F32))
        emb = jnp.dot(_pick_rows(pp_ref, ps_ref, rows).astype(BF16), wp_ref[...],
                      preferred_element_type=F32)
        h3 = h2 + gate * emb
        op_ref[rows, :] = _rms(h3, gf_ref[...]) if final else h3

    @pl.when(pl.program_id(0) == 0)
    def _():
        os_ref[...] = op_ref[...]


def _gate(h1, mlp_out, p_prompt, p_sample, layer, w_gate, w_ple, g, g_final, final):
    row = _stacked_rows
    return pl.pallas_call(
        functools.partial(_gate_kernel, final=final),
        grid=(N_TOK // ROW_TILE,),
        in_specs=[row(D_MODEL), row(D_MODEL),
                  _prompt_rows(PLE_DIM, layer, 0), _sample_rows(PLE_DIM, layer),
                  _resident(w_gate.shape), _resident_layer(w_ple.shape, layer),
                  _resident_layer(g.shape, layer), _resident(g_final.shape)],
        out_specs=(_prompt_rows(D_MODEL), _sample_rows(D_MODEL, single_buffer=False)),
        out_shape=(jax.ShapeDtypeStruct((SEQ, D_MODEL), F32),
                   jax.ShapeDtypeStruct((N_SAMPLE, D_MODEL), F32)),
        compiler_params=_params("arbitrary"),
        name="gate",
    )(h1, mlp_out, p_prompt, p_sample, w_gate, w_ple, g, g_final)


def _t5_bucket(rel):
    nb = N_BUCKETS // 2
    max_exact = nb // 2
    ret = jnp.where(rel > 0, nb, 0)
    n = jnp.abs(rel)
    nf = jnp.maximum(n, max_exact).astype(F32)
    large = max_exact + (jnp.log(nf / max_exact) / math.log(MAX_DISTANCE / max_exact)
                         * (nb - max_exact)).astype(jnp.int32)
    large = jnp.minimum(large, nb - 1)
    return ret + jnp.where(n < max_exact, n, large)


def _rotary_tables():
    half = RET_DIM // 2
    inv = ROPE_BASE ** (-jnp.arange(half, dtype=F32) / half)

    def cos_sin(pos):
        ang = pos.astype(F32)[:, None] * inv[None, :]
        return jnp.cos(ang), jnp.sin(ang)

    ca, sa = cos_sin(CHUNK * jnp.arange(SEQ // CHUNK, dtype=jnp.int32))
    cb, sb = cos_sin(jnp.arange(CHUNK, dtype=jnp.int32))
    cos_p = (ca[:, None] * cb[None] - sa[:, None] * sb[None]).reshape(SEQ, half)
    sin_p = (sa[:, None] * cb[None] + ca[:, None] * sb[None]).reshape(SEQ, half)
    cos_s, sin_s = cos_sin(PAST_LEN + jnp.arange(DEC_SEQ, dtype=jnp.int32))
    cos = jnp.concatenate([cos_p, jnp.tile(cos_s, (DEC_BATCH, 1))])
    sin = jnp.concatenate([sin_p, jnp.tile(sin_s, (DEC_BATCH, 1))])
    return jnp.concatenate([cos, cos], axis=-1), jnp.concatenate([-sin, sin], axis=-1)


def kernel(x_prompt, x_sample, cache_k, cache_v, state_ret, p_prompt, p_sample, rel_bias,
           norm_mix, w_in, attn_sink, ret_gain, w_out, norm_mlp, w_up, w_down, norm_ple,
           w_gate, w_ple, norm_final):
    q_idx = jnp.arange(Q_BLOCK, dtype=jnp.int32)[:, None]
    s_idx = jnp.arange(K_BLOCK, dtype=jnp.int32)[None, :]
    bucket_p = _t5_bucket(s_idx - WINDOW - q_idx)
    bucket_s = _t5_bucket(jnp.arange(WINDOW + DEC_SEQ, dtype=jnp.int32)[None, :] - WINDOW
                          - jnp.arange(DEC_SEQ, dtype=jnp.int32)[:, None])
    back = q_idx // CHUNK + WINDOW // CHUNK - s_idx // CHUNK
    seen = jnp.logical_and(back >= 0, back <= WINDOW // CHUNK)
    bucket_p = jnp.where(seen, bucket_p, -1)
    bucket_s = jnp.pad(bucket_s, ((0, 0), (0, SAMPLE_KEYS - WINDOW - DEC_SEQ)), constant_values=-1)
    bias_p, bias_s = _rel_bias(rel_bias, bucket_p, bucket_s)

    cos2, sin2 = _rotary_tables()
    consts_p = _retention_consts(RET_TILE)
    consts_s = _retention_consts(DEC_SEQ)

    w_in_bf = {0: w_in[0].astype(BF16)}
    w_out_bf = {0: w_out[0].astype(BF16)}
    w_gate_bf = {}
    w_ple = w_ple.astype(BF16)
    norm_mix, norm_mlp, norm_ple = (g.reshape(DEPTH, 1, D_MODEL)
                                    for g in (norm_mix, norm_mlp, norm_ple))
    hp = x_prompt.reshape(SEQ, D_MODEL)
    hs = x_sample.reshape(N_SAMPLE, D_MODEL)
    p_sample = p_sample.reshape(DEPTH, N_SAMPLE, PLE_DIM)
    k_p, v_p, s_p, k_s, v_s = [], [], [], [], []
    state_s = None
    sink = attn_sink.astype(F32).reshape(DEPTH, N_HEADS, 1, 1) * LOG2E
    sink_p = _pair_layout(jnp.broadcast_to(sink, (DEPTH, N_HEADS, Q_BLOCK, PAIR)))
    sink_s = _pair_layout(jnp.broadcast_to(sink, (DEPTH, N_HEADS, DEC_SEQ, PAIR)))
    for l in range(DEPTH):
        gain = ret_gain[l].reshape(1, RET_WIDTH)

        qa, k, v, qr, kr, vr, gr = _in_proj(hp, hs, norm_mix, w_in_bf[l], l, cos2, sin2)
        oa_s, or_s, state_s = _mix_sample(
            qa, k, v, cache_k[l].reshape(DEC_BATCH, WINDOW, KV_WIDTH),
            cache_v[l].reshape(DEC_BATCH, WINDOW, KV_WIDTH), bias_s, sink_s,
            qr, kr, vr, gr, state_ret, l, consts_s, gain, state_s)
        h1, xn, st_p = _mix_out(qa, k, v, bias_p, sink_p, qr, kr, vr, gr, consts_p, gain,
                                oa_s, or_s, hp, hs, w_out_bf[l], norm_mlp, l)
        jobs = [(w_gate, l)]
        if l + 1 < DEPTH:
            jobs += [(w_in, l + 1), (w_out, l + 1)]
        mlp_out, *cast = _mlp(xn, w_up, w_down, l, jobs)
        w_gate_bf[l] = cast[0]
        if l + 1 < DEPTH:
            w_in_bf[l + 1], w_out_bf[l + 1] = cast[1], cast[2]
        hp, hs = _gate(h1, mlp_out, p_prompt, p_sample, l, w_gate_bf[l], w_ple, norm_ple,
                       norm_final.reshape(1, D_MODEL), final=(l == DEPTH - 1))

        k_p.append(k[SEQ - WINDOW:SEQ].reshape(1, WINDOW, N_KV, HEAD_DIM))
        v_p.append(v[SEQ - WINDOW:SEQ].reshape(1, WINDOW, N_KV, HEAD_DIM))
        s_p.append(st_p[None])
        k_s.append(k[SEQ:].reshape(DEC_BATCH, DEC_SEQ, N_KV, HEAD_DIM))
        v_s.append(v[SEQ:].reshape(DEC_BATCH, DEC_SEQ, N_KV, HEAD_DIM))

    return (hp.reshape(1, SEQ, D_MODEL), hs.reshape(DEC_BATCH, DEC_SEQ, D_MODEL),
            jnp.stack(k_p), jnp.stack(v_p), jnp.stack(s_p),
            jnp.stack(k_s), jnp.stack(v_s), state_s)
```

```python
import functools
import math

import jax
import jax.numpy as jnp
from jax import lax
from jax.experimental import pallas as pl
from jax.experimental.pallas import tpu as pltpu

F32 = jnp.float32
BF16 = jnp.bfloat16

D_MODEL = 2048
SEQ = 8192
DEPTH = 2
DEC_BATCH = 32
DEC_SEQ = 16
PAST_LEN = 4096
N_SAMPLE = DEC_BATCH * DEC_SEQ
N_TOK = SEQ + N_SAMPLE

CHUNK = 64
WINDOW = 128
HEAD_DIM = 64
N_HEADS = 16
N_KV = 4
GROUP = N_HEADS // N_KV
ATT_WIDTH = N_HEADS * HEAD_DIM
KV_WIDTH = N_KV * HEAD_DIM
RET_DIM = 128
RET_HEADS = 8
RET_WIDTH = RET_HEADS * RET_DIM
IN_WIDTH = ATT_WIDTH + 2 * KV_WIDTH + 4 * RET_WIDTH
D_FF = 4 * D_MODEL
PLE_DIM = 256
N_BUCKETS = 32
MAX_DISTANCE = 128
ROPE_BASE = 10000.0
EPS = 1e-6
ATT_SCALE = HEAD_DIM ** -0.5
LOG2E = 1.0 / math.log(2.0)
RET_SCALE = RET_DIM ** -0.5

ROW_TILE = 512
ROW_PARTS = 2
MLP_ROW_TILE = 1088
MLP_FF_TILE = 512
Q_BLOCK = 2 * CHUNK
K_BLOCK = WINDOW + Q_BLOCK
RET_TILE = 256
SAMPLE_BLOCK = 8
SAMPLE_KEYS = 256
RET_SAMPLE_BLOCK = 4
BF16_TILE_ROWS = 16
V7X_VMEM_BYTES = 64 * 1024 * 1024
MIB = 1024 * 1024
VMEM_LIMIT = V7X_VMEM_BYTES - 6 * MIB
MIX_OUT_VMEM_LIMIT = V7X_VMEM_BYTES - 2 * MIB

NT_DIMS = (((1,), (1,)), ((), ()))
TN_DIMS = (((0,), (0,)), ((), ()))


def _params(*sem, vmem_limit=VMEM_LIMIT):
    return pltpu.CompilerParams(dimension_semantics=sem, vmem_limit_bytes=vmem_limit)


def _resident(shape):
    nd = len(shape)
    return pl.BlockSpec(shape, lambda *_: (0,) * nd, pipeline_mode=pl.Buffered(1))


def _resident_layer(shape, layer):
    nd = len(shape) - 1
    return pl.BlockSpec((None,) + tuple(shape[1:]), lambda *_: (layer,) + (0,) * nd,
                        pipeline_mode=pl.Buffered(1))


def _rms(x, g):
    ms = jnp.mean(x * x, axis=-1, keepdims=True)
    return x * lax.rsqrt(ms + EPS) * g


N_PROMPT_TILES = SEQ // ROW_TILE
assert N_SAMPLE == ROW_TILE


def _stacked_rows(width):
    return pl.BlockSpec((ROW_TILE, width),
                        lambda i: ((i + N_PROMPT_TILES) % (N_PROMPT_TILES + 1), 0))


def _prompt_rows(width, *lead):
    none = (None,) * len(lead)
    return pl.BlockSpec(none + (ROW_TILE, width), lambda i: lead + (jnp.maximum(i - 1, 0), 0))


def _sample_rows(width, *lead, single_buffer=True):
    none = (None,) * len(lead)
    mode = dict(pipeline_mode=pl.Buffered(1)) if single_buffer else {}
    return pl.BlockSpec(none + (ROW_TILE, width), lambda i: lead + (0, 0), **mode)


def _row_parts(parts=ROW_PARTS):
    part = ROW_TILE // parts
    return [slice(r * part, (r + 1) * part) for r in range(parts)]


def _pick_rows(prompt_ref, sample_ref, rows=slice(None)):
    return jnp.where(pl.program_id(0) == 0, sample_ref[rows, :], prompt_ref[rows, :])


def _bias_kernel(table_ref, bp_ref, bs_ref, op_ref, os_ref):
    bp = bp_ref[...]
    bs = bs_ref[...]

    def lookup(bucket, head):
        acc = jnp.full(bucket.shape, -jnp.inf, F32)
        for b in range(N_BUCKETS):
            acc = jnp.where(bucket == b, table_ref[b, head] * LOG2E, acc)
        return acc

    def per_kv(kv, c):
        for half in range(2):
            for side in range(2):
                head = kv * GROUP + 2 * half + side
                rp, cp = bp.shape
                op_ref[kv, half * rp:(half + 1) * rp, side * cp:(side + 1) * cp] = lookup(bp, head)
                rs, cs = bs.shape
                os_ref[kv, half * rs:(half + 1) * rs, side * cs:(side + 1) * cs] = lookup(bs, head)
        return c

    lax.fori_loop(0, N_KV, per_kv, 0)


def _rel_bias(table, bucket_p, bucket_s):
    def out(bucket):
        rows, cols = bucket.shape
        return jax.ShapeDtypeStruct((N_KV, 2 * rows, 2 * cols), F32)

    return pl.pallas_call(
        _bias_kernel,
        out_shape=(out(bucket_p), out(bucket_s)),
        in_specs=[pl.BlockSpec(memory_space=pltpu.SMEM),
                  pl.BlockSpec(memory_space=pltpu.VMEM),
                  pl.BlockSpec(memory_space=pltpu.VMEM)],
        out_specs=(pl.BlockSpec(memory_space=pltpu.VMEM), pl.BlockSpec(memory_space=pltpu.VMEM)),
        name="rel_bias",
    )(table, bucket_p, bucket_s)


def _in_proj_kernel(hp_ref, hs_ref, g_ref, w_ref, cos_ref, sin_ref,
                    qa_ref, k_ref, v_ref, qr_ref, kr_ref, vr_ref, gr_ref):
    base = ATT_WIDTH + 2 * KV_WIDTH
    for rows in _row_parts():
        xn = _rms(_pick_rows(hp_ref, hs_ref, rows), g_ref[...]).astype(BF16)

        def proj(lo, width):
            return jnp.dot(xn, w_ref[:, lo:lo + width], preferred_element_type=F32)

        qa_ref[rows, :] = (proj(0, ATT_WIDTH) * (ATT_SCALE * LOG2E)).astype(BF16)
        k_ref[rows, :] = proj(ATT_WIDTH, KV_WIDTH)
        v_ref[rows, :] = proj(ATT_WIDTH + KV_WIDTH, KV_WIDTH)
        cos = cos_ref[rows, :]
        sin = sin_ref[rows, :]

        def rot(x):
            return x * cos + pltpu.roll(x, RET_DIM // 2, 1) * sin

        q = proj(base, RET_WIDTH)
        for h in range(RET_HEADS):
            sl = slice(h * RET_DIM, (h + 1) * RET_DIM)
            qr_ref[rows, sl] = rot(q[:, sl]).astype(BF16)
        k = proj(base + RET_WIDTH, RET_WIDTH)
        for h in range(RET_HEADS):
            sl = slice(h * RET_DIM, (h + 1) * RET_DIM)
            kr_ref[rows, sl] = (rot(k[:, sl]) * RET_SCALE).astype(BF16)
        vr_ref[rows, :] = proj(base + 2 * RET_WIDTH, RET_WIDTH).astype(BF16)
        gr_ref[rows, :] = proj(base + 3 * RET_WIDTH, RET_WIDTH).astype(BF16)


def _in_proj(hp, hs, g, w, layer, cos2, sin2):
    tm = ROW_TILE
    row = _stacked_rows
    out = lambda width, dt: jax.ShapeDtypeStruct((N_TOK, width), dt)
    return pl.pallas_call(
        _in_proj_kernel,
        grid=(N_TOK // tm,),
        in_specs=[_prompt_rows(D_MODEL), _sample_rows(D_MODEL),
                  _resident_layer(g.shape, layer), _resident(w.shape),
                  row(RET_DIM), row(RET_DIM)],
        out_specs=(row(ATT_WIDTH), row(KV_WIDTH), row(KV_WIDTH),
                   row(RET_WIDTH), row(RET_WIDTH), row(RET_WIDTH), row(RET_WIDTH)),
        out_shape=(out(ATT_WIDTH, BF16), out(KV_WIDTH, F32), out(KV_WIDTH, F32),
                   out(RET_WIDTH, BF16), out(RET_WIDTH, BF16), out(RET_WIDTH, BF16),
                   out(RET_WIDTH, BF16)),
        compiler_params=_params("parallel"),
        name="in_proj",
    )(hp, hs, g, w, cos2, sin2)


PAIR = 2 * HEAD_DIM


def _attend_pairs(qst, kcat, vcat, bias, sinks):
    half = kcat.shape[0] // 2
    side0 = lax.broadcasted_iota(jnp.int32, vcat.shape, 0) < half
    low = lax.broadcasted_iota(jnp.int32, vcat.shape, 1) < HEAD_DIM
    vext = jnp.concatenate([vcat, jnp.where(side0 == low, 1.0, 0.0).astype(BF16)], axis=1)
    s = lax.dot_general(qst, kcat, NT_DIMS, preferred_element_type=F32) + bias
    ps, es = [], []
    for side in range(2):
        ss = s[:, side * half:(side + 1) * half]
        m = jnp.maximum(jnp.max(ss, axis=-1, keepdims=True), sinks[side])
        ps.append(jnp.exp2(ss - jnp.concatenate([m] * (half // PAIR), axis=1)))
        es.append(jnp.exp2(sinks[side] - m))
    p = jnp.concatenate(ps, axis=1).astype(BF16)
    r = jnp.dot(p, vext, preferred_element_type=F32)
    lo = lax.broadcasted_iota(jnp.int32, qst.shape, 1) < HEAD_DIM
    den = r[:, PAIR:] + jnp.where(lo, es[0], es[1])
    return r[:, :PAIR] / den


def _fill_padded(dst, rows, x):
    lo = lax.broadcasted_iota(jnp.int32, (x.shape[0], PAIR), 1) < HEAD_DIM
    for pair in range(N_KV // 2):
        xp = x[:, pair * PAIR:(pair + 1) * PAIR]
        xr = pltpu.roll(xp, HEAD_DIM, 1)
        zero = jnp.zeros_like(xp)
        dst[2 * pair, 0, rows, :] = jnp.where(lo, xp, zero).astype(BF16)
        dst[2 * pair, 1, rows, :] = jnp.where(lo, zero, xr).astype(BF16)
        dst[2 * pair + 1, 0, rows, :] = jnp.where(lo, xr, zero).astype(BF16)
        dst[2 * pair + 1, 1, rows, :] = jnp.where(lo, zero, xp).astype(BF16)


def _pair_layout(x):
    lead = x.shape[:-3]
    rows, cols = x.shape[-2:]
    x = x.reshape(lead + (N_KV, 2, 2, rows, cols))
    x = jnp.swapaxes(x, -4, -3)
    return x.reshape(lead + (N_KV, 2, 2 * rows, cols))


def _attn_sample_kernel(q_ref, kn_ref, vn_ref, ck_ref, cv_ref, bias_ref, sink_ref, o_ref, kz, vz):
    keys = WINDOW + DEC_SEQ
    pad = jnp.zeros((SAMPLE_BLOCK, N_KV, 2, SAMPLE_KEYS - keys, PAIR), BF16)
    kz[:, :, :, keys:, :] = pad
    vz[:, :, :, keys:, :] = pad
    for b in range(SAMPLE_BLOCK):
        new = slice(b * DEC_SEQ, (b + 1) * DEC_SEQ)
        _fill_padded(kz.at[b], slice(0, WINDOW), ck_ref[b])
        _fill_padded(kz.at[b], slice(WINDOW, keys), kn_ref[new, :])
        _fill_padded(vz.at[b], slice(0, WINDOW), cv_ref[b])
        _fill_padded(vz.at[b], slice(WINDOW, keys), vn_ref[new, :])
    for b in range(SAMPLE_BLOCK):
        rows = slice(b * DEC_SEQ, (b + 1) * DEC_SEQ)
        for kv in range(N_KV):
            c0 = kv * GROUP * HEAD_DIM
            qst = jnp.concatenate([q_ref[rows, c0:c0 + 2 * HEAD_DIM],
                                   q_ref[rows, c0 + 2 * HEAD_DIM:c0 + 4 * HEAD_DIM]], axis=0)
            o = _attend_pairs(qst, kz[b, kv].reshape(2 * SAMPLE_KEYS, PAIR),
                              vz[b, kv].reshape(2 * SAMPLE_KEYS, PAIR),
                              bias_ref[kv], [sink_ref[kv, side] for side in range(2)])
            o_ref[rows, c0:c0 + 2 * HEAD_DIM] = o[0:DEC_SEQ].astype(BF16)
            o_ref[rows, c0 + 2 * HEAD_DIM:c0 + 4 * HEAD_DIM] = o[DEC_SEQ:].astype(BF16)


def _attn_sample(qa, k, v, cache_k, cache_v, bias, sink, layer):
    rows = SAMPLE_BLOCK * DEC_SEQ
    first = SEQ // rows
    new = lambda width: pl.BlockSpec((rows, width), lambda i: (first + i, 0))
    cache = pl.BlockSpec((SAMPLE_BLOCK, WINDOW, KV_WIDTH), lambda i: (i, 0, 0))
    return pl.pallas_call(
        _attn_sample_kernel,
        grid=(DEC_BATCH // SAMPLE_BLOCK,),
        in_specs=[new(ATT_WIDTH), new(KV_WIDTH), new(KV_WIDTH), cache, cache,
                  _resident(bias.shape), _resident_layer(sink.shape, layer)],
        out_specs=pl.BlockSpec((rows, ATT_WIDTH), lambda i: (i, 0)),
        out_shape=jax.ShapeDtypeStruct((N_SAMPLE, ATT_WIDTH), BF16),
        scratch_shapes=[pltpu.VMEM((SAMPLE_BLOCK, N_KV, 2, SAMPLE_KEYS, PAIR), BF16)] * 2,
        compiler_params=_params("parallel"),
        name="attn_sample",
    )(qa, k, v, cache_k, cache_v, bias, sink)


def _retention_head(q, k, v, g, state, decay, zeta, xi, g_chunk, gain):
    scores = lax.dot_general(q, k, NT_DIMS, preferred_element_type=F32) * decay
    if q.shape[0] % RET_DIM == 0:
        lhs = jnp.concatenate([(q.astype(F32) * xi).astype(BF16), scores.astype(BF16)], axis=1)
        rhs = jnp.concatenate([state.astype(BF16), v], axis=0)
        o = jnp.dot(lhs, rhs, preferred_element_type=F32)
    else:
        o = jnp.dot(scores.astype(BF16), v, preferred_element_type=F32)
        o = o + jnp.dot(q, state.astype(BF16), preferred_element_type=F32) * xi
    kz = (k.astype(F32) * zeta).astype(BF16)
    upd = lax.dot_general(kz, v, TN_DIMS, preferred_element_type=F32)
    new_state = g_chunk * state + upd
    o = o * lax.rsqrt(jnp.mean(o * o, axis=-1, keepdims=True) + EPS) * gain
    gf = g.astype(F32)
    return gf * jax.nn.sigmoid(gf) * o, new_state


def _ret_sample_kernel(gc_ref, q_ref, k_ref, v_ref, g_ref, s_ref, decay_ref, zeta_ref, xi_ref,
                       gain_ref, *refs, opening):
    o_ref, st_ref = refs[-2:]
    if opening:
        st_ref[1:] = jnp.zeros((DEPTH - 1,) + st_ref.shape[1:], F32)
        st_ref = st_ref.at[0]
    for b in range(RET_SAMPLE_BLOCK):
        rows = slice(b * DEC_SEQ, (b + 1) * DEC_SEQ)
        for h in range(RET_HEADS):
            sl = slice(h * RET_DIM, (h + 1) * RET_DIM)
            o, s_new = _retention_head(q_ref[rows, sl], k_ref[rows, sl], v_ref[rows, sl],
                                       g_ref[rows, sl], s_ref[b, h], decay_ref[h], zeta_ref[h],
                                       xi_ref[h], gc_ref[h], gain_ref[:, sl])
            st_ref[b, h] = s_new
            o_ref[rows, sl] = o.astype(BF16)


def _ret_sample(qr, kr, vr, gr, state, layer, consts, gain, new_state):
    g_chunk, decay, zeta, xi = consts
    nb = RET_SAMPLE_BLOCK
    first = SEQ // (nb * DEC_SEQ)
    row = pl.BlockSpec((nb * DEC_SEQ, RET_WIDTH), lambda i: (first + i, 0))
    st_block = (nb, RET_HEADS, RET_DIM, RET_DIM)
    st_spec = pl.BlockSpec((None,) + st_block, lambda i: (layer, i, 0, 0, 0))
    in_specs = [pl.BlockSpec(memory_space=pltpu.SMEM), row, row, row, row, st_spec,
                _resident(decay.shape), _resident(zeta.shape), _resident(xi.shape),
                _resident(gain.shape)]
    args = [g_chunk, qr, kr, vr, gr, state, decay, zeta, xi, gain]
    opening = new_state is None
    if opening:
        assert layer == 0
        aliases = {}
        st_out = pl.BlockSpec((DEPTH,) + st_block, lambda i: (0, i, 0, 0, 0))
    else:
        args.append(new_state)
        in_specs.append(pl.BlockSpec(memory_space=pl.ANY))
        aliases = {len(args) - 1: 1}
        st_out = st_spec
    return pl.pallas_call(
        functools.partial(_ret_sample_kernel, opening=opening),
        grid=(DEC_BATCH // nb,),
        in_specs=in_specs,
        out_specs=(pl.BlockSpec((nb * DEC_SEQ, RET_WIDTH), lambda i: (i, 0)), st_out),
        out_shape=(jax.ShapeDtypeStruct((N_SAMPLE, RET_WIDTH), BF16),
                   jax.ShapeDtypeStruct(state.shape, F32)),
        input_output_aliases=aliases,
        compiler_params=_params("parallel"),
        name="ret_sample",
    )(*args)


def _retention_consts(chunk):
    lg = jnp.log(1.0 - 2.0 ** (-5.0 - jnp.arange(RET_HEADS, dtype=F32)))
    idx = jnp.arange(chunk, dtype=F32)
    diff = idx[:, None] - idx[None, :]
    decay = jnp.where(diff >= 0, jnp.exp(lg[:, None, None] * jnp.maximum(diff, 0.0)), 0.0)
    zeta = jnp.exp(lg[:, None] * (chunk - 1 - idx))[:, :, None]
    xi = jnp.exp(lg[:, None] * (idx + 1))[:, :, None]
    dense = (RET_HEADS, chunk, RET_DIM)
    return (jnp.exp(lg * chunk), decay,
            jnp.broadcast_to(zeta, dense), jnp.broadcast_to(xi, dense))


def _mix_out_kernel(gc_ref, q_ref, kc_ref, kp_ref, vc_ref, vp_ref, bias_ref, sink_ref,
                    qr_ref, kr_ref, vr_ref, gr_ref, decay_ref, zeta_ref, xi_ref, gain_ref,
                    oas_ref, ors_ref, hp_ref, hs_ref, w_ref, g_ref,
                    h1_ref, xn_ref, st_ref, kz, vz, state):
    i = pl.program_id(0)

    def project(rows, h, oa, orr):
        h1 = (h + jnp.dot(oa, w_ref[0:ATT_WIDTH, :], preferred_element_type=F32)
              + jnp.dot(orr, w_ref[ATT_WIDTH:, :], preferred_element_type=F32))
        h1_ref[rows, :] = h1
        xn_ref[rows, :] = _rms(h1, g_ref[...]).astype(BF16)

    @pl.when(i == 0)
    def _():
        state[...] = jnp.zeros(state.shape, F32)
        for rows in _row_parts():
            project(rows, hs_ref[rows, :], oas_ref[rows, :], ors_ref[rows, :])

    @pl.when(i > 0)
    def _():
        _fill_padded(kz, slice(0, WINDOW), kp_ref[...])
        _fill_padded(kz, slice(WINDOW, WINDOW + ROW_TILE), kc_ref[...])
        _fill_padded(vz, slice(0, WINDOW), vp_ref[...])
        _fill_padded(vz, slice(WINDOW, WINDOW + ROW_TILE), vc_ref[...])
        col = lax.broadcasted_iota(jnp.int32, (2 * Q_BLOCK, 2 * K_BLOCK), 1)
        before_start = jnp.logical_and(i == 1, col % K_BLOCK < WINDOW)
        for part, rows in enumerate(_row_parts(MIX_PARTS)):
            blocks = []
            for j in range(part * Q_PER_PART, (part + 1) * Q_PER_PART):
                qrows = slice(j * Q_BLOCK, (j + 1) * Q_BLOCK)
                band = slice(j * Q_BLOCK, j * Q_BLOCK + K_BLOCK)
                pieces = []
                for kv in range(N_KV):
                    c0 = kv * GROUP * HEAD_DIM
                    qst = jnp.concatenate([q_ref[qrows, c0:c0 + PAIR],
                                           q_ref[qrows, c0 + PAIR:c0 + 2 * PAIR]], axis=0)
                    bias = bias_ref[kv]
                    if j == 0:
                        bias = jnp.where(before_start, -jnp.inf, bias)
                    o = _attend_pairs(
                        qst,
                        jnp.concatenate([kz[kv, side, band, :] for side in range(2)], axis=0),
                        jnp.concatenate([vz[kv, side, band, :] for side in range(2)], axis=0),
                        bias, [sink_ref[kv, side] for side in range(2)])
                    pieces += [o[0:Q_BLOCK].astype(BF16), o[Q_BLOCK:].astype(BF16)]
                blocks.append(jnp.concatenate(pieces, axis=1))
            oa = jnp.concatenate(blocks, axis=0)
            heads = []
            for h in range(RET_HEADS):
                sl = slice(h * RET_DIM, (h + 1) * RET_DIM)
                o, s_new = _retention_head(qr_ref[rows, sl], kr_ref[rows, sl], vr_ref[rows, sl],
                                           gr_ref[rows, sl], state[h], decay_ref[h], zeta_ref[h],
                                           xi_ref[h], gc_ref[h], gain_ref[:, sl])
                state[h] = s_new
                heads.append(o.astype(BF16))
            project(rows, hp_ref[rows, :], oa, jnp.concatenate(heads, axis=1))

    @pl.when(i == pl.num_programs(0) - 1)
    def _():
        st_ref[...] = state[...]


MIX_PARTS = ROW_TILE // RET_TILE
assert RET_TILE % Q_BLOCK == 0
Q_PER_PART = RET_TILE // Q_BLOCK


def _mix_out(qa, k, v, bias, sink, qr, kr, vr, gr, consts, gain, oa_s, or_s, hp, hs, w, g, layer):
    g_chunk, decay, zeta, xi = consts
    per = ROW_TILE // WINDOW
    prev = pl.BlockSpec((WINDOW, KV_WIDTH),
                        lambda i: (jnp.maximum(jnp.maximum(i - 1, 0) * per - 1, 0), 0))
    row = _stacked_rows
    st_shape = (RET_HEADS, RET_DIM, RET_DIM)
    return pl.pallas_call(
        _mix_out_kernel,
        grid=(N_TOK // ROW_TILE,),
        in_specs=[pl.BlockSpec(memory_space=pltpu.SMEM),
                  _prompt_rows(ATT_WIDTH), _prompt_rows(KV_WIDTH), prev,
                  _prompt_rows(KV_WIDTH), prev, _resident(bias.shape),
                  _resident_layer(sink.shape, layer),
                  _prompt_rows(RET_WIDTH), _prompt_rows(RET_WIDTH), _prompt_rows(RET_WIDTH),
                  _prompt_rows(RET_WIDTH), _resident(decay.shape), _resident(zeta.shape),
                  _resident(xi.shape), _resident(gain.shape),
                  _sample_rows(ATT_WIDTH), _sample_rows(RET_WIDTH),
                  _prompt_rows(D_MODEL), _sample_rows(D_MODEL),
                  _resident(w.shape), _resident_layer(g.shape, layer)],
        out_specs=(row(D_MODEL), row(D_MODEL), pl.BlockSpec(st_shape, lambda i: (0, 0, 0))),
        out_shape=(jax.ShapeDtypeStruct((N_TOK, D_MODEL), F32),
                   jax.ShapeDtypeStruct((N_TOK, D_MODEL), BF16),
                   jax.ShapeDtypeStruct(st_shape, F32)),
        scratch_shapes=[pltpu.VMEM((N_KV, 2, ROW_TILE + WINDOW, PAIR), BF16),
                        pltpu.VMEM((N_KV, 2, ROW_TILE + WINDOW, PAIR), BF16),
                        pltpu.VMEM(st_shape, F32)],
        compiler_params=_params("arbitrary", vmem_limit=MIX_OUT_VMEM_LIMIT),
        name="mix_out",
    )(g_chunk, qa, k, k, v, v, bias, sink, qr, kr, vr, gr, decay, zeta, xi, gain,
      oa_s, or_s, hp, hs, w, g)


def _mlp_kernel(xn_ref, wu_ref, wd_ref, *refs):
    n_extra = (len(refs) - 1) // 2
    o_ref = refs[n_extra]

    @pl.when(pl.program_id(1) == 0)
    def _():
        o_ref[...] = jnp.zeros(o_ref.shape, F32)

    up = jnp.dot(xn_ref[...], wu_ref[...].astype(BF16), preferred_element_type=F32)
    act = jnp.square(jnp.maximum(up, 0.0)).astype(BF16)
    o_ref[...] += jnp.dot(act, wd_ref[...].astype(BF16), preferred_element_type=F32)
    for src, dst in zip(refs[:n_extra], refs[n_extra + 1:]):
        dst[...] = src[...].astype(BF16)


def _mlp(xn, w_up, w_down, layer, cast_jobs=()):
    tm, tf = MLP_ROW_TILE, MLP_FF_TILE
    n_ff = D_FF // tf
    steps = (N_TOK // tm) * n_ff
    slab_in, slab_out, slab_shape = [], [], []
    for w, lyr in cast_jobs:
        _, k, n = w.shape
        rows = k // steps
        assert rows * steps == k and rows % BF16_TILE_ROWS == 0
        slab_in.append(pl.BlockSpec((None, rows, n), lambda i, f, lyr=lyr: (lyr, i * n_ff + f, 0)))
        slab_out.append(pl.BlockSpec((rows, n), lambda i, f: (i * n_ff + f, 0)))
        slab_shape.append(jax.ShapeDtypeStruct((k, n), BF16))
    return pl.pallas_call(
        _mlp_kernel,
        grid=(N_TOK // tm, n_ff),
        in_specs=[pl.BlockSpec((tm, D_MODEL), lambda i, f: (i, 0)),
                  pl.BlockSpec((None, D_MODEL, tf), lambda i, f: (layer, 0, f)),
                  pl.BlockSpec((None, tf, D_MODEL), lambda i, f: (layer, f, 0))] + slab_in,
        out_specs=[pl.BlockSpec((tm, D_MODEL), lambda i, f: (i, 0))] + slab_out,
        out_shape=[jax.ShapeDtypeStruct((N_TOK, D_MODEL), F32)] + slab_shape,
        compiler_params=_params("arbitrary", "arbitrary"),
        name="mlp",
    )(xn, w_up, w_down, *[w for w, _ in cast_jobs])


def _gate_kernel(h1_ref, m_ref, pp_ref, ps_ref, wg_ref, wp_ref, g_ref, gf_ref, op_ref, os_ref,
                 *, final):
    for rows in _row_parts():
        emb = jnp.dot(_pick_rows(pp_ref, ps_ref, rows).astype(BF16), wp_ref[...],
                      preferred_element_type=F32)
        h2 = h1_ref[rows, :] + m_ref[rows, :]
        xn = _rms(h2, g_ref[...]).astype(BF16)
        gate = jax.nn.sigmoid(jnp.dot(xn, wg_ref[...], preferred_element_type=F32))
        h3 = h2 + gate * emb
        op_ref[rows, :] = _rms(h3, gf_ref[...]) if final else h3

    @pl.when(pl.program_id(0) == 0)
    def _():
        os_ref[...] = op_ref[...]


def _gate(h1, mlp_out, p_prompt, p_sample, layer, w_gate, w_ple, g, g_final, final):
    row = _stacked_rows
    return pl.pallas_call(
        functools.partial(_gate_kernel, final=final),
        grid=(N_TOK // ROW_TILE,),
        in_specs=[row(D_MODEL), row(D_MODEL),
                  _prompt_rows(PLE_DIM, layer, 0), _sample_rows(PLE_DIM, layer),
                  _resident(w_gate.shape), _resident_layer(w_ple.shape, layer),
                  _resident_layer(g.shape, layer), _resident(g_final.shape)],
        out_specs=(_prompt_rows(D_MODEL), _sample_rows(D_MODEL, single_buffer=False)),
        out_shape=(jax.ShapeDtypeStruct((SEQ, D_MODEL), F32),
                   jax.ShapeDtypeStruct((N_SAMPLE, D_MODEL), F32)),
        compiler_params=_params("arbitrary"),
        name="gate",
    )(h1, mlp_out, p_prompt, p_sample, w_gate, w_ple, g, g_final)


def _t5_bucket(rel):
    nb = N_BUCKETS // 2
    max_exact = nb // 2
    ret = jnp.where(rel > 0, nb, 0)
    n = jnp.abs(rel)
    nf = jnp.maximum(n, max_exact).astype(F32)
    large = max_exact + (jnp.log(nf / max_exact) / math.log(MAX_DISTANCE / max_exact)
                         * (nb - max_exact)).astype(jnp.int32)
    large = jnp.minimum(large, nb - 1)
    return ret + jnp.where(n < max_exact, n, large)


def _rotary_tables():
    half = RET_DIM // 2
    inv = ROPE_BASE ** (-jnp.arange(half, dtype=F32) / half)

    def cos_sin(pos):
        ang = pos.astype(F32)[:, None] * inv[None, :]
        return jnp.cos(ang), jnp.sin(ang)

    ca, sa = cos_sin(CHUNK * jnp.arange(SEQ // CHUNK, dtype=jnp.int32))
    cb, sb = cos_sin(jnp.arange(CHUNK, dtype=jnp.int32))
    cos_p = (ca[:, None] * cb[None] - sa[:, None] * sb[None]).reshape(SEQ, half)
    sin_p = (sa[:, None] * cb[None] + ca[:, None] * sb[None]).reshape(SEQ, half)
    cos_s, sin_s = cos_sin(PAST_LEN + jnp.arange(DEC_SEQ, dtype=jnp.int32))
    cos = jnp.concatenate([cos_p, jnp.tile(cos_s, (DEC_BATCH, 1))])
    sin = jnp.concatenate([sin_p, jnp.tile(sin_s, (DEC_BATCH, 1))])
    return jnp.concatenate([cos, cos], axis=-1), jnp.concatenate([-sin, sin], axis=-1)


def kernel(x_prompt, x_sample, cache_k, cache_v, state_ret, p_prompt, p_sample, rel_bias,
           norm_mix, w_in, attn_sink, ret_gain, w_out, norm_mlp, w_up, w_down, norm_ple,
           w_gate, w_ple, norm_final):
    q_idx = jnp.arange(Q_BLOCK, dtype=jnp.int32)[:, None]
    s_idx = jnp.arange(K_BLOCK, dtype=jnp.int32)[None, :]
    bucket_p = _t5_bucket(s_idx - WINDOW - q_idx)
    bucket_s = _t5_bucket(jnp.arange(WINDOW + DEC_SEQ, dtype=jnp.int32)[None, :] - WINDOW
                          - jnp.arange(DEC_SEQ, dtype=jnp.int32)[:, None])
    back = q_idx // CHUNK + WINDOW // CHUNK - s_idx // CHUNK
    seen = jnp.logical_and(back >= 0, back <= WINDOW // CHUNK)
    bucket_p = jnp.where(seen, bucket_p, -1)
    bucket_s = jnp.pad(bucket_s, ((0, 0), (0, SAMPLE_KEYS - WINDOW - DEC_SEQ)), constant_values=-1)
    bias_p, bias_s = _rel_bias(rel_bias, bucket_p, bucket_s)

    cos2, sin2 = _rotary_tables()
    consts_p = _retention_consts(RET_TILE)
    consts_s = _retention_consts(DEC_SEQ)

    w_in_bf = {0: w_in[0].astype(BF16)}
    w_out_bf = {0: w_out[0].astype(BF16)}
    w_gate_bf = {}
    w_ple = w_ple.astype(BF16)
    norm_mix, norm_mlp, norm_ple = (g.reshape(DEPTH, 1, D_MODEL)
                                    for g in (norm_mix, norm_mlp, norm_ple))
    hp = x_prompt.reshape(SEQ, D_MODEL)
    hs = x_sample.reshape(N_SAMPLE, D_MODEL)
    p_sample = p_sample.reshape(DEPTH, N_SAMPLE, PLE_DIM)
    k_p, v_p, s_p, k_s, v_s = [], [], [], [], []
    state_s = None
    sink = attn_sink.astype(F32).reshape(DEPTH, N_HEADS, 1, 1) * LOG2E
    sink_p = _pair_layout(jnp.broadcast_to(sink, (DEPTH, N_HEADS, Q_BLOCK, PAIR)))
    sink_s = _pair_layout(jnp.broadcast_to(sink, (DEPTH, N_HEADS, DEC_SEQ, PAIR)))
    for l in range(DEPTH):
        gain = ret_gain[l].reshape(1, RET_WIDTH)

        qa, k, v, qr, kr, vr, gr = _in_proj(hp, hs, norm_mix, w_in_bf[l], l, cos2, sin2)
        oa_s = _attn_sample(qa, k, v, cache_k[l].reshape(DEC_BATCH, WINDOW, KV_WIDTH),
                            cache_v[l].reshape(DEC_BATCH, WINDOW, KV_WIDTH), bias_s, sink_s, l)
        or_s, state_s = _ret_sample(qr, kr, vr, gr, state_ret, l, consts_s, gain, state_s)
        h1, xn, st_p = _mix_out(qa, k, v, bias_p, sink_p, qr, kr, vr, gr, consts_p, gain,
                                oa_s, or_s, hp, hs, w_out_bf[l], norm_mlp, l)
        jobs = [(w_gate, l)]
        if l + 1 < DEPTH:
            jobs += [(w_in, l + 1), (w_out, l + 1)]
        mlp_out, *cast = _mlp(xn, w_up, w_down, l, jobs)
        w_gate_bf[l] = cast[0]
        if l + 1 < DEPTH:
            w_in_bf[l + 1], w_out_bf[l + 1] = cast[1], cast[2]
        hp, hs = _gate(h1, mlp_out, p_prompt, p_sample, l, w_gate_bf[l], w_ple, norm_ple,
                       norm_final.reshape(1, D_MODEL), final=(l == DEPTH - 1))

        k_p.append(k[SEQ - WINDOW:SEQ].reshape(1, WINDOW, N_KV, HEAD_DIM))
        v_p.append(v[SEQ - WINDOW:SEQ].reshape(1, WINDOW, N_KV, HEAD_DIM))
        s_p.append(st_p[None])
        k_s.append(k[SEQ:].reshape(DEC_BATCH, DEC_SEQ, N_KV, HEAD_DIM))
        v_s.append(v[SEQ:].reshape(DEC_BATCH, DEC_SEQ, N_KV, HEAD_DIM))

    return (hp.reshape(1, SEQ, D_MODEL), hs.reshape(DEC_BATCH, DEC_SEQ, D_MODEL),
            jnp.stack(k_p), jnp.stack(v_p), jnp.stack(s_p),
            jnp.stack(k_s), jnp.stack(v_s), state_s)
```

```python
import functools
import math

import jax
import jax.numpy as jnp
from jax import lax
from jax.experimental import pallas as pl
from jax.experimental.pallas import tpu as pltpu

F32 = jnp.float32
BF16 = jnp.bfloat16

D_MODEL = 2048
SEQ = 8192
DEPTH = 2
DEC_BATCH = 32
DEC_SEQ = 16
PAST_LEN = 4096
N_SAMPLE = DEC_BATCH * DEC_SEQ
N_TOK = SEQ + N_SAMPLE

CHUNK = 64
WINDOW = 128
HEAD_DIM = 64
N_HEADS = 16
N_KV = 4
GROUP = N_HEADS // N_KV
ATT_WIDTH = N_HEADS * HEAD_DIM
KV_WIDTH = N_KV * HEAD_DIM
PADDED_WIDTH = N_KV * 2 * 2 * HEAD_DIM
RET_DIM = 128
RET_HEADS = 8
RET_WIDTH = RET_HEADS * RET_DIM
IN_WIDTH = ATT_WIDTH + 2 * KV_WIDTH + 4 * RET_WIDTH
D_FF = 4 * D_MODEL
PLE_DIM = 256
N_BUCKETS = 32
MAX_DISTANCE = 128
ROPE_BASE = 10000.0
EPS = 1e-6
ATT_SCALE = HEAD_DIM ** -0.5
LOG2E = 1.0 / math.log(2.0)
RET_SCALE = RET_DIM ** -0.5

ROW_TILE = 512
ROW_PARTS = 2
MLP_ROW_TILE = 1088
MLP_FF_TILE = 512
Q_BLOCK = 2 * CHUNK
K_BLOCK = WINDOW + Q_BLOCK
RET_TILE = 256
SAMPLE_BLOCK = 8
SAMPLE_KEYS = 256
RET_SAMPLE_BLOCK = 4
VMEM_LIMIT = 58 * 1024 * 1024
MIX_OUT_VMEM_LIMIT = 63 * 1024 * 1024

NT_DIMS = (((1,), (1,)), ((), ()))
TN_DIMS = (((0,), (0,)), ((), ()))


def _params(*sem, vmem_limit=VMEM_LIMIT):
    return pltpu.CompilerParams(dimension_semantics=sem, vmem_limit_bytes=vmem_limit)


def _resident(shape):
    nd = len(shape)
    return pl.BlockSpec(shape, lambda *_: (0,) * nd, pipeline_mode=pl.Buffered(1))


def _resident_layer(shape, layer):
    nd = len(shape) - 1
    return pl.BlockSpec((None,) + tuple(shape[1:]), lambda *_: (layer,) + (0,) * nd,
                        pipeline_mode=pl.Buffered(1))


def _rms(x, g):
    ms = jnp.mean(x * x, axis=-1, keepdims=True)
    return x * lax.rsqrt(ms + EPS) * g


N_PROMPT_TILES = SEQ // ROW_TILE
assert N_SAMPLE == ROW_TILE


def _stacked_rows(width):
    return pl.BlockSpec((ROW_TILE, width),
                        lambda i: ((i + N_PROMPT_TILES) % (N_PROMPT_TILES + 1), 0))


def _prompt_rows(width, *lead):
    none = (None,) * len(lead)
    return pl.BlockSpec(none + (ROW_TILE, width), lambda i: lead + (jnp.maximum(i - 1, 0), 0))


def _sample_rows(width, *lead, single_buffer=True):
    none = (None,) * len(lead)
    mode = dict(pipeline_mode=pl.Buffered(1)) if single_buffer else {}
    return pl.BlockSpec(none + (ROW_TILE, width), lambda i: lead + (0, 0), **mode)


def _row_parts(parts=ROW_PARTS):
    part = ROW_TILE // parts
    return [slice(r * part, (r + 1) * part) for r in range(parts)]


def _pick_rows(prompt_ref, sample_ref, rows=slice(None)):
    return jnp.where(pl.program_id(0) == 0, sample_ref[rows, :], prompt_ref[rows, :])


def _bias_kernel(table_ref, bp_ref, bs_ref, op_ref, os_ref):
    bp = bp_ref[...]
    bs = bs_ref[...]

    def lookup(bucket, head):
        acc = jnp.full(bucket.shape, -jnp.inf, F32)
        for b in range(N_BUCKETS):
            acc = jnp.where(bucket == b, table_ref[b, head] * LOG2E, acc)
        return acc

    def per_kv(kv, c):
        for half in range(2):
            for side in range(2):
                head = kv * GROUP + 2 * half + side
                rp, cp = bp.shape
                op_ref[kv, half * rp:(half + 1) * rp, side * cp:(side + 1) * cp] = lookup(bp, head)
                rs, cs = bs.shape
                os_ref[kv, half * rs:(half + 1) * rs, side * cs:(side + 1) * cs] = lookup(bs, head)
        return c

    lax.fori_loop(0, N_KV, per_kv, 0)


def _rel_bias(table, bucket_p, bucket_s):
    def out(bucket):
        rows, cols = bucket.shape
        return jax.ShapeDtypeStruct((N_KV, 2 * rows, 2 * cols), F32)

    return pl.pallas_call(
        _bias_kernel,
        out_shape=(out(bucket_p), out(bucket_s)),
        in_specs=[pl.BlockSpec(memory_space=pltpu.SMEM),
                  pl.BlockSpec(memory_space=pltpu.VMEM),
                  pl.BlockSpec(memory_space=pltpu.VMEM)],
        out_specs=(pl.BlockSpec(memory_space=pltpu.VMEM), pl.BlockSpec(memory_space=pltpu.VMEM)),
        name="rel_bias",
    )(table, bucket_p, bucket_s)


def _in_proj_kernel(hp_ref, hs_ref, g_ref, w_ref, cos_ref, sin_ref,
                    qa_ref, k_ref, v_ref, kz_ref, vz_ref, qr_ref, kr_ref, vr_ref, gr_ref):
    base = ATT_WIDTH + 2 * KV_WIDTH
    for rows in _row_parts():
        xn = _rms(_pick_rows(hp_ref, hs_ref, rows), g_ref[...]).astype(BF16)

        def proj(lo, width):
            return jnp.dot(xn, w_ref[:, lo:lo + width], preferred_element_type=F32)

        qa_ref[rows, :] = (proj(0, ATT_WIDTH) * (ATT_SCALE * LOG2E)).astype(BF16)
        for lo, plain_ref, padded_ref in ((ATT_WIDTH, k_ref, kz_ref),
                                          (ATT_WIDTH + KV_WIDTH, v_ref, vz_ref)):
            x = proj(lo, KV_WIDTH)
            plain_ref[rows, :] = x
            sides = _padded_sides(x)
            for kv in range(N_KV):
                for side in range(2):
                    padded_ref[rows, _padded_col(kv, side)] = sides[kv][side]
        cos = cos_ref[rows, :]
        sin = sin_ref[rows, :]

        def rot(x):
            return x * cos + pltpu.roll(x, RET_DIM // 2, 1) * sin

        q = proj(base, RET_WIDTH)
        for h in range(RET_HEADS):
            sl = slice(h * RET_DIM, (h + 1) * RET_DIM)
            qr_ref[rows, sl] = rot(q[:, sl]).astype(BF16)
        k = proj(base + RET_WIDTH, RET_WIDTH)
        for h in range(RET_HEADS):
            sl = slice(h * RET_DIM, (h + 1) * RET_DIM)
            kr_ref[rows, sl] = (rot(k[:, sl]) * RET_SCALE).astype(BF16)
        vr_ref[rows, :] = proj(base + 2 * RET_WIDTH, RET_WIDTH).astype(BF16)
        gr_ref[rows, :] = proj(base + 3 * RET_WIDTH, RET_WIDTH).astype(BF16)


def _in_proj(hp, hs, g, w, layer, cos2, sin2):
    tm = ROW_TILE
    row = _stacked_rows
    out = lambda width, dt: jax.ShapeDtypeStruct((N_TOK, width), dt)
    return pl.pallas_call(
        _in_proj_kernel,
        grid=(N_TOK // tm,),
        in_specs=[_prompt_rows(D_MODEL), _sample_rows(D_MODEL),
                  _resident_layer(g.shape, layer), _resident(w.shape),
                  row(RET_DIM), row(RET_DIM)],
        out_specs=(row(ATT_WIDTH), row(KV_WIDTH), row(KV_WIDTH),
                   row(PADDED_WIDTH), row(PADDED_WIDTH),
                   row(RET_WIDTH), row(RET_WIDTH), row(RET_WIDTH), row(RET_WIDTH)),
        out_shape=(out(ATT_WIDTH, BF16), out(KV_WIDTH, F32), out(KV_WIDTH, F32),
                   out(PADDED_WIDTH, BF16), out(PADDED_WIDTH, BF16),
                   out(RET_WIDTH, BF16), out(RET_WIDTH, BF16), out(RET_WIDTH, BF16),
                   out(RET_WIDTH, BF16)),
        compiler_params=_params("parallel"),
        name="in_proj",
    )(hp, hs, g, w, cos2, sin2)


PAIR = 2 * HEAD_DIM
SINK_ROWS = 8


def _attend_pairs(qst, kcat, vcat, bias, sinks):
    reps = qst.shape[0] // (2 * SINK_ROWS)
    sinks = [jnp.concatenate([s[0:SINK_ROWS]] * reps + [s[SINK_ROWS:]] * reps, axis=0)
             for s in sinks]
    half = kcat.shape[0] // 2
    side0 = lax.broadcasted_iota(jnp.int32, vcat.shape, 0) < half
    low = lax.broadcasted_iota(jnp.int32, vcat.shape, 1) < HEAD_DIM
    vext = jnp.concatenate([vcat, jnp.where(side0 == low, 1.0, 0.0).astype(BF16)], axis=1)
    s = lax.dot_general(qst, kcat, NT_DIMS, preferred_element_type=F32) + bias
    ps, es = [], []
    for side in range(2):
        ss = s[:, side * half:(side + 1) * half]
        m = jnp.maximum(jnp.max(ss, axis=-1, keepdims=True), sinks[side])
        ps.append(jnp.exp2(ss - jnp.concatenate([m] * (half // PAIR), axis=1)))
        es.append(jnp.exp2(sinks[side] - m))
    p = jnp.concatenate(ps, axis=1).astype(BF16)
    r = jnp.dot(p, vext, preferred_element_type=F32)
    lo = lax.broadcasted_iota(jnp.int32, qst.shape, 1) < HEAD_DIM
    den = r[:, PAIR:] + jnp.where(lo, es[0], es[1])
    return r[:, :PAIR] / den


def _padded_sides(x):
    lo = lax.broadcasted_iota(jnp.int32, (x.shape[0], PAIR), 1) < HEAD_DIM
    out = []
    for pair in range(N_KV // 2):
        xp = x[:, pair * PAIR:(pair + 1) * PAIR]
        xr = pltpu.roll(xp, HEAD_DIM, 1)
        zero = jnp.zeros_like(xp)
        out.append([jnp.where(lo, xp, zero).astype(BF16), jnp.where(lo, zero, xr).astype(BF16)])
        out.append([jnp.where(lo, xr, zero).astype(BF16), jnp.where(lo, zero, xp).astype(BF16)])
    return out


def _padded_col(kv, side):
    c0 = (2 * kv + side) * PAIR
    return slice(c0, c0 + PAIR)


def _fill_padded(dst, rows, x):
    sides = _padded_sides(x)
    for kv in range(N_KV):
        for side in range(2):
            dst[kv, side, rows, :] = sides[kv][side]


def _pair_layout(x):
    lead = x.shape[:-3]
    rows, cols = x.shape[-2:]
    x = x.reshape(lead + (N_KV, 2, 2, rows, cols))
    x = jnp.swapaxes(x, -4, -3)
    return x.reshape(lead + (N_KV, 2, 2 * rows, cols))


def _attn_sample_kernel(q_ref, kn_ref, vn_ref, ck_ref, cv_ref, bias_ref, sink_ref, o_ref, kz, vz):
    keys = WINDOW + DEC_SEQ
    pad = jnp.zeros((SAMPLE_BLOCK, N_KV, 2, SAMPLE_KEYS - keys, PAIR), BF16)
    kz[:, :, :, keys:, :] = pad
    vz[:, :, :, keys:, :] = pad
    for b in range(SAMPLE_BLOCK):
        new = slice(b * DEC_SEQ, (b + 1) * DEC_SEQ)
        _fill_padded(kz.at[b], slice(0, WINDOW), ck_ref[b])
        _fill_padded(vz.at[b], slice(0, WINDOW), cv_ref[b])
        for kv in range(N_KV):
            for side in range(2):
                kz[b, kv, side, WINDOW:keys, :] = kn_ref[new, _padded_col(kv, side)]
                vz[b, kv, side, WINDOW:keys, :] = vn_ref[new, _padded_col(kv, side)]
    for b in range(SAMPLE_BLOCK):
        rows = slice(b * DEC_SEQ, (b + 1) * DEC_SEQ)
        for kv in range(N_KV):
            c0 = kv * GROUP * HEAD_DIM
            qst = jnp.concatenate([q_ref[rows, c0:c0 + 2 * HEAD_DIM],
                                   q_ref[rows, c0 + 2 * HEAD_DIM:c0 + 4 * HEAD_DIM]], axis=0)
            o = _attend_pairs(qst, kz[b, kv].reshape(2 * SAMPLE_KEYS, PAIR),
                              vz[b, kv].reshape(2 * SAMPLE_KEYS, PAIR),
                              bias_ref[kv], [sink_ref[kv, side] for side in range(2)])
            o_ref[rows, c0:c0 + 2 * HEAD_DIM] = o[0:DEC_SEQ].astype(BF16)
            o_ref[rows, c0 + 2 * HEAD_DIM:c0 + 4 * HEAD_DIM] = o[DEC_SEQ:].astype(BF16)


def _attn_sample(qa, k, v, cache_k, cache_v, bias, sink, layer):
    rows = SAMPLE_BLOCK * DEC_SEQ
    first = SEQ // rows
    new = lambda width: pl.BlockSpec((rows, width), lambda i: (first + i, 0))
    cache = pl.BlockSpec((SAMPLE_BLOCK, WINDOW, KV_WIDTH), lambda i: (i, 0, 0))
    return pl.pallas_call(
        _attn_sample_kernel,
        grid=(DEC_BATCH // SAMPLE_BLOCK,),
        in_specs=[new(ATT_WIDTH), new(PADDED_WIDTH), new(PADDED_WIDTH), cache, cache,
                  _resident(bias.shape), _resident_layer(sink.shape, layer)],
        out_specs=pl.BlockSpec((rows, ATT_WIDTH), lambda i: (i, 0)),
        out_shape=jax.ShapeDtypeStruct((N_SAMPLE, ATT_WIDTH), BF16),
        scratch_shapes=[pltpu.VMEM((SAMPLE_BLOCK, N_KV, 2, SAMPLE_KEYS, PAIR), BF16)] * 2,
        compiler_params=_params("parallel"),
        name="attn_sample",
    )(qa, k, v, cache_k, cache_v, bias, sink)


def _retention_head(q, k, v, g, state, decay, zeta, xi, g_chunk, gain):
    scores = lax.dot_general(q, k, NT_DIMS, preferred_element_type=F32) * decay
    if q.shape[0] % RET_DIM == 0:
        lhs = jnp.concatenate([(q.astype(F32) * xi).astype(BF16), scores.astype(BF16)], axis=1)
        rhs = jnp.concatenate([state.astype(BF16), v], axis=0)
        o = jnp.dot(lhs, rhs, preferred_element_type=F32)
    else:
        o = jnp.dot(scores.astype(BF16), v, preferred_element_type=F32)
        o = o + jnp.dot(q, state.astype(BF16), preferred_element_type=F32) * xi
    kz = (k.astype(F32) * zeta).astype(BF16)
    upd = lax.dot_general(kz, v, TN_DIMS, preferred_element_type=F32)
    new_state = g_chunk * state + upd
    o = o * lax.rsqrt(jnp.mean(o * o, axis=-1, keepdims=True) + EPS) * gain
    gf = g.astype(F32)
    return gf * jax.nn.sigmoid(gf) * o, new_state


def _ret_sample_kernel(gc_ref, q_ref, k_ref, v_ref, g_ref, s_ref, decay_ref, zeta_ref, xi_ref,
                       gain_ref, *refs, opening):
    o_ref, st_ref = refs[-2:]
    if opening:
        st_ref[1:] = jnp.zeros((DEPTH - 1,) + st_ref.shape[1:], F32)
        st_ref = st_ref.at[0]
    for b in range(RET_SAMPLE_BLOCK):
        rows = slice(b * DEC_SEQ, (b + 1) * DEC_SEQ)
        for h in range(RET_HEADS):
            sl = slice(h * RET_DIM, (h + 1) * RET_DIM)
            o, s_new = _retention_head(q_ref[rows, sl], k_ref[rows, sl], v_ref[rows, sl],
                                       g_ref[rows, sl], s_ref[b, h], decay_ref[h], zeta_ref[h],
                                       xi_ref[h], gc_ref[h], gain_ref[:, sl])
            st_ref[b, h] = s_new
            o_ref[rows, sl] = o.astype(BF16)


def _ret_sample(qr, kr, vr, gr, state, layer, consts, gain, new_state):
    g_chunk, decay, zeta, xi = consts
    nb = RET_SAMPLE_BLOCK
    first = SEQ // (nb * DEC_SEQ)
    row = pl.BlockSpec((nb * DEC_SEQ, RET_WIDTH), lambda i: (first + i, 0))
    st_block = (nb, RET_HEADS, RET_DIM, RET_DIM)
    st_spec = pl.BlockSpec((None,) + st_block, lambda i: (layer, i, 0, 0, 0))
    in_specs = [pl.BlockSpec(memory_space=pltpu.SMEM), row, row, row, row, st_spec,
                _resident(decay.shape), _resident(zeta.shape), _resident(xi.shape),
                _resident(gain.shape)]
    args = [g_chunk, qr, kr, vr, gr, state, decay, zeta, xi, gain]
    opening = new_state is None
    if opening:
        assert layer == 0
        aliases = {}
        st_out = pl.BlockSpec((DEPTH,) + st_block, lambda i: (0, i, 0, 0, 0))
    else:
        args.append(new_state)
        in_specs.append(pl.BlockSpec(memory_space=pl.ANY))
        aliases = {len(args) - 1: 1}
        st_out = st_spec
    return pl.pallas_call(
        functools.partial(_ret_sample_kernel, opening=opening),
        grid=(DEC_BATCH // nb,),
        in_specs=in_specs,
        out_specs=(pl.BlockSpec((nb * DEC_SEQ, RET_WIDTH), lambda i: (i, 0)), st_out),
        out_shape=(jax.ShapeDtypeStruct((N_SAMPLE, RET_WIDTH), BF16),
                   jax.ShapeDtypeStruct(state.shape, F32)),
        input_output_aliases=aliases,
        compiler_params=_params("parallel"),
        name="ret_sample",
    )(*args)


def _retention_consts(chunk):
    lg = jnp.log(1.0 - 2.0 ** (-5.0 - jnp.arange(RET_HEADS, dtype=F32)))
    idx = jnp.arange(chunk, dtype=F32)
    diff = idx[:, None] - idx[None, :]
    decay = jnp.where(diff >= 0, jnp.exp(lg[:, None, None] * jnp.maximum(diff, 0.0)), 0.0)
    zeta = jnp.exp(lg[:, None] * (chunk - 1 - idx))[:, :, None]
    xi = jnp.exp(lg[:, None] * (idx + 1))[:, :, None]
    dense = (RET_HEADS, chunk, RET_DIM)
    return (jnp.exp(lg * chunk), decay,
            jnp.broadcast_to(zeta, dense), jnp.broadcast_to(xi, dense))


def _mix_out_kernel(gc_ref, q_ref, kc_ref, kp_ref, vc_ref, vp_ref, bias_ref, sink_ref,
                    qr_ref, kr_ref, vr_ref, gr_ref, decay_ref, zeta_ref, xi_ref, gain_ref,
                    oas_ref, ors_ref, hp_ref, hs_ref, w_ref, g_ref,
                    h1_ref, xn_ref, st_ref, state):
    i = pl.program_id(0)

    def band(cur_ref, prev_ref, j, kv):
        sides = []
        for side in range(2):
            col = _padded_col(kv, side)
            if j == 0:
                sides += [prev_ref[:, col], cur_ref[0:Q_BLOCK, col]]
            else:
                sides.append(cur_ref[j * Q_BLOCK - WINDOW:(j + 1) * Q_BLOCK, col])
        return jnp.concatenate(sides, axis=0)

    def project(rows, h, oa, orr):
        h1 = (h + jnp.dot(oa, w_ref[0:ATT_WIDTH, :], preferred_element_type=F32)
              + jnp.dot(orr, w_ref[ATT_WIDTH:, :], preferred_element_type=F32))
        h1_ref[rows, :] = h1
        xn_ref[rows, :] = _rms(h1, g_ref[...]).astype(BF16)

    @pl.when(i == 0)
    def _():
        state[...] = jnp.zeros(state.shape, F32)
        for rows in _row_parts():
            project(rows, hs_ref[rows, :], oas_ref[rows, :], ors_ref[rows, :])

    @pl.when(i > 0)
    def _():
        col = lax.broadcasted_iota(jnp.int32, (2 * Q_BLOCK, 2 * K_BLOCK), 1)
        before_start = jnp.logical_and(i == 1, col % K_BLOCK < WINDOW)
        for part, rows in enumerate(_row_parts(MIX_PARTS)):
            blocks = []
            for j in range(part * Q_PER_PART, (part + 1) * Q_PER_PART):
                qrows = slice(j * Q_BLOCK, (j + 1) * Q_BLOCK)
                pieces = []
                for kv in range(N_KV):
                    c0 = kv * GROUP * HEAD_DIM
                    qst = jnp.concatenate([q_ref[qrows, c0:c0 + PAIR],
                                           q_ref[qrows, c0 + PAIR:c0 + 2 * PAIR]], axis=0)
                    bias = bias_ref[kv]
                    if j == 0:
                        bias = jnp.where(before_start, -jnp.inf, bias)
                    o = _attend_pairs(qst, band(kc_ref, kp_ref, j, kv), band(vc_ref, vp_ref, j, kv),
                                      bias, [sink_ref[kv, side] for side in range(2)])
                    pieces += [o[0:Q_BLOCK].astype(BF16), o[Q_BLOCK:].astype(BF16)]
                blocks.append(jnp.concatenate(pieces, axis=1))
            oa = jnp.concatenate(blocks, axis=0)
            heads = []
            for h in range(RET_HEADS):
                sl = slice(h * RET_DIM, (h + 1) * RET_DIM)
                o, s_new = _retention_head(qr_ref[rows, sl], kr_ref[rows, sl], vr_ref[rows, sl],
                                           gr_ref[rows, sl], state[h], decay_ref[h], zeta_ref[h],
                                           xi_ref[h], gc_ref[h], gain_ref[:, sl])
                state[h] = s_new
                heads.append(o.astype(BF16))
            project(rows, hp_ref[rows, :], oa, jnp.concatenate(heads, axis=1))

    @pl.when(i == pl.num_programs(0) - 1)
    def _():
        st_ref[...] = state[...]


MIX_PARTS = ROW_TILE // RET_TILE
assert RET_TILE % Q_BLOCK == 0
Q_PER_PART = RET_TILE // Q_BLOCK


def _mix_out(qa, k, v, bias, sink, qr, kr, vr, gr, consts, gain, oa_s, or_s, hp, hs, w, g, layer):
    g_chunk, decay, zeta, xi = consts
    per = ROW_TILE // WINDOW
    prev = pl.BlockSpec((WINDOW, PADDED_WIDTH),
                        lambda i: (jnp.maximum(jnp.maximum(i - 1, 0) * per - 1, 0), 0))
    row = _stacked_rows
    st_shape = (RET_HEADS, RET_DIM, RET_DIM)
    return pl.pallas_call(
        _mix_out_kernel,
        grid=(N_TOK // ROW_TILE,),
        in_specs=[pl.BlockSpec(memory_space=pltpu.SMEM),
                  _prompt_rows(ATT_WIDTH), _prompt_rows(PADDED_WIDTH), prev,
                  _prompt_rows(PADDED_WIDTH), prev, _resident(bias.shape),
                  _resident_layer(sink.shape, layer),
                  _prompt_rows(RET_WIDTH), _prompt_rows(RET_WIDTH), _prompt_rows(RET_WIDTH),
                  _prompt_rows(RET_WIDTH), _resident(decay.shape), _resident(zeta.shape),
                  _resident(xi.shape), _resident(gain.shape),
                  _sample_rows(ATT_WIDTH), _sample_rows(RET_WIDTH),
                  _prompt_rows(D_MODEL), _sample_rows(D_MODEL),
                  _resident(w.shape), _resident_layer(g.shape, layer)],
        out_specs=(row(D_MODEL), row(D_MODEL), pl.BlockSpec(st_shape, lambda i: (0, 0, 0))),
        out_shape=(jax.ShapeDtypeStruct((N_TOK, D_MODEL), F32),
                   jax.ShapeDtypeStruct((N_TOK, D_MODEL), BF16),
                   jax.ShapeDtypeStruct(st_shape, F32)),
        scratch_shapes=[pltpu.VMEM(st_shape, F32)],
        compiler_params=_params("arbitrary", vmem_limit=MIX_OUT_VMEM_LIMIT),
        name="mix_out",
    )(g_chunk, qa, k, k, v, v, bias, sink, qr, kr, vr, gr, decay, zeta, xi, gain,
      oa_s, or_s, hp, hs, w, g)


def _mlp_kernel(xn_ref, wu_ref, wd_ref, *refs):
    n_extra = (len(refs) - 1) // 2
    o_ref = refs[n_extra]

    @pl.when(pl.program_id(1) == 0)
    def _():
        o_ref[...] = jnp.zeros(o_ref.shape, F32)

    up = jnp.dot(xn_ref[...], wu_ref[...].astype(BF16), preferred_element_type=F32)
    act = jnp.square(jnp.maximum(up, 0.0)).astype(BF16)
    o_ref[...] += jnp.dot(act, wd_ref[...].astype(BF16), preferred_element_type=F32)
    for src, dst in zip(refs[:n_extra], refs[n_extra + 1:]):
        dst[...] = src[...].astype(BF16)


def _mlp(xn, w_up, w_down, layer, cast_jobs=()):
    tm, tf = MLP_ROW_TILE, MLP_FF_TILE
    n_ff = D_FF // tf
    steps = (N_TOK // tm) * n_ff
    slab_in, slab_out, slab_shape = [], [], []
    for w, lyr in cast_jobs:
        _, k, n = w.shape
        rows = k // steps
        assert rows * steps == k and rows % 16 == 0
        slab_in.append(pl.BlockSpec((None, rows, n), lambda i, f, lyr=lyr: (lyr, i * n_ff + f, 0)))
        slab_out.append(pl.BlockSpec((rows, n), lambda i, f: (i * n_ff + f, 0)))
        slab_shape.append(jax.ShapeDtypeStruct((k, n), BF16))
    return pl.pallas_call(
        _mlp_kernel,
        grid=(N_TOK // tm, n_ff),
        in_specs=[pl.BlockSpec((tm, D_MODEL), lambda i, f: (i, 0)),
                  pl.BlockSpec((None, D_MODEL, tf), lambda i, f: (layer, 0, f)),
                  pl.BlockSpec((None, tf, D_MODEL), lambda i, f: (layer, f, 0))] + slab_in,
        out_specs=[pl.BlockSpec((tm, D_MODEL), lambda i, f: (i, 0))] + slab_out,
        out_shape=[jax.ShapeDtypeStruct((N_TOK, D_MODEL), F32)] + slab_shape,
        compiler_params=_params("arbitrary", "arbitrary"),
        name="mlp",
    )(xn, w_up, w_down, *[w for w, _ in cast_jobs])


def _gate_kernel(h1_ref, m_ref, pp_ref, ps_ref, wg_ref, wp_ref, g_ref, gf_ref, op_ref, os_ref,
                 *, final):
    for rows in _row_parts():
        h2 = h1_ref[rows, :] + m_ref[rows, :]
        xn = _rms(h2, g_ref[...]).astype(BF16)
        gate = jax.nn.sigmoid(jnp.dot(xn, wg_ref[...], preferred_element_type=F32))
        emb = jnp.dot(_pick_rows(pp_ref, ps_ref, rows).astype(BF16), wp_ref[...],
                      preferred_element_type=F32)
        h3 = h2 + gate * emb
        op_ref[rows, :] = _rms(h3, gf_ref[...]) if final else h3

    @pl.when(pl.program_id(0) == 0)
    def _():
        os_ref[...] = op_ref[...]


def _gate(h1, mlp_out, p_prompt, p_sample, layer, w_gate, w_ple, g, g_final, final):
    row = _stacked_rows
    return pl.pallas_call(
        functools.partial(_gate_kernel, final=final),
        grid=(N_TOK // ROW_TILE,),
        in_specs=[row(D_MODEL), row(D_MODEL),
                  _prompt_rows(PLE_DIM, layer, 0), _sample_rows(PLE_DIM, layer),
                  _resident(w_gate.shape), _resident_layer(w_ple.shape, layer),
                  _resident_layer(g.shape, layer), _resident(g_final.shape)],
        out_specs=(_prompt_rows(D_MODEL), _sample_rows(D_MODEL, single_buffer=False)),
        out_shape=(jax.ShapeDtypeStruct((SEQ, D_MODEL), F32),
                   jax.ShapeDtypeStruct((N_SAMPLE, D_MODEL), F32)),
        compiler_params=_params("arbitrary"),
        name="gate",
    )(h1, mlp_out, p_prompt, p_sample, w_gate, w_ple, g, g_final)


def _t5_bucket(rel):
    nb = N_BUCKETS // 2
    max_exact = nb // 2
    ret = jnp.where(rel > 0, nb, 0)
    n = jnp.abs(rel)
    nf = jnp.maximum(n, max_exact).astype(F32)
    large = max_exact + (jnp.log(nf / max_exact) / math.log(MAX_DISTANCE / max_exact)
                         * (nb - max_exact)).astype(jnp.int32)
    large = jnp.minimum(large, nb - 1)
    return ret + jnp.where(n < max_exact, n, large)


def _rotary_tables():
    half = RET_DIM // 2
    inv = ROPE_BASE ** (-jnp.arange(half, dtype=F32) / half)

    def cos_sin(pos):
        ang = pos.astype(F32)[:, None] * inv[None, :]
        return jnp.cos(ang), jnp.sin(ang)

    ca, sa = cos_sin(CHUNK * jnp.arange(SEQ // CHUNK, dtype=jnp.int32))
    cb, sb = cos_sin(jnp.arange(CHUNK, dtype=jnp.int32))
    cos_p = (ca[:, None] * cb[None] - sa[:, None] * sb[None]).reshape(SEQ, half)
    sin_p = (sa[:, None] * cb[None] + ca[:, None] * sb[None]).reshape(SEQ, half)
    cos_s, sin_s = cos_sin(PAST_LEN + jnp.arange(DEC_SEQ, dtype=jnp.int32))
    cos = jnp.concatenate([cos_p, jnp.tile(cos_s, (DEC_BATCH, 1))])
    sin = jnp.concatenate([sin_p, jnp.tile(sin_s, (DEC_BATCH, 1))])
    return jnp.concatenate([cos, cos], axis=-1), jnp.concatenate([-sin, sin], axis=-1)


def kernel(x_prompt, x_sample, cache_k, cache_v, state_ret, p_prompt, p_sample, rel_bias,
           norm_mix, w_in, attn_sink, ret_gain, w_out, norm_mlp, w_up, w_down, norm_ple,
           w_gate, w_ple, norm_final):
    q_idx = jnp.arange(Q_BLOCK, dtype=jnp.int32)[:, None]
    s_idx = jnp.arange(K_BLOCK, dtype=jnp.int32)[None, :]
    bucket_p = _t5_bucket(s_idx - WINDOW - q_idx)
    bucket_s = _t5_bucket(jnp.arange(WINDOW + DEC_SEQ, dtype=jnp.int32)[None, :] - WINDOW
                          - jnp.arange(DEC_SEQ, dtype=jnp.int32)[:, None])
    back = q_idx // CHUNK + WINDOW // CHUNK - s_idx // CHUNK
    seen = jnp.logical_and(back >= 0, back <= WINDOW // CHUNK)
    bucket_p = jnp.where(seen, bucket_p, -1)
    bucket_s = jnp.pad(bucket_s, ((0, 0), (0, SAMPLE_KEYS - WINDOW - DEC_SEQ)), constant_values=-1)
    bias_p, bias_s = _rel_bias(rel_bias, bucket_p, bucket_s)

    cos2, sin2 = _rotary_tables()
    consts_p = _retention_consts(RET_TILE)
    consts_s = _retention_consts(DEC_SEQ)

    w_in_bf = {0: w_in[0].astype(BF16)}
    w_out_bf = {0: w_out[0].astype(BF16)}
    w_gate_bf = {}
    w_ple = w_ple.astype(BF16)
    norm_mix, norm_mlp, norm_ple = (g.reshape(DEPTH, 1, D_MODEL)
                                    for g in (norm_mix, norm_mlp, norm_ple))
    hp = x_prompt.reshape(SEQ, D_MODEL)
    hs = x_sample.reshape(N_SAMPLE, D_MODEL)
    p_sample = p_sample.reshape(DEPTH, N_SAMPLE, PLE_DIM)
    k_p, v_p, s_p, k_s, v_s = [], [], [], [], []
    state_s = None
    sink = attn_sink.astype(F32).reshape(DEPTH, N_HEADS, 1, 1) * LOG2E
    sink_p = _pair_layout(jnp.broadcast_to(sink, (DEPTH, N_HEADS, SINK_ROWS, PAIR)))
    sink_s = sink_p
    for l in range(DEPTH):
        gain = ret_gain[l].reshape(1, RET_WIDTH)

        qa, k, v, kz, vz, qr, kr, vr, gr = _in_proj(hp, hs, norm_mix, w_in_bf[l], l, cos2, sin2)
        oa_s = _attn_sample(qa, kz, vz, cache_k[l].reshape(DEC_BATCH, WINDOW, KV_WIDTH),
                            cache_v[l].reshape(DEC_BATCH, WINDOW, KV_WIDTH), bias_s, sink_s, l)
        or_s, state_s = _ret_sample(qr, kr, vr, gr, state_ret, l, consts_s, gain, state_s)
        h1, xn, st_p = _mix_out(qa, kz, vz, bias_p, sink_p, qr, kr, vr, gr, consts_p, gain,
                                oa_s, or_s, hp, hs, w_out_bf[l], norm_mlp, l)
        jobs = [(w_gate, l)]
        if l + 1 < DEPTH:
            jobs += [(w_in, l + 1), (w_out, l + 1)]
        mlp_out, *cast = _mlp(xn, w_up, w_down, l, jobs)
        w_gate_bf[l] = cast[0]
        if l + 1 < DEPTH:
            w_in_bf[l + 1], w_out_bf[l + 1] = cast[1], cast[2]
        hp, hs = _gate(h1, mlp_out, p_prompt, p_sample, l, w_gate_bf[l], w_ple, norm_ple,
                       norm_final.reshape(1, D_MODEL), final=(l == DEPTH - 1))

        k_p.append(k[SEQ - WINDOW:SEQ].reshape(1, WINDOW, N_KV, HEAD_DIM))
        v_p.append(v[SEQ - WINDOW:SEQ].reshape(1, WINDOW, N_KV, HEAD_DIM))
        s_p.append(st_p[None])
        k_s.append(k[SEQ:].reshape(DEC_BATCH, DEC_SEQ, N_KV, HEAD_DIM))
        v_s.append(v[SEQ:].reshape(DEC_BATCH, DEC_SEQ, N_KV, HEAD_DIM))

    return (hp.reshape(1, SEQ, D_MODEL), hs.reshape(DEC_BATCH, DEC_SEQ, D_MODEL),
            jnp.stack(k_p), jnp.stack(v_p), jnp.stack(s_p),
            jnp.stack(k_s), jnp.stack(v_s), state_s)
```

```python
import functools
import math

import jax
import jax.numpy as jnp
from jax import lax
from jax.experimental import pallas as pl
from jax.experimental.pallas import tpu as pltpu

F32 = jnp.float32
BF16 = jnp.bfloat16

D_MODEL = 2048
SEQ = 8192
DEPTH = 2
DEC_BATCH = 32
DEC_SEQ = 16
PAST_LEN = 4096
N_SAMPLE = DEC_BATCH * DEC_SEQ
N_TOK = SEQ + N_SAMPLE

CHUNK = 64
WINDOW = 128
HEAD_DIM = 64
N_HEADS = 16
N_KV = 4
GROUP = N_HEADS // N_KV
ATT_WIDTH = N_HEADS * HEAD_DIM
KV_WIDTH = N_KV * HEAD_DIM
PADDED_WIDTH = N_KV * 2 * 2 * HEAD_DIM
RET_DIM = 128
RET_HEADS = 8
RET_WIDTH = RET_HEADS * RET_DIM
IN_WIDTH = ATT_WIDTH + 2 * KV_WIDTH + 4 * RET_WIDTH
D_FF = 4 * D_MODEL
PLE_DIM = 256
N_BUCKETS = 32
MAX_DISTANCE = 128
ROPE_BASE = 10000.0
EPS = 1e-6
ATT_SCALE = HEAD_DIM ** -0.5
LOG2E = 1.0 / math.log(2.0)
RET_SCALE = RET_DIM ** -0.5

ROW_TILE = 512
ROW_PARTS = 2
MLP_ROW_TILE = 1088
MLP_FF_TILE = 512
Q_BLOCK = 2 * CHUNK
K_BLOCK = WINDOW + Q_BLOCK
RET_TILE = 256
SAMPLE_BLOCK = 8
SAMPLE_KEYS = 256
RET_SAMPLE_BLOCK = 4
VMEM_LIMIT = 58 * 1024 * 1024
MIX_OUT_VMEM_LIMIT = 63 * 1024 * 1024

NT_DIMS = (((1,), (1,)), ((), ()))
TN_DIMS = (((0,), (0,)), ((), ()))


def _params(*sem, vmem_limit=VMEM_LIMIT):
    return pltpu.CompilerParams(dimension_semantics=sem, vmem_limit_bytes=vmem_limit)


def _resident(shape):
    nd = len(shape)
    return pl.BlockSpec(shape, lambda *_: (0,) * nd, pipeline_mode=pl.Buffered(1))


def _resident_layer(shape, layer):
    nd = len(shape) - 1
    return pl.BlockSpec((None,) + tuple(shape[1:]), lambda *_: (layer,) + (0,) * nd,
                        pipeline_mode=pl.Buffered(1))


def _rms(x, g):
    ms = jnp.mean(x * x, axis=-1, keepdims=True)
    return x * lax.rsqrt(ms + EPS) * g


N_PROMPT_TILES = SEQ // ROW_TILE
assert N_SAMPLE == ROW_TILE


def _stacked_rows(width):
    return pl.BlockSpec((ROW_TILE, width),
                        lambda i: ((i + N_PROMPT_TILES) % (N_PROMPT_TILES + 1), 0))


def _prompt_rows(width, *lead):
    none = (None,) * len(lead)
    return pl.BlockSpec(none + (ROW_TILE, width), lambda i: lead + (jnp.maximum(i - 1, 0), 0))


def _sample_rows(width, *lead, single_buffer=True):
    none = (None,) * len(lead)
    mode = dict(pipeline_mode=pl.Buffered(1)) if single_buffer else {}
    return pl.BlockSpec(none + (ROW_TILE, width), lambda i: lead + (0, 0), **mode)


def _row_parts(parts=ROW_PARTS):
    part = ROW_TILE // parts
    return [slice(r * part, (r + 1) * part) for r in range(parts)]


def _pick_rows(prompt_ref, sample_ref, rows=slice(None)):
    return jnp.where(pl.program_id(0) == 0, sample_ref[rows, :], prompt_ref[rows, :])


def _bias_kernel(table_ref, bp_ref, bs_ref, op_ref, os_ref):
    bp = bp_ref[...]
    bs = bs_ref[...]

    def lookup(bucket, head):
        acc = jnp.full(bucket.shape, -jnp.inf, F32)
        for b in range(N_BUCKETS):
            acc = jnp.where(bucket == b, table_ref[b, head] * LOG2E, acc)
        return acc

    def per_kv(kv, c):
        for half in range(2):
            for side in range(2):
                head = kv * GROUP + 2 * half + side
                rp, cp = bp.shape
                op_ref[kv, half * rp:(half + 1) * rp, side * cp:(side + 1) * cp] = lookup(bp, head)
                rs, cs = bs.shape
                os_ref[kv, half * rs:(half + 1) * rs, side * cs:(side + 1) * cs] = lookup(bs, head)
        return c

    lax.fori_loop(0, N_KV, per_kv, 0)


def _rel_bias(table, bucket_p, bucket_s):
    def out(bucket):
        rows, cols = bucket.shape
        return jax.ShapeDtypeStruct((N_KV, 2 * rows, 2 * cols), F32)

    return pl.pallas_call(
        _bias_kernel,
        out_shape=(out(bucket_p), out(bucket_s)),
        in_specs=[pl.BlockSpec(memory_space=pltpu.SMEM),
                  pl.BlockSpec(memory_space=pltpu.VMEM),
                  pl.BlockSpec(memory_space=pltpu.VMEM)],
        out_specs=(pl.BlockSpec(memory_space=pltpu.VMEM), pl.BlockSpec(memory_space=pltpu.VMEM)),
        name="rel_bias",
    )(table, bucket_p, bucket_s)


def _in_proj_kernel(hp_ref, hs_ref, g_ref, w_ref, cos_ref, sin_ref, *refs):
    if len(refs) > 9:
        refs[-1][...] = refs[0][...].astype(BF16)
        refs = refs[1:-1]
    qa_ref, k_ref, v_ref, kz_ref, vz_ref, qr_ref, kr_ref, vr_ref, gr_ref = refs
    base = ATT_WIDTH + 2 * KV_WIDTH
    for rows in _row_parts():
        xn = _rms(_pick_rows(hp_ref, hs_ref, rows), g_ref[...]).astype(BF16)

        def proj(lo, width):
            return jnp.dot(xn, w_ref[:, lo:lo + width], preferred_element_type=F32)

        qa_ref[rows, :] = (proj(0, ATT_WIDTH) * (ATT_SCALE * LOG2E)).astype(BF16)
        for lo, plain_ref, padded_ref in ((ATT_WIDTH, k_ref, kz_ref),
                                          (ATT_WIDTH + KV_WIDTH, v_ref, vz_ref)):
            x = proj(lo, KV_WIDTH)
            plain_ref[rows, :] = x
            sides = _padded_sides(x)
            for kv in range(N_KV):
                for side in range(2):
                    padded_ref[rows, _padded_col(kv, side)] = sides[kv][side]
        cos = cos_ref[rows, :]
        sin = sin_ref[rows, :]

        def rot(x):
            return x * cos + pltpu.roll(x, RET_DIM // 2, 1) * sin

        q = proj(base, RET_WIDTH)
        for h in range(RET_HEADS):
            sl = slice(h * RET_DIM, (h + 1) * RET_DIM)
            qr_ref[rows, sl] = rot(q[:, sl]).astype(BF16)
        k = proj(base + RET_WIDTH, RET_WIDTH)
        for h in range(RET_HEADS):
            sl = slice(h * RET_DIM, (h + 1) * RET_DIM)
            kr_ref[rows, sl] = (rot(k[:, sl]) * RET_SCALE).astype(BF16)
        vr_ref[rows, :] = proj(base + 2 * RET_WIDTH, RET_WIDTH).astype(BF16)
        gr_ref[rows, :] = proj(base + 3 * RET_WIDTH, RET_WIDTH).astype(BF16)


def _in_proj(hp, hs, g, w, layer, cos2, sin2, cast_job=None):
    tm = ROW_TILE
    row = _stacked_rows
    out = lambda width, dt: jax.ShapeDtypeStruct((N_TOK, width), dt)
    in_specs = [_prompt_rows(D_MODEL), _sample_rows(D_MODEL),
                _resident_layer(g.shape, layer), _resident(w.shape),
                row(RET_DIM), row(RET_DIM)]
    out_specs = [row(ATT_WIDTH), row(KV_WIDTH), row(KV_WIDTH),
                 row(PADDED_WIDTH), row(PADDED_WIDTH),
                 row(RET_WIDTH), row(RET_WIDTH), row(RET_WIDTH), row(RET_WIDTH)]
    out_shape = [out(ATT_WIDTH, BF16), out(KV_WIDTH, F32), out(KV_WIDTH, F32),
                 out(PADDED_WIDTH, BF16), out(PADDED_WIDTH, BF16),
                 out(RET_WIDTH, BF16), out(RET_WIDTH, BF16), out(RET_WIDTH, BF16),
                 out(RET_WIDTH, BF16)]
    args = [hp, hs, g, w, cos2, sin2]
    limit = VMEM_LIMIT
    if cast_job is not None:
        src, lyr = cast_job
        _, k, n = src.shape
        rows = k // N_PROMPT_TILES
        assert rows * N_PROMPT_TILES == k
        in_specs.append(pl.BlockSpec((None, rows, n), lambda i: (lyr, jnp.maximum(i - 1, 0), 0)))
        out_specs.append(pl.BlockSpec((rows, n), lambda i: (jnp.maximum(i - 1, 0), 0)))
        out_shape.append(jax.ShapeDtypeStruct((k, n), BF16))
        args.append(src)
        limit = MIX_OUT_VMEM_LIMIT
    return pl.pallas_call(
        _in_proj_kernel,
        grid=(N_TOK // tm,),
        in_specs=in_specs,
        out_specs=out_specs,
        out_shape=out_shape,
        compiler_params=_params("arbitrary", vmem_limit=limit),
        name="in_proj",
    )(*args)


PAIR = 2 * HEAD_DIM
SINK_ROWS = 8


def _attend_pairs(qst, kcat, vcat, bias, sinks):
    reps = qst.shape[0] // (2 * SINK_ROWS)
    sinks = [jnp.concatenate([s[0:SINK_ROWS]] * reps + [s[SINK_ROWS:]] * reps, axis=0)
             for s in sinks]
    half = kcat.shape[0] // 2
    side0 = lax.broadcasted_iota(jnp.int32, vcat.shape, 0) < half
    low = lax.broadcasted_iota(jnp.int32, vcat.shape, 1) < HEAD_DIM
    vext = jnp.concatenate([vcat, jnp.where(side0 == low, 1.0, 0.0).astype(BF16)], axis=1)
    s = lax.dot_general(qst, kcat, NT_DIMS, preferred_element_type=F32) + bias
    ps, es = [], []
    for side in range(2):
        ss = s[:, side * half:(side + 1) * half]
        m = jnp.maximum(jnp.max(ss, axis=-1, keepdims=True), sinks[side])
        ps.append(jnp.exp2(ss - jnp.concatenate([m] * (half // PAIR), axis=1)))
        es.append(jnp.exp2(sinks[side] - m))
    p = jnp.concatenate(ps, axis=1).astype(BF16)
    r = jnp.dot(p, vext, preferred_element_type=F32)
    lo = lax.broadcasted_iota(jnp.int32, qst.shape, 1) < HEAD_DIM
    den = r[:, PAIR:] + jnp.where(lo, es[0], es[1])
    return r[:, :PAIR] / den


def _padded_sides(x):
    lo = lax.broadcasted_iota(jnp.int32, (x.shape[0], PAIR), 1) < HEAD_DIM
    out = []
    for pair in range(N_KV // 2):
        xp = x[:, pair * PAIR:(pair + 1) * PAIR]
        xr = pltpu.roll(xp, HEAD_DIM, 1)
        zero = jnp.zeros_like(xp)
        out.append([jnp.where(lo, xp, zero).astype(BF16), jnp.where(lo, zero, xr).astype(BF16)])
        out.append([jnp.where(lo, xr, zero).astype(BF16), jnp.where(lo, zero, xp).astype(BF16)])
    return out


def _padded_col(kv, side):
    c0 = (2 * kv + side) * PAIR
    return slice(c0, c0 + PAIR)


def _fill_padded(dst, rows, x):
    sides = _padded_sides(x)
    for kv in range(N_KV):
        for side in range(2):
            dst[kv, side, rows, :] = sides[kv][side]


def _pair_layout(x):
    lead = x.shape[:-3]
    rows, cols = x.shape[-2:]
    x = x.reshape(lead + (N_KV, 2, 2, rows, cols))
    x = jnp.swapaxes(x, -4, -3)
    return x.reshape(lead + (N_KV, 2, 2 * rows, cols))


def _attn_sample_kernel(q_ref, kn_ref, vn_ref, ck_ref, cv_ref, bias_ref, sink_ref, o_ref, kz, vz):
    keys = WINDOW + DEC_SEQ
    pad = jnp.zeros((SAMPLE_BLOCK, N_KV, 2, SAMPLE_KEYS - keys, PAIR), BF16)
    kz[:, :, :, keys:, :] = pad
    vz[:, :, :, keys:, :] = pad
    for b in range(SAMPLE_BLOCK):
        new = slice(b * DEC_SEQ, (b + 1) * DEC_SEQ)
        _fill_padded(kz.at[b], slice(0, WINDOW), ck_ref[b])
        _fill_padded(vz.at[b], slice(0, WINDOW), cv_ref[b])
        for kv in range(N_KV):
            for side in range(2):
                kz[b, kv, side, WINDOW:keys, :] = kn_ref[new, _padded_col(kv, side)]
                vz[b, kv, side, WINDOW:keys, :] = vn_ref[new, _padded_col(kv, side)]
    for b in range(SAMPLE_BLOCK):
        rows = slice(b * DEC_SEQ, (b + 1) * DEC_SEQ)
        for kv in range(N_KV):
            c0 = kv * GROUP * HEAD_DIM
            qst = jnp.concatenate([q_ref[rows, c0:c0 + 2 * HEAD_DIM],
                                   q_ref[rows, c0 + 2 * HEAD_DIM:c0 + 4 * HEAD_DIM]], axis=0)
            o = _attend_pairs(qst, kz[b, kv].reshape(2 * SAMPLE_KEYS, PAIR),
                              vz[b, kv].reshape(2 * SAMPLE_KEYS, PAIR),
                              bias_ref[kv], [sink_ref[kv, side] for side in range(2)])
            o_ref[rows, c0:c0 + 2 * HEAD_DIM] = o[0:DEC_SEQ].astype(BF16)
            o_ref[rows, c0 + 2 * HEAD_DIM:c0 + 4 * HEAD_DIM] = o[DEC_SEQ:].astype(BF16)


def _attn_sample(qa, k, v, cache_k, cache_v, bias, sink, layer):
    rows = SAMPLE_BLOCK * DEC_SEQ
    first = SEQ // rows
    new = lambda width: pl.BlockSpec((rows, width), lambda i: (first + i, 0))
    cache = pl.BlockSpec((SAMPLE_BLOCK, WINDOW, KV_WIDTH), lambda i: (i, 0, 0))
    return pl.pallas_call(
        _attn_sample_kernel,
        grid=(DEC_BATCH // SAMPLE_BLOCK,),
        in_specs=[new(ATT_WIDTH), new(PADDED_WIDTH), new(PADDED_WIDTH), cache, cache,
                  _resident(bias.shape), _resident_layer(sink.shape, layer)],
        out_specs=pl.BlockSpec((rows, ATT_WIDTH), lambda i: (i, 0)),
        out_shape=jax.ShapeDtypeStruct((N_SAMPLE, ATT_WIDTH), BF16),
        scratch_shapes=[pltpu.VMEM((SAMPLE_BLOCK, N_KV, 2, SAMPLE_KEYS, PAIR), BF16)] * 2,
        compiler_params=_params("parallel"),
        name="attn_sample",
    )(qa, k, v, cache_k, cache_v, bias, sink)


def _retention_head(q, k, v, g, state, decay, zeta, xi, g_chunk, gain):
    scores = lax.dot_general(q, k, NT_DIMS, preferred_element_type=F32) * decay
    if q.shape[0] % RET_DIM == 0:
        lhs = jnp.concatenate([(q.astype(F32) * xi).astype(BF16), scores.astype(BF16)], axis=1)
        rhs = jnp.concatenate([state.astype(BF16), v], axis=0)
        o = jnp.dot(lhs, rhs, preferred_element_type=F32)
    else:
        o = jnp.dot(scores.astype(BF16), v, preferred_element_type=F32)
        o = o + jnp.dot(q, state.astype(BF16), preferred_element_type=F32) * xi
    kz = (k.astype(F32) * zeta).astype(BF16)
    upd = lax.dot_general(kz, v, TN_DIMS, preferred_element_type=F32)
    new_state = g_chunk * state + upd
    o = o * lax.rsqrt(jnp.mean(o * o, axis=-1, keepdims=True) + EPS) * gain
    gf = g.astype(F32)
    return gf * jax.nn.sigmoid(gf) * o, new_state


def _ret_sample_kernel(gc_ref, q_ref, k_ref, v_ref, g_ref, s_ref, decay_ref, zeta_ref, xi_ref,
                       gain_ref, *refs, opening):
    o_ref, st_ref = refs[-2:]
    if opening:
        st_ref[1:] = jnp.zeros((DEPTH - 1,) + st_ref.shape[1:], F32)
        st_ref = st_ref.at[0]
    for b in range(RET_SAMPLE_BLOCK):
        rows = slice(b * DEC_SEQ, (b + 1) * DEC_SEQ)
        for h in range(RET_HEADS):
            sl = slice(h * RET_DIM, (h + 1) * RET_DIM)
            o, s_new = _retention_head(q_ref[rows, sl], k_ref[rows, sl], v_ref[rows, sl],
                                       g_ref[rows, sl], s_ref[b, h], decay_ref[h], zeta_ref[h],
                                       xi_ref[h], gc_ref[h], gain_ref[:, sl])
            st_ref[b, h] = s_new
            o_ref[rows, sl] = o.astype(BF16)


def _ret_sample(qr, kr, vr, gr, state, layer, consts, gain, new_state):
    g_chunk, decay, zeta, xi = consts
    nb = RET_SAMPLE_BLOCK
    first = SEQ // (nb * DEC_SEQ)
    row = pl.BlockSpec((nb * DEC_SEQ, RET_WIDTH), lambda i: (first + i, 0))
    st_block = (nb, RET_HEADS, RET_DIM, RET_DIM)
    st_spec = pl.BlockSpec((None,) + st_block, lambda i: (layer, i, 0, 0, 0))
    in_specs = [pl.BlockSpec(memory_space=pltpu.SMEM), row, row, row, row, st_spec,
                _resident(decay.shape), _resident(zeta.shape), _resident(xi.shape),
                _resident(gain.shape)]
    args = [g_chunk, qr, kr, vr, gr, state, decay, zeta, xi, gain]
    opening = new_state is None
    if opening:
        assert layer == 0
        aliases = {}
        st_out = pl.BlockSpec((DEPTH,) + st_block, lambda i: (0, i, 0, 0, 0))
    else:
        args.append(new_state)
        in_specs.append(pl.BlockSpec(memory_space=pl.ANY))
        aliases = {len(args) - 1: 1}
        st_out = st_spec
    return pl.pallas_call(
        functools.partial(_ret_sample_kernel, opening=opening),
        grid=(DEC_BATCH // nb,),
        in_specs=in_specs,
        out_specs=(pl.BlockSpec((nb * DEC_SEQ, RET_WIDTH), lambda i: (i, 0)), st_out),
        out_shape=(jax.ShapeDtypeStruct((N_SAMPLE, RET_WIDTH), BF16),
                   jax.ShapeDtypeStruct(state.shape, F32)),
        input_output_aliases=aliases,
        compiler_params=_params("parallel"),
        name="ret_sample",
    )(*args)


def _retention_consts(chunk):
    lg = jnp.log(1.0 - 2.0 ** (-5.0 - jnp.arange(RET_HEADS, dtype=F32)))
    idx = jnp.arange(chunk, dtype=F32)
    diff = idx[:, None] - idx[None, :]
    decay = jnp.where(diff >= 0, jnp.exp(lg[:, None, None] * jnp.maximum(diff, 0.0)), 0.0)
    zeta = jnp.exp(lg[:, None] * (chunk - 1 - idx))[:, :, None]
    xi = jnp.exp(lg[:, None] * (idx + 1))[:, :, None]
    dense = (RET_HEADS, chunk, RET_DIM)
    return (jnp.exp(lg * chunk), decay,
            jnp.broadcast_to(zeta, dense), jnp.broadcast_to(xi, dense))


def _mix_out_kernel(gc_ref, q_ref, kc_ref, kp_ref, vc_ref, vp_ref, bias_ref, sink_ref,
                    qr_ref, kr_ref, vr_ref, gr_ref, decay_ref, zeta_ref, xi_ref, gain_ref,
                    oas_ref, ors_ref, hp_ref, hs_ref, w_ref, g_ref,
                    h1_ref, xn_ref, st_ref, state):
    i = pl.program_id(0)

    def band(cur_ref, prev_ref, j, kv):
        sides = []
        for side in range(2):
            col = _padded_col(kv, side)
            if j == 0:
                sides += [prev_ref[:, col], cur_ref[0:Q_BLOCK, col]]
            else:
                sides.append(cur_ref[j * Q_BLOCK - WINDOW:(j + 1) * Q_BLOCK, col])
        return jnp.concatenate(sides, axis=0)

    def project(rows, h, oa, orr):
        h1 = (h + jnp.dot(oa, w_ref[0:ATT_WIDTH, :], preferred_element_type=F32)
              + jnp.dot(orr, w_ref[ATT_WIDTH:, :], preferred_element_type=F32))
        h1_ref[rows, :] = h1
        xn_ref[rows, :] = _rms(h1, g_ref[...]).astype(BF16)

    @pl.when(i == 0)
    def _():
        state[...] = jnp.zeros(state.shape, F32)
        for rows in _row_parts():
            project(rows, hs_ref[rows, :], oas_ref[rows, :], ors_ref[rows, :])

    @pl.when(i > 0)
    def _():
        col = lax.broadcasted_iota(jnp.int32, (2 * Q_BLOCK, 2 * K_BLOCK), 1)
        before_start = jnp.logical_and(i == 1, col % K_BLOCK < WINDOW)
        for part, rows in enumerate(_row_parts(MIX_PARTS)):
            blocks = []
            for j in range(part * Q_PER_PART, (part + 1) * Q_PER_PART):
                qrows = slice(j * Q_BLOCK, (j + 1) * Q_BLOCK)
                pieces = []
                for kv in range(N_KV):
                    c0 = kv * GROUP * HEAD_DIM
                    qst = jnp.concatenate([q_ref[qrows, c0:c0 + PAIR],
                                           q_ref[qrows, c0 + PAIR:c0 + 2 * PAIR]], axis=0)
                    bias = bias_ref[kv]
                    if j == 0:
                        bias = jnp.where(before_start, -jnp.inf, bias)
                    o = _attend_pairs(qst, band(kc_ref, kp_ref, j, kv), band(vc_ref, vp_ref, j, kv),
                                      bias, [sink_ref[kv, side] for side in range(2)])
                    pieces += [o[0:Q_BLOCK].astype(BF16), o[Q_BLOCK:].astype(BF16)]
                blocks.append(jnp.concatenate(pieces, axis=1))
            oa = jnp.concatenate(blocks, axis=0)
            heads = []
            for h in range(RET_HEADS):
                sl = slice(h * RET_DIM, (h + 1) * RET_DIM)
                o, s_new = _retention_head(qr_ref[rows, sl], kr_ref[rows, sl], vr_ref[rows, sl],
                                           gr_ref[rows, sl], state[h], decay_ref[h], zeta_ref[h],
                                           xi_ref[h], gc_ref[h], gain_ref[:, sl])
                state[h] = s_new
                heads.append(o.astype(BF16))
            project(rows, hp_ref[rows, :], oa, jnp.concatenate(heads, axis=1))

    @pl.when(i == pl.num_programs(0) - 1)
    def _():
        st_ref[...] = state[...]


MIX_PARTS = ROW_TILE // RET_TILE
assert RET_TILE % Q_BLOCK == 0
Q_PER_PART = RET_TILE // Q_BLOCK


def _mix_out(qa, k, v, bias, sink, qr, kr, vr, gr, consts, gain, oa_s, or_s, hp, hs, w, g, layer):
    g_chunk, decay, zeta, xi = consts
    per = ROW_TILE // WINDOW
    prev = pl.BlockSpec((WINDOW, PADDED_WIDTH),
                        lambda i: (jnp.maximum(jnp.maximum(i - 1, 0) * per - 1, 0), 0))
    row = _stacked_rows
    st_shape = (RET_HEADS, RET_DIM, RET_DIM)
    return pl.pallas_call(
        _mix_out_kernel,
        grid=(N_TOK // ROW_TILE,),
        in_specs=[pl.BlockSpec(memory_space=pltpu.SMEM),
                  _prompt_rows(ATT_WIDTH), _prompt_rows(PADDED_WIDTH), prev,
                  _prompt_rows(PADDED_WIDTH), prev, _resident(bias.shape),
                  _resident_layer(sink.shape, layer),
                  _prompt_rows(RET_WIDTH), _prompt_rows(RET_WIDTH), _prompt_rows(RET_WIDTH),
                  _prompt_rows(RET_WIDTH), _resident(decay.shape), _resident(zeta.shape),
                  _resident(xi.shape), _resident(gain.shape),
                  _sample_rows(ATT_WIDTH), _sample_rows(RET_WIDTH),
                  _prompt_rows(D_MODEL), _sample_rows(D_MODEL),
                  _resident(w.shape), _resident_layer(g.shape, layer)],
        out_specs=(row(D_MODEL), row(D_MODEL), pl.BlockSpec(st_shape, lambda i: (0, 0, 0))),
        out_shape=(jax.ShapeDtypeStruct((N_TOK, D_MODEL), F32),
                   jax.ShapeDtypeStruct((N_TOK, D_MODEL), BF16),
                   jax.ShapeDtypeStruct(st_shape, F32)),
        scratch_shapes=[pltpu.VMEM(st_shape, F32)],
        compiler_params=_params("arbitrary", vmem_limit=MIX_OUT_VMEM_LIMIT),
        name="mix_out",
    )(g_chunk, qa, k, k, v, v, bias, sink, qr, kr, vr, gr, decay, zeta, xi, gain,
      oa_s, or_s, hp, hs, w, g)


def _mlp_kernel(xn_ref, wu_ref, wd_ref, *refs):
    n_extra = (len(refs) - 1) // 2
    o_ref = refs[n_extra]

    @pl.when(pl.program_id(1) == 0)
    def _():
        o_ref[...] = jnp.zeros(o_ref.shape, F32)

    up = jnp.dot(xn_ref[...], wu_ref[...].astype(BF16), preferred_element_type=F32)
    act = jnp.square(jnp.maximum(up, 0.0)).astype(BF16)
    o_ref[...] += jnp.dot(act, wd_ref[...].astype(BF16), preferred_element_type=F32)
    for src, dst in zip(refs[:n_extra], refs[n_extra + 1:]):
        dst[...] = src[...].astype(BF16)


def _mlp(xn, w_up, w_down, layer, cast_jobs=()):
    tm, tf = MLP_ROW_TILE, MLP_FF_TILE
    n_ff = D_FF // tf
    steps = (N_TOK // tm) * n_ff
    slab_in, slab_out, slab_shape = [], [], []
    for w, lyr in cast_jobs:
        _, k, n = w.shape
        rows = k // steps
        assert rows * steps == k and rows % 16 == 0
        slab_in.append(pl.BlockSpec((None, rows, n), lambda i, f, lyr=lyr: (lyr, i * n_ff + f, 0)))
        slab_out.append(pl.BlockSpec((rows, n), lambda i, f: (i * n_ff + f, 0)))
        slab_shape.append(jax.ShapeDtypeStruct((k, n), BF16))
    return pl.pallas_call(
        _mlp_kernel,
        grid=(N_TOK // tm, n_ff),
        in_specs=[pl.BlockSpec((tm, D_MODEL), lambda i, f: (i, 0)),
                  pl.BlockSpec((None, D_MODEL, tf), lambda i, f: (layer, 0, f)),
                  pl.BlockSpec((None, tf, D_MODEL), lambda i, f: (layer, f, 0))] + slab_in,
        out_specs=[pl.BlockSpec((tm, D_MODEL), lambda i, f: (i, 0))] + slab_out,
        out_shape=[jax.ShapeDtypeStruct((N_TOK, D_MODEL), F32)] + slab_shape,
        compiler_params=_params("arbitrary", "arbitrary"),
        name="mlp",
    )(xn, w_up, w_down, *[w for w, _ in cast_jobs])


def _gate_kernel(h1_ref, m_ref, pp_ref, ps_ref, wg_ref, wp_ref, g_ref, gf_ref, op_ref, os_ref,
                 *, final):
    for rows in _row_parts():
        emb = jnp.dot(_pick_rows(pp_ref, ps_ref, rows).astype(BF16), wp_ref[...],
                      preferred_element_type=F32)
        h2 = h1_ref[rows, :] + m_ref[rows, :]
        xn = _rms(h2, g_ref[...]).astype(BF16)
        gate = jax.nn.sigmoid(jnp.dot(xn, wg_ref[...], preferred_element_type=F32))
        h3 = h2 + gate * emb
        op_ref[rows, :] = _rms(h3, gf_ref[...]) if final else h3

    @pl.when(pl.program_id(0) == 0)
    def _():
        os_ref[...] = op_ref[...]


def _gate(h1, mlp_out, p_prompt, p_sample, layer, w_gate, w_ple, g, g_final, final):
    row = _stacked_rows
    return pl.pallas_call(
        functools.partial(_gate_kernel, final=final),
        grid=(N_TOK // ROW_TILE,),
        in_specs=[row(D_MODEL), row(D_MODEL),
                  _prompt_rows(PLE_DIM, layer, 0), _sample_rows(PLE_DIM, layer),
                  _resident(w_gate.shape), _resident_layer(w_ple.shape, layer),
                  _resident_layer(g.shape, layer), _resident(g_final.shape)],
        out_specs=(_prompt_rows(D_MODEL), _sample_rows(D_MODEL, single_buffer=False)),
        out_shape=(jax.ShapeDtypeStruct((SEQ, D_MODEL), F32),
                   jax.ShapeDtypeStruct((N_SAMPLE, D_MODEL), F32)),
        compiler_params=_params("arbitrary"),
        name="gate",
    )(h1, mlp_out, p_prompt, p_sample, w_gate, w_ple, g, g_final)


def _t5_bucket(rel):
    nb = N_BUCKETS // 2
    max_exact = nb // 2
    ret = jnp.where(rel > 0, nb, 0)
    n = jnp.abs(rel)
    nf = jnp.maximum(n, max_exact).astype(F32)
    large = max_exact + (jnp.log(nf / max_exact) / math.log(MAX_DISTANCE / max_exact)
                         * (nb - max_exact)).astype(jnp.int32)
    large = jnp.minimum(large, nb - 1)
    return ret + jnp.where(n < max_exact, n, large)


def _rotary_tables():
    half = RET_DIM // 2
    inv = ROPE_BASE ** (-jnp.arange(half, dtype=F32) / half)

    def cos_sin(pos):
        ang = pos.astype(F32)[:, None] * inv[None, :]
        return jnp.cos(ang), jnp.sin(ang)

    ca, sa = cos_sin(CHUNK * jnp.arange(SEQ // CHUNK, dtype=jnp.int32))
    cb, sb = cos_sin(jnp.arange(CHUNK, dtype=jnp.int32))
    cos_p = (ca[:, None] * cb[None] - sa[:, None] * sb[None]).reshape(SEQ, half)
    sin_p = (sa[:, None] * cb[None] + ca[:, None] * sb[None]).reshape(SEQ, half)
    cos_s, sin_s = cos_sin(PAST_LEN + jnp.arange(DEC_SEQ, dtype=jnp.int32))
    cos = jnp.concatenate([cos_p, jnp.tile(cos_s, (DEC_BATCH, 1))])
    sin = jnp.concatenate([sin_p, jnp.tile(sin_s, (DEC_BATCH, 1))])
    return jnp.concatenate([cos, cos], axis=-1), jnp.concatenate([-sin, sin], axis=-1)


def kernel(x_prompt, x_sample, cache_k, cache_v, state_ret, p_prompt, p_sample, rel_bias,
           norm_mix, w_in, attn_sink, ret_gain, w_out, norm_mlp, w_up, w_down, norm_ple,
           w_gate, w_ple, norm_final):
    q_idx = jnp.arange(Q_BLOCK, dtype=jnp.int32)[:, None]
    s_idx = jnp.arange(K_BLOCK, dtype=jnp.int32)[None, :]
    bucket_p = _t5_bucket(s_idx - WINDOW - q_idx)
    bucket_s = _t5_bucket(jnp.arange(WINDOW + DEC_SEQ, dtype=jnp.int32)[None, :] - WINDOW
                          - jnp.arange(DEC_SEQ, dtype=jnp.int32)[:, None])
    back = q_idx // CHUNK + WINDOW // CHUNK - s_idx // CHUNK
    seen = jnp.logical_and(back >= 0, back <= WINDOW // CHUNK)
    bucket_p = jnp.where(seen, bucket_p, -1)
    bucket_s = jnp.pad(bucket_s, ((0, 0), (0, SAMPLE_KEYS - WINDOW - DEC_SEQ)), constant_values=-1)
    bias_p, bias_s = _rel_bias(rel_bias, bucket_p, bucket_s)

    cos2, sin2 = _rotary_tables()
    consts_p = _retention_consts(RET_TILE)
    consts_s = _retention_consts(DEC_SEQ)

    w_in_bf = {0: w_in[0].astype(BF16)}
    w_gate_bf = {}
    w_ple = w_ple.astype(BF16)
    norm_mix, norm_mlp, norm_ple = (g.reshape(DEPTH, 1, D_MODEL)
                                    for g in (norm_mix, norm_mlp, norm_ple))
    hp = x_prompt.reshape(SEQ, D_MODEL)
    hs = x_sample.reshape(N_SAMPLE, D_MODEL)
    p_sample = p_sample.reshape(DEPTH, N_SAMPLE, PLE_DIM)
    k_p, v_p, s_p, k_s, v_s = [], [], [], [], []
    state_s = None
    sink = attn_sink.astype(F32).reshape(DEPTH, N_HEADS, 1, 1) * LOG2E
    sink_p = _pair_layout(jnp.broadcast_to(sink, (DEPTH, N_HEADS, SINK_ROWS, PAIR)))
    sink_s = sink_p
    for l in range(DEPTH):
        gain = ret_gain[l].reshape(1, RET_WIDTH)

        qa, k, v, kz, vz, qr, kr, vr, gr, w_out_l = _in_proj(hp, hs, norm_mix, w_in_bf[l], l,
                                                             cos2, sin2, (w_out, l))
        oa_s = _attn_sample(qa, kz, vz, cache_k[l].reshape(DEC_BATCH, WINDOW, KV_WIDTH),
                            cache_v[l].reshape(DEC_BATCH, WINDOW, KV_WIDTH), bias_s, sink_s, l)
        or_s, state_s = _ret_sample(qr, kr, vr, gr, state_ret, l, consts_s, gain, state_s)
        h1, xn, st_p = _mix_out(qa, kz, vz, bias_p, sink_p, qr, kr, vr, gr, consts_p, gain,
                                oa_s, or_s, hp, hs, w_out_l, norm_mlp, l)
        jobs = [(w_gate, l)]
        if l + 1 < DEPTH:
            jobs.append((w_in, l + 1))
        mlp_out, *cast = _mlp(xn, w_up, w_down, l, jobs)
        w_gate_bf[l] = cast[0]
        if l + 1 < DEPTH:
            w_in_bf[l + 1] = cast[1]
        hp, hs = _gate(h1, mlp_out, p_prompt, p_sample, l, w_gate_bf[l], w_ple, norm_ple,
                       norm_final.reshape(1, D_MODEL), final=(l == DEPTH - 1))

        k_p.append(k[SEQ - WINDOW:SEQ].reshape(1, WINDOW, N_KV, HEAD_DIM))
        v_p.append(v[SEQ - WINDOW:SEQ].reshape(1, WINDOW, N_KV, HEAD_DIM))
        s_p.append(st_p[None])
        k_s.append(k[SEQ:].reshape(DEC_BATCH, DEC_SEQ, N_KV, HEAD_DIM))
        v_s.append(v[SEQ:].reshape(DEC_BATCH, DEC_SEQ, N_KV, HEAD_DIM))

    return (hp.reshape(1, SEQ, D_MODEL), hs.reshape(DEC_BATCH, DEC_SEQ, D_MODEL),
            jnp.stack(k_p), jnp.stack(v_p), jnp.stack(s_p),
            jnp.stack(k_s), jnp.stack(v_s), state_s)
```
